```python
import jax
import jax.numpy as jnp
from jax import lax
import numpy as np

D_MODEL = 1024
BATCH = 8
SEQ = 2048
DEPTH = 2
DEC_BATCH = 128
DEC_SEQ = 8
PAST_LEN = 16384
PAGE_SIZE = 128

N_MIXERS = 2
N_DELTA_LAYERS = (DEPTH + 1) // 2
N_SSM_LAYERS = DEPTH // 2
GDN_HEADS = 8
GDN_DK = 128
GDN_DV = 128
GDN_QK_DIM = GDN_HEADS * GDN_DK
GDN_V_DIM = GDN_HEADS * GDN_DV
GDN_CONV_DIM = 2 * GDN_QK_DIM + GDN_V_DIM
GDN_IN = GDN_CONV_DIM + GDN_V_DIM + 2 * GDN_HEADS
CONV_W = 4
CHUNK = 64
SSM_GROUP = 16
SSM_GROUPS = D_MODEL // SSM_GROUP
SSM_P = 64
N_GROUPS = 4
EXPERTS_PER_GROUP = 4
N_EXPERTS = N_GROUPS * EXPERTS_PER_GROUP
TOP_K = 2
D_FF_EXPERT = 256
EPS = 1e-6

kernel_name = "hybrid_gdn_s5_hmoe_step"

F32 = jnp.float32


def rmsnorm(x, w):
    x32 = x.astype(F32)
    y = x32 * lax.rsqrt(jnp.mean(x32 * x32, axis=-1, keepdims=True) + EPS)
    return (y * w.astype(F32)).astype(x.dtype)


def l2norm(x):
    x32 = x.astype(F32)
    return x32 * lax.rsqrt(jnp.sum(x32 * x32, axis=-1, keepdims=True) + EPS)


def ada_mod(c, w_mod, b_mod):
    m = (jax.nn.silu(c) @ w_mod + b_mod)[:, None, :]
    shift, scale, gate = jnp.split(m, 3, axis=-1)
    return shift, scale, gate


def causal_short_conv(x, buf, w):
    L = x.shape[1]
    xc = jnp.concatenate([buf.astype(x.dtype), x], axis=1)
    y = xc[:, 0:L] * w[0]
    for j in range(1, CONV_W):
        y = y + xc[:, j:j + L] * w[j]
    return jax.nn.silu(y), xc[:, -(CONV_W - 1):]


def gated_delta_rule(q, k, v, g, beta, s0):
    B, L, H, DK = q.shape
    DV = v.shape[-1]
    C = min(CHUNK, L)
    n = -(-L // C)
    pad = n * C - L

    def prep(t):
        t = t.astype(F32)
        t = jnp.pad(t, [(0, 0), (0, pad)] + [(0, 0)] * (t.ndim - 2))
        t = jnp.moveaxis(t, 2, 1)
        return t.reshape((B, H, n, C) + t.shape[3:])

    q, k, v, g, beta = prep(q), prep(k), prep(v), prep(g), prep(beta)
    g = jnp.cumsum(g, axis=-1)
    k_beta = k * beta[..., None]
    v_beta = v * beta[..., None]
    tril = jnp.tril(jnp.ones((C, C), bool))
    strict = jnp.tril(jnp.ones((C, C), bool), -1)
    decay = jnp.exp(jnp.where(tril, g[..., :, None] - g[..., None, :], -jnp.inf))
    a = jnp.where(strict, jnp.einsum('bhnik,bhnjk->bhnij', k_beta, k) * decay, 0.0)
    eye = jnp.eye(C, dtype=F32)
    rhs = jnp.concatenate([v_beta, k_beta * jnp.exp(g)[..., None]], axis=-1)
    sol = lax.linalg.triangular_solve(a + eye, rhs, left_side=True, lower=True, unit_diagonal=True)
    u, w = sol[..., :DV], sol[..., DV:]
    qk = jnp.where(tril, jnp.einsum('bhnik,bhnjk->bhnij', q, k) * decay, 0.0)
    q_dec = q * jnp.exp(g)[..., None]
    k_dec = k * jnp.exp(g[..., -1:] - g)[..., None]
    g_last = jnp.exp(g[..., -1])

    def step(s, xs):
        u_c, w_c, qk_c, qd_c, kd_c, gl_c = xs
        v_new = u_c - w_c @ s
        o = qd_c @ s + qk_c @ v_new
        s = s * gl_c[..., None, None] + jnp.swapaxes(kd_c, -1, -2) @ v_new
        return s, o

    xs = tuple(jnp.moveaxis(t, 2, 0) for t in (u, w, qk, q_dec, k_dec, g_last))
    s, o = lax.scan(step, s0.astype(F32), xs)
    o = jnp.moveaxis(o, 0, 2).reshape(B, H, n * C, DV)[:, :, :L]
    return jnp.moveaxis(o, 1, 2), s


def gdn_mixer(h, conv_buf, s0, w_in, conv_w, a_log, dt_bias, o_norm, w_out):
    B, L, _ = h.shape
    proj = h @ w_in
    i1 = GDN_CONV_DIM
    i2 = i1 + GDN_V_DIM
    i3 = i2 + GDN_HEADS
    qkv, z, b_raw, a_raw = jnp.split(proj, [i1, i2, i3], axis=-1)
    qkv, conv_new = causal_short_conv(qkv, conv_buf, conv_w)
    q, k, v = jnp.split(qkv, [GDN_QK_DIM, 2 * GDN_QK_DIM], axis=-1)
    q = l2norm(q.reshape(B, L, GDN_HEADS, GDN_DK)) * (GDN_DK ** -0.5)
    k = l2norm(k.reshape(B, L, GDN_HEADS, GDN_DK))
    v = v.reshape(B, L, GDN_HEADS, GDN_DV)
    beta = jax.nn.sigmoid(b_raw.astype(F32))
    g = -jnp.exp(a_log.astype(F32)) * jax.nn.softplus(a_raw.astype(F32) + dt_bias.astype(F32))
    o, s_new = gated_delta_rule(q, k, v, g, beta, s0)
    o = rmsnorm(o, o_norm) * jax.nn.silu(z.reshape(B, L, GDN_HEADS, GDN_DV).astype(F32))
    out = o.reshape(B, L, GDN_V_DIM).astype(h.dtype) @ w_out
    return out, conv_new, s_new.astype(s0.dtype)


def s5_mixer(h, s_re0, s_im0, w_in, lam_re, lam_im, log_dt, b_re, b_im, c_re, c_im, d_skip, w_glu):
    B, L, _ = h.shape
    u = (h @ w_in).astype(F32).reshape(B, L, SSM_GROUPS, SSM_GROUP)
    dt = jnp.exp(log_dt.astype(F32))[:, None]
    lr, li = lam_re.astype(F32), lam_im.astype(F32)
    mag = jnp.exp(lr * dt)
    ang = li * dt
    ab_re, ab_im = mag * jnp.cos(ang), mag * jnp.sin(ang)
    den = lr * lr + li * li
    f_re = ((ab_re - 1.0) * lr + ab_im * li) / den
    f_im = (ab_im * lr - (ab_re - 1.0) * li) / den
    br, bi = b_re.astype(F32), b_im.astype(F32)
    bb_re = f_re[..., None] * br - f_im[..., None] * bi
    bb_im = f_re[..., None] * bi + f_im[..., None] * br
    x_re = jnp.einsum('blgc,gpc->blgp', u, bb_re)
    x_im = jnp.einsum('blgc,gpc->blgp', u, bb_im)
    s_re0 = s_re0.astype(F32)
    s_im0 = s_im0.astype(F32)
    x_re = x_re.at[:, 0].add(ab_re * s_re0 - ab_im * s_im0)
    x_im = x_im.at[:, 0].add(ab_re * s_im0 + ab_im * s_re0)
    a_re = jnp.broadcast_to(ab_re, x_re.shape)
    a_im = jnp.broadcast_to(ab_im, x_im.shape)

    def combine(e1, e2):
        a1r, a1i, b1r, b1i = e1
        a2r, a2i, b2r, b2i = e2
        return (a1r * a2r - a1i * a2i, a1r * a2i + a1i * a2r,
                a2r * b1r - a2i * b1i + b2r, a2r * b1i + a2i * b1r + b2i)

    _, _, st_re, st_im = lax.associative_scan(combine, (a_re, a_im, x_re, x_im), axis=1)
    y = (jnp.einsum('blgp,gcp->blgc', st_re, c_re.astype(F32))
         - jnp.einsum('blgp,gcp->blgc', st_im, c_im.astype(F32))
         + d_skip.astype(F32) * u)
    y = jax.nn.gelu(y.reshape(B, L, D_MODEL)).astype(h.dtype)
    a, gt = jnp.split(y @ w_glu, 2, axis=-1)
    out = a * jax.nn.sigmoid(gt)
    return out, st_re[:, -1].astype(h.dtype), st_im[:, -1].astype(h.dtype)


def hier_moe(h, w_rg, b_rg, w_re, b_re, w1, w3, w2):
    shp = h.shape
    t = h.reshape(-1, D_MODEL)
    lg = (t @ w_rg + b_rg).astype(F32)
    pg = jax.nn.softmax(lg, axis=-1)
    g_idx = jnp.argmax(lg, axis=-1)
    p_group = jnp.take_along_axis(pg, g_idx[:, None], axis=-1)
    le = (t @ w_re + b_re).astype(F32).reshape(-1, N_GROUPS, EXPERTS_PER_GROUP)
    le = jnp.take_along_axis(le, g_idx[:, None, None], axis=1)[:, 0]
    pe = jax.nn.softmax(le, axis=-1)
    top_p, top_i = lax.top_k(pe, TOP_K)
    top_p = top_p / jnp.sum(top_p, axis=-1, keepdims=True) * p_group
    e_idx = g_idx[:, None] * EXPERTS_PER_GROUP + top_i
    comb = jnp.sum(jax.nn.one_hot(e_idx, N_EXPERTS, dtype=F32) * top_p[..., None], axis=1)
    hid = jax.nn.silu(jnp.einsum('td,edf->tef', t, w1)) * jnp.einsum('td,edf->tef', t, w3)
    hid = hid * comb.astype(hid.dtype)[..., None]
    out = jnp.einsum('tef,efd->td', hid, w2)
    return out.reshape(shp)


def run_trunk(x, c, conv0, delta0, ssm_re0, ssm_im0,
              norm_mix, w_mod_mix, b_mod_mix, norm_ffn, w_mod_ffn, b_mod_ffn, norm_final,
              gdn_w_in, gdn_conv_w, gdn_a_log, gdn_dt_bias, gdn_o_norm, gdn_w_out,
              s5_w_in, s5_lam_re, s5_lam_im, s5_log_dt, s5_b_re, s5_b_im, s5_c_re, s5_c_im, s5_d, s5_w_glu,
              moe_w_rg, moe_b_rg, moe_w_re, moe_b_re, moe_w1, moe_w3, moe_w2):
    new_conv, new_delta, new_re, new_im = [], [], [], []
    for i in range(DEPTH):
        j = i // N_MIXERS
        shift, scale, gate = ada_mod(c, w_mod_mix[i], b_mod_mix[i])
        hn = rmsnorm(x, norm_mix[i]) * (1.0 + scale) + shift
        if i % N_MIXERS == 0:
            out, cv, sd = gdn_mixer(hn, conv0[j], delta0[j], gdn_w_in[j], gdn_conv_w[j], gdn_a_log[j],
                                    gdn_dt_bias[j], gdn_o_norm[j], gdn_w_out[j])
            new_conv.append(cv)
            new_delta.append(sd)
        else:
            out, sr, si = s5_mixer(hn, ssm_re0[j], ssm_im0[j], s5_w_in[j], s5_lam_re[j], s5_lam_im[j],
                                   s5_log_dt[j], s5_b_re[j], s5_b_im[j], s5_c_re[j], s5_c_im[j],
                                   s5_d[j], s5_w_glu[j])
            new_re.append(sr)
            new_im.append(si)
        x = x + gate * out
        shift, scale, gate = ada_mod(c, w_mod_ffn[i], b_mod_ffn[i])
        hn = rmsnorm(x, norm_ffn[i]) * (1.0 + scale) + shift
        x = x + gate * hier_moe(hn, moe_w_rg[i], moe_b_rg[i], moe_w_re[i], moe_b_re[i],
                                moe_w1[i], moe_w3[i], moe_w2[i])
    y = rmsnorm(x, norm_final)
    return y, jnp.stack(new_delta), jnp.stack(new_conv), jnp.stack(new_re), jnp.stack(new_im)


def setup_inputs(seed: int = 0) -> dict:
    key = jax.random.key(seed)
    ks = iter(jax.random.split(key, 64))
    nrm = lambda shape, s: jax.random.normal(next(ks), shape, F32) * s
    D, NA, NB = D_MODEL, N_DELTA_LAYERS, N_SSM_LAYERS
    G, P = SSM_GROUPS, SSM_P
    inp = {}
    inp['x_prompt'] = nrm((BATCH, SEQ, D), 1.0)
    inp['x_sample'] = nrm((DEC_BATCH, DEC_SEQ, D), 1.0)
    inp['state_delta'] = nrm((NA, DEC_BATCH, GDN_HEADS, GDN_DK, GDN_DV), GDN_DK ** -0.5)
    inp['state_conv'] = nrm((NA, DEC_BATCH, CONV_W - 1, GDN_CONV_DIM), 1.0)
    inp['state_ssm_re'] = nrm((NB, DEC_BATCH, G, P), 0.3)
    inp['state_ssm_im'] = nrm((NB, DEC_BATCH, G, P), 0.3)
    inp['c_prompt'] = nrm((BATCH, D), 1.0)
    inp['c_sample'] = nrm((DEC_BATCH, D), 1.0)
    inp['norm_mix'] = 1.0 + nrm((DEPTH, D), 0.05)
    inp['w_mod_mix'] = nrm((DEPTH, D, 3 * D), 0.5 * D ** -0.5)
    inp['b_mod_mix'] = nrm((DEPTH, 3 * D), 0.01)
    inp['norm_ffn'] = 1.0 + nrm((DEPTH, D), 0.05)
    inp['w_mod_ffn'] = nrm((DEPTH, D, 3 * D), 0.5 * D ** -0.5)
    inp['b_mod_ffn'] = nrm((DEPTH, 3 * D), 0.01)
    inp['norm_final'] = 1.0 + nrm((D,), 0.05)
    inp['gdn_w_in'] = nrm((NA, D, GDN_IN), D ** -0.5)
    inp['gdn_conv_w'] = nrm((NA, CONV_W, GDN_CONV_DIM), CONV_W ** -0.5)
    inp['gdn_a_log'] = jnp.log(jax.random.uniform(next(ks), (NA, GDN_HEADS), F32, 1.0, 16.0))
    dt = jnp.exp(jax.random.uniform(next(ks), (NA, GDN_HEADS), F32, np.log(1e-3), np.log(1e-1)))
    inp['gdn_dt_bias'] = dt + jnp.log(-jnp.expm1(-dt))
    inp['gdn_o_norm'] = 1.0 + nrm((NA, GDN_DV), 0.05)
    inp['gdn_w_out'] = nrm((NA, GDN_V_DIM, D), GDN_V_DIM ** -0.5)
    inp['s5_w_in'] = nrm((NB, D, D), D ** -0.5)
    inp['s5_lam_re'] = -0.5 + nrm((NB, G, P), 0.01)
    inp['s5_lam_im'] = np.pi * jnp.arange(P, dtype=F32) + nrm((NB, G, P), 0.01)
    inp['s5_log_dt'] = jax.random.uniform(next(ks), (NB, G), F32, np.log(1e-3), np.log(1e-1))
    inp['s5_b_re'] = nrm((NB, G, P, SSM_GROUP), (2 * SSM_GROUP) ** -0.5)
    inp['s5_b_im'] = nrm((NB, G, P, SSM_GROUP), (2 * SSM_GROUP) ** -0.5)
    inp['s5_c_re'] = nrm((NB, G, SSM_GROUP, P), P ** -0.5)
    inp['s5_c_im'] = nrm((NB, G, SSM_GROUP, P), P ** -0.5)
    inp['s5_d'] = nrm((NB, G, SSM_GROUP), 1.0)
    inp['s5_w_glu'] = nrm((NB, D, 2 * D), D ** -0.5)
    inp['moe_w_rg'] = nrm((DEPTH, D, N_GROUPS), D ** -0.5)
    inp['moe_b_rg'] = nrm((DEPTH, N_GROUPS), 0.01)
    inp['moe_w_re'] = nrm((DEPTH, D, N_EXPERTS), D ** -0.5)
    inp['moe_b_re'] = nrm((DEPTH, N_EXPERTS), 0.01)
    inp['moe_w1'] = nrm((DEPTH, N_EXPERTS, D, D_FF_EXPERT), D ** -0.5)
    inp['moe_w3'] = nrm((DEPTH, N_EXPERTS, D, D_FF_EXPERT), D ** -0.5)
    inp['moe_w2'] = nrm((DEPTH, N_EXPERTS, D_FF_EXPERT, D), D_FF_EXPERT ** -0.5)
    return inp


def reference(x_prompt, x_sample, state_delta, state_conv, state_ssm_re, state_ssm_im, c_prompt, c_sample,
              norm_mix, w_mod_mix, b_mod_mix, norm_ffn, w_mod_ffn, b_mod_ffn, norm_final,
              gdn_w_in, gdn_conv_w, gdn_a_log, gdn_dt_bias, gdn_o_norm, gdn_w_out,
              s5_w_in, s5_lam_re, s5_lam_im, s5_log_dt, s5_b_re, s5_b_im, s5_c_re, s5_c_im, s5_d, s5_w_glu,
              moe_w_rg, moe_b_rg, moe_w_re, moe_b_re, moe_w1, moe_w3, moe_w2):
    weights = (norm_mix, w_mod_mix, b_mod_mix, norm_ffn, w_mod_ffn, b_mod_ffn, norm_final,
               gdn_w_in, gdn_conv_w, gdn_a_log, gdn_dt_bias, gdn_o_norm, gdn_w_out,
               s5_w_in, s5_lam_re, s5_lam_im, s5_log_dt, s5_b_re, s5_b_im, s5_c_re, s5_c_im, s5_d, s5_w_glu,
               moe_w_rg, moe_b_rg, moe_w_re, moe_b_re, moe_w1, moe_w3, moe_w2)
    bp = x_prompt.shape[0]
    p_conv0 = jnp.zeros((N_DELTA_LAYERS, bp, CONV_W - 1, GDN_CONV_DIM), state_conv.dtype)
    p_delta0 = jnp.zeros((N_DELTA_LAYERS, bp, GDN_HEADS, GDN_DK, GDN_DV), state_delta.dtype)
    p_re0 = jnp.zeros((N_SSM_LAYERS, bp, SSM_GROUPS, SSM_P), state_ssm_re.dtype)
    p_im0 = jnp.zeros((N_SSM_LAYERS, bp, SSM_GROUPS, SSM_P), state_ssm_im.dtype)
    y_prompt, p_delta, p_conv, p_ssm_re, p_ssm_im = run_trunk(
        x_prompt, c_prompt, p_conv0, p_delta0, p_re0, p_im0, *weights)
    y_sample, s_delta, s_conv, s_ssm_re, s_ssm_im = run_trunk(
        x_sample, c_sample, state_conv, state_delta, state_ssm_re, state_ssm_im, *weights)
    return (y_prompt, y_sample, p_delta, p_conv, p_ssm_re, p_ssm_im, s_delta, s_conv, s_ssm_re, s_ssm_im)
```

```python
import functools

import jax
import jax.numpy as jnp
from jax import lax
from jax.experimental import pallas as pl
from jax.experimental.pallas import tpu as pltpu

F32 = jnp.float32
BF16 = jnp.bfloat16

D_MODEL = 1024
N_HEADS = 8
HEAD_DIM = 128
QK_DIM = N_HEADS * HEAD_DIM
CONV_DIM = 3 * QK_DIM
CONV_TAPS = 4
CHUNK = 64
SSM_GROUPS = 64
SSM_GROUP = 16
SSM_P = 64
SSM_STATE = SSM_GROUPS * SSM_P
N_GROUPS = 4
EXPERTS_PER_GROUP = 4
N_EXPERTS = 16
D_FF = 256
EPS = 1e-6

LANES = 128
SUBLANES = 8
MXU_DIM = 256
ROWS = 256
SCAN_LANES = 512
VMEM_LIMIT = 56 * 1024 * 1024


def _mm(a, b):
    return jnp.dot(a.astype(BF16), b.astype(BF16), preferred_element_type=F32)


def _mm_nt(a, b):
    return lax.dot_general(a.astype(BF16), b.astype(BF16), (((1,), (1,)), ((), ())),
                           preferred_element_type=F32)


def _mm_tn(a, b):
    return lax.dot_general(a.astype(BF16), b.astype(BF16), (((0,), (0,)), ((), ())),
                           preferred_element_type=F32)


def _split_bf16(x, terms):
    out = []
    for _ in range(terms - 1):
        p = x.astype(BF16)
        out.append(p)
        x = x - p.astype(F32)
    out.append(x.astype(BF16))
    return out


def _rms(x, w):
    return x * lax.rsqrt(jnp.mean(x * x, axis=-1, keepdims=True) + EPS) * w


def _silu(x):
    return x * jax.nn.sigmoid(x)


def _softplus(x):
    return jnp.maximum(x, 0.0) + jnp.log1p(jnp.exp(-jnp.abs(x)))


def _lane_col(tile, lane):
    li = lax.broadcasted_iota(jnp.int32, tile.shape, 1)
    return jnp.sum(jnp.where(li == lane, tile, 0.0), axis=1, keepdims=True)


def _log2(n):
    assert n & (n - 1) == 0
    return n.bit_length() - 1


def _params(*sem):
    return pltpu.CompilerParams(dimension_semantics=sem, vmem_limit_bytes=VMEM_LIMIT)


def _mod_body(c_ref, w_ref, b_ref, o_ref):
    o_ref[0] = _mm(_silu(c_ref[...]), w_ref[0]) + b_ref[0]


def _ada_mod(c_all, w_mod, b_mod):
    nl, nr, tn = w_mod.shape[0], c_all.shape[0], 768
    return pl.pallas_call(
        _mod_body,
        grid=(nl, 3 * D_MODEL // tn),
        in_specs=[pl.BlockSpec((nr, D_MODEL), lambda l, j: (0, 0)),
                  pl.BlockSpec((1, D_MODEL, tn), lambda l, j: (l, 0, j)),
                  pl.BlockSpec((1, 1, tn), lambda l, j: (l, 0, j))],
        out_specs=pl.BlockSpec((1, nr, tn), lambda l, j: (l, 0, j)),
        out_shape=jax.ShapeDtypeStruct((nl, nr, 3 * D_MODEL), F32),
        compiler_params=_params("parallel", "parallel"),
        name="ada_mod",
    )(c_all, w_mod, b_mod.reshape(nl, 1, 3 * D_MODEL))


def _gdn_front_body(x_ref, sh_ref, sc_ref, nw_ref, wq_ref, wba_ref, alog_ref, dtb_ref,
                    qkv_ref, z_ref, gt_ref, *, chunk):
    x = x_ref[...]
    nb, tt, _ = x.shape
    r = nb * tt
    h = _rms(x, nw_ref[...]) * (1.0 + sc_ref[...]) + sh_ref[...]
    hb = h.reshape(r, D_MODEL).astype(BF16)
    pq = jnp.dot(hb, wq_ref[...], preferred_element_type=F32)
    qkv_ref[...] = pq[:, :CONV_DIM].reshape(nb, tt, CONV_DIM)
    z_ref[...] = pq[:, CONV_DIM:].reshape(nb, tt, QK_DIM)
    ba = jnp.dot(hb, wba_ref[...], preferred_element_type=F32)
    beta = jax.nn.sigmoid(ba)
    g = -jnp.exp(alog_ref[...]) * _softplus(ba + dtb_ref[...])
    ri = lax.broadcasted_iota(jnp.int32, (r, r), 0)
    ci = lax.broadcasted_iota(jnp.int32, (r, r), 1)
    same = (ri >> _log2(chunk)) == (ci >> _log2(chunk))
    low = jnp.where(same & (ri >= ci), 1.0, 0.0).astype(BF16)
    ones = jnp.where(same, 1.0, 0.0).astype(BF16)
    gc = jnp.zeros((r, LANES), F32)
    gl = jnp.zeros((r, LANES), F32)
    for piece in _split_bf16(g, 3):
        gc = gc + jnp.dot(low, piece, preferred_element_type=F32)
        gl = gl + jnp.dot(ones, piece, preferred_element_type=F32)
    gt_ref[...] = jnp.concatenate([beta, gc, gl], axis=-1).reshape(nb, tt, 3 * LANES)


def _gdn_front(x, shift, scale, nw, wq, wba, alog, dtb, *, nb, tt, chunk):
    b, l, _ = x.shape
    tok = lambda w: pl.BlockSpec((nb, tt, w), lambda i, j: (i, j, 0))
    mod = lambda col: pl.BlockSpec((nb, 1, D_MODEL), lambda i, j: (i, 0, col))
    full = lambda a: pl.BlockSpec(a.shape, lambda i, j: (0,) * a.ndim)
    return pl.pallas_call(
        functools.partial(_gdn_front_body, chunk=chunk),
        grid=(b // nb, l // tt),
        in_specs=[tok(D_MODEL), mod(0), mod(1), full(nw), full(wq), full(wba), full(alog), full(dtb)],
        out_specs=[tok(CONV_DIM), tok(QK_DIM), tok(3 * LANES)],
        out_shape=[jax.ShapeDtypeStruct((b, l, CONV_DIM), F32),
                   jax.ShapeDtypeStruct((b, l, QK_DIM), F32),
                   jax.ShapeDtypeStruct((b, l, 3 * LANES), F32)],
        compiler_params=_params("parallel", "parallel"),
        name="gdn_front",
    )(x, shift, scale, nw, wq, wba, alog, dtb)


def _inv_unit_lower(a, ri, ci, chunk):
    base = min(16, chunk)
    same = lambda size: (ri >> _log2(size)) == (ci >> _log2(size))
    n = jnp.where(same(base), -a, 0.0)
    q, m, k = n, n, 1
    while 2 * k < base:
        m = _mm(m, m)
        q = q + m + _mm(q, m)
        k *= 2
    size = base
    while size < chunk:
        e = jnp.where(same(2 * size) & jnp.logical_not(same(size)), a, 0.0)
        x = e + _mm(q, e)
        q = q - (x + _mm(x, q))
        size *= 2
    return q


def _gdn_delta_body(*refs, nb, tt, chunk, has_state):
    if has_state:
        (qkv_ref, z_ref, gt_ref, cw_ref, on_ref, s0_ref, c0_ref,
         o_ref, s_ref, cn_ref, xc, qkv_s, o_s) = refs
    else:
        (qkv_ref, z_ref, gt_ref, cw_ref, on_ref,
         o_ref, s_ref, cn_ref, xc, qkv_s, o_s) = refs
    hist = CONV_TAPS - 1
    pad = SUBLANES

    @pl.when(pl.program_id(1) == 0)
    def _():
        if has_state:
            s_ref[...] = s0_ref[...]
            xc[:, pad - hist:pad, :] = c0_ref[...]
        else:
            s_ref[...] = jnp.zeros(s_ref.shape, F32)
            xc[:, pad - hist:pad, :] = jnp.zeros((nb, hist, CONV_DIM), F32)

    xc[:, pad:pad + tt, :] = qkv_ref[...]
    cw = cw_ref[...]
    y = xc[:, pad - hist:pad - hist + tt, :] * cw[0:1]
    for j in range(1, CONV_TAPS):
        y = y + xc[:, pad - hist + j:pad - hist + j + tt, :] * cw[j:j + 1]
    qkv_s[...] = _silu(y)
    last = xc[:, pad + tt - hist:pad + tt, :]
    cn_ref[...] = last
    xc[:, pad - hist:pad, :] = last

    for hd in range(N_HEADS):
        for part, scl in ((0, HEAD_DIM ** -0.5), (1, 1.0)):
            lo = part * QK_DIM + hd * HEAD_DIM
            xh = qkv_s[:, :, lo:lo + HEAD_DIM]
            xh = xh * (lax.rsqrt(jnp.sum(xh * xh, axis=-1, keepdims=True) + EPS) * scl)
            qkv_s[:, :, lo:lo + HEAD_DIM] = xh

    per_stack = MXU_DIM // chunk
    units_all = [(s, hd) for s in range(nb) for hd in range(N_HEADS)]
    assert len(units_all) % per_stack == 0
    stacks = [units_all[i:i + per_stack] for i in range(0, len(units_all), per_stack)]
    r = MXU_DIM
    ri = lax.broadcasted_iota(jnp.int32, (r, r), 0)
    ci = lax.broadcasted_iota(jnp.int32, (r, r), 1)
    same = (ri >> _log2(chunk)) == (ci >> _log2(chunk))
    tril = same & (ri >= ci)
    strict = same & (ri > ci)
    eye = ri == ci

    def chunk_step(c, carry):
        r0 = c * chunk if isinstance(c, int) else pl.multiple_of(c * chunk, chunk)
        rows = pl.ds(r0, chunk)
        for units in stacks:
            cat = lambda f: jnp.concatenate([f(s, hd) for s, hd in units], axis=0)
            qn = cat(lambda s, hd: qkv_s[s, rows, hd * HEAD_DIM:(hd + 1) * HEAD_DIM])
            kn = cat(lambda s, hd: qkv_s[s, rows, QK_DIM + hd * HEAD_DIM:QK_DIM + (hd + 1) * HEAD_DIM])
            v = cat(lambda s, hd: qkv_s[s, rows, 2 * QK_DIM + hd * HEAD_DIM:2 * QK_DIM + (hd + 1) * HEAD_DIM])
            beta = cat(lambda s, hd: _lane_col(gt_ref[s, rows, 0:LANES], hd))
            gc = cat(lambda s, hd: _lane_col(gt_ref[s, rows, LANES:2 * LANES], N_HEADS + hd))
            gl = cat(lambda s, hd: _lane_col(gt_ref[s, rows, 2 * LANES:3 * LANES], N_HEADS + hd))
            eg = jnp.exp(gc)
            kb = kn * beta
            gc_row = jnp.sum(jnp.where(eye, gc, 0.0), axis=0, keepdims=True)
            decay = jnp.where(tril, jnp.exp(jnp.where(tril, gc - gc_row, 0.0)), 0.0)
            a = jnp.where(strict, _mm_nt(kb, kn) * decay, 0.0)
            q_inv = _inv_unit_lower(a, ri, ci, chunk)
            rhs = jnp.concatenate([v * beta, kb * eg], axis=1)
            uw = rhs + _mm(q_inv, rhs)
            u, w = uw[:, :HEAD_DIM], uw[:, HEAD_DIM:]
            qk = jnp.where(tril, _mm_nt(qn, kn) * decay, 0.0)
            qd = qn * eg
            kd = kn * jnp.exp(gl - gc)
            ws, qs = [], []
            for i, (s, hd) in enumerate(units):
                sl = slice(i * chunk, (i + 1) * chunk)
                both = _mm(jnp.concatenate([w[sl], qd[sl]], axis=0), s_ref[s, hd])
                ws.append(both[:chunk])
                qs.append(both[chunk:])
            v_new = u - jnp.concatenate(ws, axis=0)
            o = jnp.concatenate(qs, axis=0) + _mm(qk, v_new)
            for i, (s, hd) in enumerate(units):
                sl = slice(i * chunk, (i + 1) * chunk)
                o_s[s, rows, hd * HEAD_DIM:(hd + 1) * HEAD_DIM] = o[sl]
                g_last = jnp.exp(gl[i * chunk:i * chunk + 1])
                s_ref[s, hd] = s_ref[s, hd] * g_last + _mm_tn(kd[sl], v_new[sl])
        return carry

    n_chunks = tt // chunk
    if n_chunks == 1:
        chunk_step(0, 0)
    else:
        lax.fori_loop(0, n_chunks, chunk_step, 0)

    on = on_ref[...]
    for hd in range(N_HEADS):
        sl = slice(hd * HEAD_DIM, (hd + 1) * HEAD_DIM)
        o_ref[:, :, sl] = _rms(o_s[:, :, sl], on) * _silu(z_ref[:, :, sl])


def _gdn_delta(qkv, z, gates, cw, onorm, s0, c0, *, nb, tt, chunk):
    b, l, _ = qkv.shape
    has_state = s0 is not None
    tok = lambda w: pl.BlockSpec((nb, tt, w), lambda i, j: (i, j, 0))
    full = lambda a: pl.BlockSpec(a.shape, lambda i, j: (0,) * a.ndim)
    st_spec = pl.BlockSpec((nb, N_HEADS, HEAD_DIM, HEAD_DIM), lambda i, j: (i, 0, 0, 0))
    cv_spec = pl.BlockSpec((nb, CONV_TAPS - 1, CONV_DIM), lambda i, j: (i, 0, 0))
    in_specs = [tok(CONV_DIM), tok(QK_DIM), tok(3 * LANES), full(cw), full(onorm)]
    args = [qkv, z, gates, cw, onorm]
    if has_state:
        in_specs += [st_spec, cv_spec]
        args += [s0, c0]
    return pl.pallas_call(
        functools.partial(_gdn_delta_body, nb=nb, tt=tt, chunk=chunk, has_state=has_state),
        grid=(b // nb, l // tt),
        in_specs=in_specs,
        out_specs=[tok(QK_DIM), st_spec, cv_spec],
        out_shape=[jax.ShapeDtypeStruct((b, l, QK_DIM), F32),
                   jax.ShapeDtypeStruct((b, N_HEADS, HEAD_DIM, HEAD_DIM), F32),
                   jax.ShapeDtypeStruct((b, CONV_TAPS - 1, CONV_DIM), F32)],
        scratch_shapes=[pltpu.VMEM((nb, tt + SUBLANES, CONV_DIM), F32),
                        pltpu.VMEM((nb, tt, CONV_DIM), F32),
                        pltpu.VMEM((nb, tt, QK_DIM), F32)],
        compiler_params=_params("parallel", "arbitrary"),
        name="gdn_delta",
    )(*args)


def _route(logits):
    lane = lax.broadcasted_iota(jnp.int32, logits.shape, 1)
    neg, big = -1e30, 1 << 20
    rmax = lambda t: jnp.max(t, axis=1, keepdims=True)
    rsum = lambda t: jnp.sum(t, axis=1, keepdims=True)
    first = lambda cond: jnp.min(jnp.where(cond, lane, big), axis=1, keepdims=True)
    is_g = lane < N_GROUPS
    lg = jnp.where(is_g, logits, neg)
    mg = rmax(lg)
    g_idx = first(is_g & (lg >= mg))
    p_group = 1.0 / rsum(jnp.where(is_g, jnp.exp(lg - mg), 0.0))
    lo = N_GROUPS + EXPERTS_PER_GROUP * g_idx
    in_g = (lane >= lo) & (lane < lo + EXPERTS_PER_GROUP)
    le = jnp.where(in_g, logits, neg)
    m1 = rmax(le)
    se = rsum(jnp.where(in_g, jnp.exp(le - m1), 0.0))
    i1 = first(in_g & (le >= m1))
    le2 = jnp.where(lane == i1, neg, le)
    m2 = rmax(le2)
    i2 = first(in_g & (lane != i1) & (le2 >= m2))
    p1 = 1.0 / se
    p2 = jnp.exp(m2 - m1) / se
    tot = p1 + p2
    return (jnp.where(lane == i1, p1 / tot * p_group, 0.0)
            + jnp.where(lane == i2, p2 / tot * p_group, 0.0))


def _moe_body(*refs, nb, tt, has_proj, final_norm):
    it = iter(refs)
    x_ref = next(it)
    if has_proj:
        y_ref, wo_ref, gm_ref = next(it), next(it), next(it)
    sh_ref, sc_ref, gf_ref, nw_ref, wr_ref, br_ref, w1_ref, w3_ref, w2_ref = (next(it) for _ in range(9))
    if final_norm:
        nf_ref = next(it)
    out_ref, x1_s, h_s, cb_s, acc_s = (next(it) for _ in range(5))
    e = pl.program_id(2)
    r = nb * tt

    @pl.when(e == 0)
    def _():
        x = x_ref[...].reshape(nb, tt, D_MODEL)
        if has_proj:
            x = x + gm_ref[...] * _mm(y_ref[...].reshape(r, D_MODEL), wo_ref[...]).reshape(nb, tt, D_MODEL)
        h = (_rms(x, nw_ref[...]) * (1.0 + sc_ref[...]) + sh_ref[...]).reshape(r, D_MODEL)
        x1_s[...] = x.reshape(r, D_MODEL)
        h_s[...] = h.astype(BF16)
        h_hi, h_lo = _split_bf16(h, 2)
        w_hi, w_lo = _split_bf16(wr_ref[...], 2)
        dot = lambda p, q: jnp.dot(p, q, preferred_element_type=F32)
        logits = dot(h_hi, w_hi) + (dot(h_hi, w_lo) + dot(h_lo, w_hi)) + br_ref[...]
        cb_s[...] = _route(logits)
        acc_s[...] = jnp.zeros((r, D_MODEL), F32)

    hb = h_s[...]
    a = jnp.dot(hb, w1_ref[0], preferred_element_type=F32)
    b = jnp.dot(hb, w3_ref[0], preferred_element_type=F32)
    ce = _lane_col(cb_s[...], e + N_GROUPS)
    hid = _silu(a) * b * ce
    acc_s[...] += jnp.dot(hid.astype(BF16), w2_ref[0], preferred_element_type=F32)

    @pl.when(e == N_EXPERTS - 1)
    def _():
        out = x1_s[...].reshape(nb, tt, D_MODEL) + gf_ref[...] * acc_s[...].reshape(nb, tt, D_MODEL)
        if final_norm:
            out = _rms(out, nf_ref[...])
        out_ref[...] = out.reshape(out_ref.shape)


def _moe(x, y, wo, mod_mix, mod_ffn, nw, wr, br, w1, w3, w2, nf, *, dims, nb, tt,
         in_time_major, out_time_major):
    b, l = dims
    has_proj, final_norm = y is not None, nf is not None
    grid = (b // nb, l // tt, N_EXPERTS)
    bm = pl.BlockSpec((nb, tt, D_MODEL), lambda i, j, e: (i, j, 0))
    tm = pl.BlockSpec((tt, D_MODEL), lambda i, j, e: (j, i))
    mod = lambda col: pl.BlockSpec((nb, 1, D_MODEL), lambda i, j, e: (i, 0, col))
    full = lambda a: pl.BlockSpec(a.shape, lambda i, j, e: (0,) * a.ndim)
    x_spec = tm if in_time_major else bm
    in_specs, args = [x_spec], [x]
    if has_proj:
        in_specs += [x_spec, full(wo), mod(2)]
        args += [y, wo, mod_mix]
    in_specs += [mod(0), mod(1), mod(2), full(nw), full(wr), full(br),
                 pl.BlockSpec((1, D_MODEL, D_FF), lambda i, j, e: (e, 0, 0)),
                 pl.BlockSpec((1, D_MODEL, D_FF), lambda i, j, e: (e, 0, 0)),
                 pl.BlockSpec((1, D_FF, D_MODEL), lambda i, j, e: (e, 0, 0))]
    args += [mod_ffn, mod_ffn, mod_ffn, nw, wr, br, w1, w3, w2]
    if final_norm:
        in_specs.append(full(nf))
        args.append(nf)
    if out_time_major:
        assert nb == 1
        out_spec, out_shape = tm, jax.ShapeDtypeStruct((l, b * D_MODEL), F32)
    else:
        out_spec, out_shape = bm, jax.ShapeDtypeStruct((b, l, D_MODEL), F32)
    r = nb * tt
    return pl.pallas_call(
        functools.partial(_moe_body, nb=nb, tt=tt, has_proj=has_proj, final_norm=final_norm),
        grid=grid,
        in_specs=in_specs,
        out_specs=out_spec,
        out_shape=out_shape,
        scratch_shapes=[pltpu.VMEM((r, D_MODEL), F32), pltpu.VMEM((r, D_MODEL), BF16),
                        pltpu.VMEM((r, LANES), F32), pltpu.VMEM((r, D_MODEL), F32)],
        compiler_params=_params("parallel", "parallel", "arbitrary"),
        name="moe",
    )(*args)


def _s5_body(*refs, tt, nbs, has_state):
    if has_state:
        (x_ref, sh_ref, sc_ref, gm_ref, nw_ref, win_ref, wbr_ref, wbi_ref, wcr_ref, wci_ref, dsk_ref,
         wglu_ref, abr_ref, abi_ref, s0r_ref, s0i_ref, out_ref, sr_ref, si_ref, xr_s, xi_s) = refs
    else:
        (x_ref, sh_ref, sc_ref, gm_ref, nw_ref, win_ref, wbr_ref, wbi_ref, wcr_ref, wci_ref, dsk_ref,
         wglu_ref, abr_ref, abi_ref, out_ref, sr_ref, si_ref, xr_s, xi_s) = refs
    r = tt * nbs
    blocks = D_MODEL // MXU_DIM
    sw = SSM_STATE // blocks

    @pl.when(pl.program_id(1) == 0)
    def _():
        if has_state:
            sr_ref[...] = s0r_ref[...]
            si_ref[...] = s0i_ref[...]
        else:
            sr_ref[...] = jnp.zeros(sr_ref.shape, F32)
            si_ref[...] = jnp.zeros(si_ref.shape, F32)

    x = x_ref[...]
    h = (_rms(x, nw_ref[...]) * (1.0 + sc_ref[...]) + sh_ref[...]).reshape(r, D_MODEL)
    u = _mm(h, win_ref[...])
    ub = u.astype(BF16)
    for j in range(blocks):
        uj = ub[:, j * MXU_DIM:(j + 1) * MXU_DIM]
        xr_s[:, :, j * sw:(j + 1) * sw] = jnp.dot(
            uj, wbr_ref[j], preferred_element_type=F32).reshape(tt, nbs, sw)
        xi_s[:, :, j * sw:(j + 1) * sw] = jnp.dot(
            uj, wbi_ref[j], preferred_element_type=F32).reshape(tt, nbs, sw)

    for rg in range(nbs // SUBLANES):
        rows = slice(rg * SUBLANES, (rg + 1) * SUBLANES)
        for ch in range(SSM_STATE // SCAN_LANES):
            lanes = slice(ch * SCAN_LANES, (ch + 1) * SCAN_LANES)
            ar = jnp.broadcast_to(abr_ref[:, lanes], (SUBLANES, SCAN_LANES))
            ai = jnp.broadcast_to(abi_ref[:, lanes], (SUBLANES, SCAN_LANES))

            def step(t, hc):
                hr, hi = hc
                nr = ar * hr - ai * hi + xr_s[t, rows, lanes]
                ni = ar * hi + ai * hr + xi_s[t, rows, lanes]
                xr_s[t, rows, lanes] = nr
                xi_s[t, rows, lanes] = ni
                return nr, ni

            hr, hi = lax.fori_loop(0, tt, step, (sr_ref[rows, lanes], si_ref[rows, lanes]))
            sr_ref[rows, lanes] = hr
            si_ref[rows, lanes] = hi

    ys = []
    for j in range(blocks):
        st_r = xr_s[:, :, j * sw:(j + 1) * sw].reshape(r, sw)
        st_i = xi_s[:, :, j * sw:(j + 1) * sw].reshape(r, sw)
        ys.append(_mm(st_r, wcr_ref[j]) - _mm(st_i, wci_ref[j]))
    y = jnp.concatenate(ys, axis=1) + dsk_ref[...] * u
    ag = _mm(jax.nn.gelu(y), wglu_ref[...])
    mix = ag[:, :D_MODEL] * jax.nn.sigmoid(ag[:, D_MODEL:])
    out_ref[...] = x + gm_ref[...] * mix.reshape(tt, nbs, D_MODEL)


def _s5(x, mod, nw, win, wbr, wbi, wcr, wci, dsk, wglu, abr, abi, s0r, s0i, *, tt, nbs):
    l, b, _ = x.shape
    has_state = s0r is not None
    xs = pl.BlockSpec((tt, nbs, D_MODEL), lambda i, j: (j, i, 0))
    mods = lambda col: pl.BlockSpec((nbs, D_MODEL), lambda i, j: (i, col))
    full = lambda a: pl.BlockSpec(a.shape, lambda i, j: (0,) * a.ndim)
    st = pl.BlockSpec((nbs, SSM_STATE), lambda i, j: (i, 0))
    consts = [nw, win, wbr, wbi, wcr, wci, dsk, wglu, abr, abi]
    in_specs = [xs, mods(0), mods(1), mods(2)] + [full(a) for a in consts]
    args = [x, mod, mod, mod] + consts
    if has_state:
        in_specs += [st, st]
        args += [s0r, s0i]
    return pl.pallas_call(
        functools.partial(_s5_body, tt=tt, nbs=nbs, has_state=has_state),
        grid=(b // nbs, l // tt),
        in_specs=in_specs,
        out_specs=[xs, st, st],
        out_shape=[jax.ShapeDtypeStruct((l, b, D_MODEL), F32),
                   jax.ShapeDtypeStruct((b, SSM_STATE), F32),
                   jax.ShapeDtypeStruct((b, SSM_STATE), F32)],
        scratch_shapes=[pltpu.VMEM((tt, nbs, SSM_STATE), F32), pltpu.VMEM((tt, nbs, SSM_STATE), F32)],
        compiler_params=_params("parallel", "arbitrary"),
        name="s5",
    )(*args)


def _s5_discretize(lam_re, lam_im, log_dt, b_re, b_im, c_re, c_im):
    dt = jnp.exp(log_dt)[:, None]
    mag = jnp.exp(lam_re * dt)
    ang = lam_im * dt
    ab_re, ab_im = mag * jnp.cos(ang), mag * jnp.sin(ang)
    den = lam_re * lam_re + lam_im * lam_im
    f_re = ((ab_re - 1.0) * lam_re + ab_im * lam_im) / den
    f_im = (ab_im * lam_re - (ab_re - 1.0) * lam_im) / den
    bb_re = f_re[..., None] * b_re - f_im[..., None] * b_im
    bb_im = f_re[..., None] * b_im + f_im[..., None] * b_re
    blocks = D_MODEL // MXU_DIM
    gpb = SSM_GROUPS // blocks
    eye = jnp.eye(gpb, dtype=F32)

    def b_blocks(bb):
        t = bb.reshape(blocks, gpb, SSM_P, SSM_GROUP)
        w = jnp.einsum('jgpc,gh->jgchp', t, eye)
        return w.reshape(blocks, gpb * SSM_GROUP, gpb * SSM_P).astype(BF16)

    def c_blocks(cc):
        t = cc.reshape(blocks, gpb, SSM_GROUP, SSM_P)
        w = jnp.einsum('jgcp,gh->jgphc', t, eye)
        return w.reshape(blocks, gpb * SSM_P, gpb * SSM_GROUP).astype(BF16)

    return (ab_re.reshape(1, SSM_STATE), ab_im.reshape(1, SSM_STATE),
            b_blocks(bb_re), b_blocks(bb_im), c_blocks(c_re), c_blocks(c_im))


def _trunk(x, mods_mix, mods_ffn, conv0, delta0, re0, im0, wts, *, nb_tok, tt_tok, nb_delta, tt_delta, chunk,
           tt_s5, nbs_s5):
    b, l, _ = x.shape
    mm3 = [m.reshape(b, 1, 3 * D_MODEL) for m in mods_mix]
    mf3 = [m.reshape(b, 1, 3 * D_MODEL) for m in mods_ffn]

    qkv, z, gates = _gdn_front(x, mm3[0], mm3[0], wts['norm_mix'][0], wts['gdn_wq'], wts['gdn_wba'],
                               wts['gdn_alog'], wts['gdn_dtb'], nb=nb_tok, tt=tt_tok, chunk=chunk)
    o, s_delta, s_conv = _gdn_delta(qkv, z, gates, wts['gdn_cw'], wts['gdn_onorm'], delta0, conv0,
                                    nb=nb_delta, tt=tt_delta, chunk=chunk)
    prompt_like = nb_tok == 1
    moe_w = lambda i: (wts['norm_ffn'][i], wts['moe_wr'][i], wts['moe_br'][i],
                       wts['moe_w1'][i], wts['moe_w3'][i], wts['moe_w2'][i])
    x1 = _moe(x, o, wts['gdn_wout'], mm3[0], mf3[0], *moe_w(0), None, dims=(b, l), nb=nb_tok, tt=tt_tok,
              in_time_major=False, out_time_major=prompt_like)
    xt = x1.reshape(l, b, D_MODEL) if prompt_like else jnp.swapaxes(x1, 0, 1)
    x2, s_re, s_im = _s5(xt, mods_mix[1], wts['norm_mix'][1], wts['s5_win'], wts['s5_wbr'], wts['s5_wbi'],
                         wts['s5_wcr'], wts['s5_wci'], wts['s5_d'], wts['s5_wglu'], wts['s5_abr'], wts['s5_abi'],
                         re0, im0, tt=tt_s5, nbs=nbs_s5)
    x2 = x2.reshape(l, b * D_MODEL) if prompt_like else jnp.swapaxes(x2, 0, 1)
    y = _moe(x2, None, None, None, mf3[1], *moe_w(1), wts['norm_final'], dims=(b, l), nb=nb_tok, tt=tt_tok,
             in_time_major=prompt_like, out_time_major=False)
    return (y, s_delta[None], s_conv[None],
            s_re.reshape(1, b, SSM_GROUPS, SSM_P), s_im.reshape(1, b, SSM_GROUPS, SSM_P))


def kernel(x_prompt, x_sample, state_delta, state_conv, state_ssm_re, state_ssm_im, c_prompt, c_sample, norm_mix, w_mod_mix, b_mod_mix, norm_ffn, w_mod_ffn, b_mod_ffn, norm_final, gdn_w_in, gdn_conv_w, gdn_a_log, gdn_dt_bias, gdn_o_norm, gdn_w_out, s5_w_in, s5_lam_re, s5_lam_im, s5_log_dt, s5_b_re, s5_b_im, s5_c_re, s5_c_im, s5_d, s5_w_glu, moe_w_rg, moe_b_rg, moe_w_re, moe_b_re, moe_w1, moe_w3, moe_w2):
    bp = x_prompt.shape[0]
    depth = norm_mix.shape[0]

    lane_pad = lambda a, lo: jnp.pad(a, [(0, 0)] * (a.ndim - 1) + [(lo, LANES - lo - a.shape[-1])])
    abr, abi, wbr, wbi, wcr, wci = _s5_discretize(s5_lam_re[0], s5_lam_im[0], s5_log_dt[0], s5_b_re[0],
                                                  s5_b_im[0], s5_c_re[0], s5_c_im[0])
    wts = dict(
        norm_mix=norm_mix.reshape(depth, 1, D_MODEL), norm_ffn=norm_ffn.reshape(depth, 1, D_MODEL),
        norm_final=norm_final.reshape(1, D_MODEL),
        gdn_wq=gdn_w_in[0, :, :CONV_DIM + QK_DIM].astype(BF16),
        gdn_wba=lane_pad(gdn_w_in[0, :, CONV_DIM + QK_DIM:], 0).astype(BF16),
        gdn_alog=lane_pad(gdn_a_log[0][None], N_HEADS), gdn_dtb=lane_pad(gdn_dt_bias[0][None], N_HEADS),
        gdn_cw=jnp.pad(gdn_conv_w[0], ((0, SUBLANES - CONV_TAPS), (0, 0))),
        gdn_onorm=gdn_o_norm[0][None], gdn_wout=gdn_w_out[0].astype(BF16),
        s5_win=s5_w_in[0].astype(BF16), s5_wbr=wbr, s5_wbi=wbi, s5_wcr=wcr, s5_wci=wci,
        s5_d=s5_d[0].reshape(1, D_MODEL), s5_wglu=s5_w_glu[0].astype(BF16), s5_abr=abr, s5_abi=abi,
        moe_wr=lane_pad(jnp.concatenate([moe_w_rg, moe_w_re], axis=-1), 0),
        moe_br=lane_pad(jnp.concatenate([moe_b_rg, moe_b_re], axis=-1), 0)[:, None, :],
        moe_w1=moe_w1.astype(BF16), moe_w3=moe_w3.astype(BF16), moe_w2=moe_w2.astype(BF16),
    )

    c_all = jnp.concatenate([c_prompt, c_sample], axis=0)
    m_mix = _ada_mod(c_all, w_mod_mix, b_mod_mix)
    m_ffn = _ada_mod(c_all, w_mod_ffn, b_mod_ffn)

    y_p, p_delta, p_conv, p_re, p_im = _trunk(
        x_prompt, [m_mix[i, :bp] for i in range(depth)], [m_ffn[i, :bp] for i in range(depth)],
        None, None, None, None, wts,
        nb_tok=1, tt_tok=ROWS, nb_delta=1, tt_delta=ROWS, chunk=CHUNK, tt_s5=ROWS // SUBLANES, nbs_s5=SUBLANES)
    ls = x_sample.shape[1]
    y_s, s_delta, s_conv, s_re, s_im = _trunk(
        x_sample, [m_mix[i, bp:] for i in range(depth)], [m_ffn[i, bp:] for i in range(depth)],
        state_conv[0], state_delta[0], state_ssm_re[0].reshape(-1, SSM_STATE),
        state_ssm_im[0].reshape(-1, SSM_STATE), wts,
        nb_tok=ROWS // ls, tt_tok=ls, nb_delta=MXU_DIM // (ls * N_HEADS), tt_delta=ls, chunk=ls,
        tt_s5=ls, nbs_s5=ROWS // ls)
    return (y_p, y_s, p_delta, p_conv, p_re, p_im, s_delta, s_conv, s_re, s_im)
```

```python
import functools

import jax
import jax.numpy as jnp
from jax import lax
from jax.experimental import pallas as pl
from jax.experimental.pallas import tpu as pltpu

F32 = jnp.float32
BF16 = jnp.bfloat16

D_MODEL = 1024
N_HEADS = 8
HEAD_DIM = 128
QK_DIM = N_HEADS * HEAD_DIM
CONV_DIM = 3 * QK_DIM
CONV_TAPS = 4
CHUNK = 64
SSM_GROUPS = 64
SSM_GROUP = 16
SSM_P = 64
SSM_STATE = SSM_GROUPS * SSM_P
N_GROUPS = 4
EXPERTS_PER_GROUP = 4
N_EXPERTS = 16
D_FF = 256
EPS = 1e-6

LANES = 128
SUBLANES = 8
MXU_DIM = 256
ROWS = 256
MOE_ROWS = 512
SCAN_LANES = 512
VMEM_LIMIT = 56 * 1024 * 1024


def _mm(a, b):
    return jnp.dot(a.astype(BF16), b.astype(BF16), preferred_element_type=F32)


def _mm_nt(a, b):
    return lax.dot_general(a.astype(BF16), b.astype(BF16), (((1,), (1,)), ((), ())),
                           preferred_element_type=F32)


def _split_bf16(x, terms):
    out = []
    for _ in range(terms - 1):
        p = x.astype(BF16)
        out.append(p)
        x = x - p.astype(F32)
    out.append(x.astype(BF16))
    return out


def _rms(x, w):
    return x * lax.rsqrt(jnp.mean(x * x, axis=-1, keepdims=True) + EPS) * w


def _silu(x):
    return x * jax.nn.sigmoid(x)


def _softplus(x):
    return jnp.maximum(x, 0.0) + jnp.log1p(jnp.exp(-jnp.abs(x)))


def _lane_col(tile, lane):
    li = lax.broadcasted_iota(jnp.int32, tile.shape, 1)
    return jnp.sum(jnp.where(li == lane, tile, 0.0), axis=1, keepdims=True)


def _log2(n):
    assert n & (n - 1) == 0
    return n.bit_length() - 1


def _params(*sem):
    return pltpu.CompilerParams(dimension_semantics=sem, vmem_limit_bytes=VMEM_LIMIT)


def _resident(a):
    return pl.BlockSpec(a.shape, lambda *_: (0,) * a.ndim, pipeline_mode=pl.Buffered(1))


def _mod_body(c_ref, w_ref, b_ref, o_ref):
    o_ref[0] = _mm(_silu(c_ref[...]), w_ref[0]) + b_ref[0]


def _ada_mod(c_all, w_mod, b_mod):
    nl, nr, tn = w_mod.shape[0], c_all.shape[0], 768
    return pl.pallas_call(
        _mod_body,
        grid=(nl, 3 * D_MODEL // tn),
        in_specs=[pl.BlockSpec((nr, D_MODEL), lambda l, j: (0, 0)),
                  pl.BlockSpec((1, D_MODEL, tn), lambda l, j: (l, 0, j)),
                  pl.BlockSpec((1, 1, tn), lambda l, j: (l, 0, j))],
        out_specs=pl.BlockSpec((1, nr, tn), lambda l, j: (l, 0, j)),
        out_shape=jax.ShapeDtypeStruct((nl, nr, 3 * D_MODEL), F32),
        compiler_params=_params("parallel", "parallel"),
        name="ada_mod",
    )(c_all, w_mod, b_mod.reshape(nl, 1, 3 * D_MODEL))


def _gdn_front_body(x_ref, sh_ref, sc_ref, nw_ref, wq_ref, wba_ref, alog_ref, dtb_ref,
                    qkv_ref, z_ref, gt_ref, *, chunk):
    x = x_ref[...]
    nb, tt, _ = x.shape
    r = nb * tt
    h = _rms(x, nw_ref[...]) * (1.0 + sc_ref[...]) + sh_ref[...]
    hb = h.reshape(r, D_MODEL).astype(BF16)
    pq = jnp.dot(hb, wq_ref[...], preferred_element_type=F32)
    qkv_ref[...] = pq[:, :CONV_DIM].reshape(nb, tt, CONV_DIM)
    z_ref[...] = pq[:, CONV_DIM:].reshape(nb, tt, QK_DIM)
    ba = jnp.dot(hb, wba_ref[...], preferred_element_type=F32)
    beta = jax.nn.sigmoid(ba)
    g = -jnp.exp(alog_ref[...]) * _softplus(ba + dtb_ref[...])
    ri = lax.broadcasted_iota(jnp.int32, (r, r), 0)
    ci = lax.broadcasted_iota(jnp.int32, (r, r), 1)
    same = (ri >> _log2(chunk)) == (ci >> _log2(chunk))
    low = jnp.where(same & (ri >= ci), 1.0, 0.0).astype(BF16)
    ones = jnp.where(same, 1.0, 0.0).astype(BF16)
    gc = jnp.zeros((r, LANES), F32)
    gl = jnp.zeros((r, LANES), F32)
    for piece in _split_bf16(g, 3):
        gc = gc + jnp.dot(low, piece, preferred_element_type=F32)
        gl = gl + jnp.dot(ones, piece, preferred_element_type=F32)
    gt_ref[...] = jnp.concatenate([beta, gc, gl], axis=-1).reshape(nb, tt, 3 * LANES)


def _gdn_front(x, shift, scale, nw, wq, wba, alog, dtb, *, nb, tt, chunk):
    b, l, _ = x.shape
    tok = lambda w: pl.BlockSpec((nb, tt, w), lambda i, j: (i, j, 0))
    mod = lambda col: pl.BlockSpec((nb, 1, D_MODEL), lambda i, j: (i, 0, col))
    full = _resident
    return pl.pallas_call(
        functools.partial(_gdn_front_body, chunk=chunk),
        grid=(b // nb, l // tt),
        in_specs=[tok(D_MODEL), mod(0), mod(1), full(nw), full(wq), full(wba), full(alog), full(dtb)],
        out_specs=[tok(CONV_DIM), tok(QK_DIM), tok(3 * LANES)],
        out_shape=[jax.ShapeDtypeStruct((b, l, CONV_DIM), F32),
                   jax.ShapeDtypeStruct((b, l, QK_DIM), F32),
                   jax.ShapeDtypeStruct((b, l, 3 * LANES), F32)],
        compiler_params=_params("parallel", "parallel"),
        name="gdn_front",
    )(x, shift, scale, nw, wq, wba, alog, dtb)


def _inv_unit_lower(a, ri, ci, chunk):
    base = min(16, chunk)
    same = lambda size: (ri >> _log2(size)) == (ci >> _log2(size))
    n = jnp.where(same(base), -a, 0.0)
    q, m, k = n, n, 1
    while 2 * k < base:
        m = _mm(m, m)
        q = q + m + _mm(q, m)
        k *= 2
    size = base
    while size < chunk:
        e = jnp.where(same(2 * size) & jnp.logical_not(same(size)), a, 0.0)
        x = e + _mm(q, e)
        q = q - (x + _mm(x, q))
        size *= 2
    return q


def _gdn_delta_body(*refs, nb, tt, chunk, has_state):
    if has_state:
        (qkv_ref, z_ref, gt_ref, cw_ref, on_ref, s0_ref, c0_ref,
         o_ref, s_ref, cn_ref, xc, qkv_s, o_s, u_s, wq_s, qk_s, kdt_s) = refs
    else:
        (qkv_ref, z_ref, gt_ref, cw_ref, on_ref,
         o_ref, s_ref, cn_ref, xc, qkv_s, o_s, u_s, wq_s, qk_s, kdt_s) = refs
    hist = CONV_TAPS - 1
    pad = SUBLANES

    @pl.when(pl.program_id(1) == 0)
    def _():
        if has_state:
            s_ref[...] = s0_ref[...]
            xc[:, pad - hist:pad, :] = c0_ref[...]
        else:
            s_ref[...] = jnp.zeros(s_ref.shape, F32)
            xc[:, pad - hist:pad, :] = jnp.zeros((nb, hist, CONV_DIM), F32)

    xc[:, pad:pad + tt, :] = qkv_ref[...]
    cw = cw_ref[...]
    y = xc[:, pad - hist:pad - hist + tt, :] * cw[0:1]
    for j in range(1, CONV_TAPS):
        y = y + xc[:, pad - hist + j:pad - hist + j + tt, :] * cw[j:j + 1]
    qkv_s[...] = _silu(y)
    last = xc[:, pad + tt - hist:pad + tt, :]
    cn_ref[...] = last
    xc[:, pad - hist:pad, :] = last

    for hd in range(N_HEADS):
        for part, scl in ((0, HEAD_DIM ** -0.5), (1, 1.0)):
            lo = part * QK_DIM + hd * HEAD_DIM
            xh = qkv_s[:, :, lo:lo + HEAD_DIM]
            xh = xh * (lax.rsqrt(jnp.sum(xh * xh, axis=-1, keepdims=True) + EPS) * scl)
            qkv_s[:, :, lo:lo + HEAD_DIM] = xh

    per_stack = MXU_DIM // chunk
    units_all = [(s, hd) for s in range(nb) for hd in range(N_HEADS)]
    assert len(units_all) % per_stack == 0
    stacks = [units_all[i:i + per_stack] for i in range(0, len(units_all), per_stack)]
    r = MXU_DIM
    ri = lax.broadcasted_iota(jnp.int32, (r, r), 0)
    ci = lax.broadcasted_iota(jnp.int32, (r, r), 1)
    same = (ri >> _log2(chunk)) == (ci >> _log2(chunk))
    tril = same & (ri >= ci)
    strict = same & (ri > ci)
    eye = ri == ci

    n_chunks = tt // chunk
    dot = lambda p, q: jnp.dot(p, q, preferred_element_type=F32)

    for c in range(n_chunks):
        rows = slice(c * chunk, (c + 1) * chunk)
        for k, units in enumerate(stacks):
            idx = c * len(stacks) + k
            cat = lambda f: jnp.concatenate([f(s, hd) for s, hd in units], axis=0)
            qn = cat(lambda s, hd: qkv_s[s, rows, hd * HEAD_DIM:(hd + 1) * HEAD_DIM])
            kn = cat(lambda s, hd: qkv_s[s, rows, QK_DIM + hd * HEAD_DIM:QK_DIM + (hd + 1) * HEAD_DIM])
            v = cat(lambda s, hd: qkv_s[s, rows, 2 * QK_DIM + hd * HEAD_DIM:2 * QK_DIM + (hd + 1) * HEAD_DIM])
            beta = cat(lambda s, hd: _lane_col(gt_ref[s, rows, 0:LANES], hd))
            gc = cat(lambda s, hd: _lane_col(gt_ref[s, rows, LANES:2 * LANES], N_HEADS + hd))
            gl = cat(lambda s, hd: _lane_col(gt_ref[s, rows, 2 * LANES:3 * LANES], N_HEADS + hd))
            eg = jnp.exp(gc)
            kb = kn * beta
            gc_row = jnp.sum(jnp.where(eye, gc, 0.0), axis=0, keepdims=True)
            decay = jnp.where(tril, jnp.exp(jnp.where(tril, gc - gc_row, 0.0)), 0.0)
            a = jnp.where(strict, _mm_nt(kb, kn) * decay, 0.0)
            q_inv = _inv_unit_lower(a, ri, ci, chunk)
            rhs = jnp.concatenate([v * beta, kb * eg], axis=1)
            uw = rhs + _mm(q_inv, rhs)
            w = uw[:, HEAD_DIM:]
            qd = qn * eg
            u_s[idx] = uw[:, :HEAD_DIM]
            wq_s[idx] = jnp.concatenate(
                [t[i * chunk:(i + 1) * chunk] for i in range(per_stack) for t in (w, qd)], axis=0).astype(BF16)
            qk_s[idx] = jnp.where(tril, _mm_nt(qn, kn) * decay, 0.0).astype(BF16)
            kdt_s[idx] = (kn * jnp.exp(gl - gc)).T.astype(BF16)

    unit_of_row = lax.broadcasted_iota(jnp.int32, (r, HEAD_DIM), 0) >> _log2(chunk)
    for c in range(n_chunks):
        rows = slice(c * chunk, (c + 1) * chunk)
        for k, units in enumerate(stacks):
            idx = c * len(stacks) + k
            ws, qs = [], []
            for i, (s, hd) in enumerate(units):
                both = dot(wq_s[idx, 2 * i * chunk:2 * (i + 1) * chunk, :], s_ref[s, hd].astype(BF16))
                ws.append(both[:chunk])
                qs.append(both[chunk:])
            v_new = (u_s[idx] - jnp.concatenate(ws, axis=0)).astype(BF16)
            o = jnp.concatenate(qs, axis=0) + dot(qk_s[idx], v_new)
            kdt = kdt_s[idx]
            for i, (s, hd) in enumerate(units):
                o_s[s, rows, hd * HEAD_DIM:(hd + 1) * HEAD_DIM] = o[i * chunk:(i + 1) * chunk]
                g_last = jnp.exp(_lane_col(gt_ref[s, c * chunk:c * chunk + 1, 2 * LANES:3 * LANES], N_HEADS + hd))
                upd = dot(kdt, jnp.where(unit_of_row == i, v_new, jnp.zeros_like(v_new)))
                s_ref[s, hd] = s_ref[s, hd] * g_last + upd

    on = on_ref[...]
    for hd in range(N_HEADS):
        sl = slice(hd * HEAD_DIM, (hd + 1) * HEAD_DIM)
        o_ref[:, :, sl] = _rms(o_s[:, :, sl], on) * _silu(z_ref[:, :, sl])


def _gdn_delta(qkv, z, gates, cw, onorm, s0, c0, *, nb, tt, chunk):
    b, l, _ = qkv.shape
    has_state = s0 is not None
    n_stacks = nb * tt * N_HEADS // MXU_DIM
    tok = lambda w: pl.BlockSpec((nb, tt, w), lambda i, j: (i, j, 0))
    full = _resident
    st_spec = pl.BlockSpec((nb, N_HEADS, HEAD_DIM, HEAD_DIM), lambda i, j: (i, 0, 0, 0))
    cv_spec = pl.BlockSpec((nb, CONV_TAPS - 1, CONV_DIM), lambda i, j: (i, 0, 0))
    in_specs = [tok(CONV_DIM), tok(QK_DIM), tok(3 * LANES), full(cw), full(onorm)]
    args = [qkv, z, gates, cw, onorm]
    if has_state:
        in_specs += [st_spec, cv_spec]
        args += [s0, c0]
    return pl.pallas_call(
        functools.partial(_gdn_delta_body, nb=nb, tt=tt, chunk=chunk, has_state=has_state),
        grid=(b // nb, l // tt),
        in_specs=in_specs,
        out_specs=[tok(QK_DIM), st_spec, cv_spec],
        out_shape=[jax.ShapeDtypeStruct((b, l, QK_DIM), F32),
                   jax.ShapeDtypeStruct((b, N_HEADS, HEAD_DIM, HEAD_DIM), F32),
                   jax.ShapeDtypeStruct((b, CONV_TAPS - 1, CONV_DIM), F32)],
        scratch_shapes=[pltpu.VMEM((nb, tt + SUBLANES, CONV_DIM), F32),
                        pltpu.VMEM((nb, tt, CONV_DIM), F32),
                        pltpu.VMEM((nb, tt, QK_DIM), F32),
                        pltpu.VMEM((n_stacks, MXU_DIM, HEAD_DIM), F32),
                        pltpu.VMEM((n_stacks, 2 * MXU_DIM, HEAD_DIM), BF16),
                        pltpu.VMEM((n_stacks, MXU_DIM, MXU_DIM), BF16),
                        pltpu.VMEM((n_stacks, HEAD_DIM, MXU_DIM), BF16)],
        compiler_params=_params("parallel", "arbitrary"),
        name="gdn_delta",
    )(*args)


def _route(logits):
    lane = lax.broadcasted_iota(jnp.int32, logits.shape, 1)
    neg, big = -1e30, 1 << 20
    rmax = lambda t: jnp.max(t, axis=1, keepdims=True)
    rsum = lambda t: jnp.sum(t, axis=1, keepdims=True)
    first = lambda cond: jnp.min(jnp.where(cond, lane, big), axis=1, keepdims=True)
    is_g = lane < N_GROUPS
    lg = jnp.where(is_g, logits, neg)
    mg = rmax(lg)
    g_idx = first(is_g & (lg >= mg))
    p_group = 1.0 / rsum(jnp.where(is_g, jnp.exp(lg - mg), 0.0))
    lo = N_GROUPS + EXPERTS_PER_GROUP * g_idx
    in_g = (lane >= lo) & (lane < lo + EXPERTS_PER_GROUP)
    le = jnp.where(in_g, logits, neg)
    m1 = rmax(le)
    se = rsum(jnp.where(in_g, jnp.exp(le - m1), 0.0))
    i1 = first(in_g & (le >= m1))
    le2 = jnp.where(lane == i1, neg, le)
    m2 = rmax(le2)
    i2 = first(in_g & (lane != i1) & (le2 >= m2))
    p1 = 1.0 / se
    p2 = jnp.exp(m2 - m1) / se
    tot = p1 + p2
    return (jnp.where(lane == i1, p1 / tot * p_group, 0.0)
            + jnp.where(lane == i2, p2 / tot * p_group, 0.0))


def _moe_body(*refs, nb, tt, has_proj, final_norm):
    it = iter(refs)
    x_ref = next(it)
    if has_proj:
        y_ref, wo_ref, gm_ref = next(it), next(it), next(it)
    sh_ref, sc_ref, gf_ref, nw_ref, wr_ref, br_ref, w1_ref, w3_ref, w2_ref = (next(it) for _ in range(9))
    if final_norm:
        nf_ref = next(it)
    out_ref = next(it)
    r = nb * tt
    dot = lambda p, q: jnp.dot(p, q, preferred_element_type=F32)

    x = x_ref[...].reshape(nb, tt, D_MODEL)
    if has_proj:
        x = x + gm_ref[...] * _mm(y_ref[...].reshape(r, D_MODEL), wo_ref[...]).reshape(nb, tt, D_MODEL)
    h = (_rms(x, nw_ref[...]) * (1.0 + sc_ref[...]) + sh_ref[...]).reshape(r, D_MODEL)
    h_hi, h_lo = _split_bf16(h, 2)
    w_hi, w_lo = _split_bf16(wr_ref[...], 2)
    logits = dot(h_hi, w_hi) + (dot(h_hi, w_lo) + dot(h_lo, w_hi)) + br_ref[...]
    comb = _route(logits)

    gw = EXPERTS_PER_GROUP * D_FF
    acc = jnp.zeros((r, D_MODEL), F32)
    for g in range(N_GROUPS):
        cols = slice(g * gw, (g + 1) * gw)
        hid = _silu(dot(h_hi, w1_ref[:, cols])) * dot(h_hi, w3_ref[:, cols])
        parts = []
        for j in range(EXPERTS_PER_GROUP):
            ce = _lane_col(comb, N_GROUPS + g * EXPERTS_PER_GROUP + j)
            parts.append((hid[:, j * D_FF:(j + 1) * D_FF] * ce).astype(BF16))
        acc = acc + dot(jnp.concatenate(parts, axis=1), w2_ref[cols, :])

    out = x + gf_ref[...] * acc.reshape(nb, tt, D_MODEL)
    if final_norm:
        out = _rms(out, nf_ref[...])
    out_ref[...] = out.reshape(out_ref.shape)


def _moe(x, y, wo, mod_mix, mod_ffn, nw, wr, br, w1, w3, w2, nf, *, dims, nb, tt,
         in_time_major, out_time_major):
    b, l = dims
    has_proj, final_norm = y is not None, nf is not None
    grid = (b // nb, l // tt)
    bm = pl.BlockSpec((nb, tt, D_MODEL), lambda i, j: (i, j, 0))
    tm = pl.BlockSpec((tt, D_MODEL), lambda i, j: (j, i))
    mod = lambda col: pl.BlockSpec((nb, 1, D_MODEL), lambda i, j: (i, 0, col))
    full = _resident
    x_spec = tm if in_time_major else bm
    in_specs, args = [x_spec], [x]
    if has_proj:
        in_specs += [x_spec, full(wo), mod(2)]
        args += [y, wo, mod_mix]
    in_specs += [mod(0), mod(1), mod(2)] + [full(a) for a in (nw, wr, br, w1, w3, w2)]
    args += [mod_ffn, mod_ffn, mod_ffn, nw, wr, br, w1, w3, w2]
    if final_norm:
        in_specs.append(full(nf))
        args.append(nf)
    if out_time_major:
        assert nb == 1
        out_spec, out_shape = tm, jax.ShapeDtypeStruct((l, b * D_MODEL), F32)
    else:
        out_spec, out_shape = bm, jax.ShapeDtypeStruct((b, l, D_MODEL), F32)
    return pl.pallas_call(
        functools.partial(_moe_body, nb=nb, tt=tt, has_proj=has_proj, final_norm=final_norm),
        grid=grid,
        in_specs=in_specs,
        out_specs=out_spec,
        out_shape=out_shape,
        compiler_params=_params("parallel", "parallel"),
        name="moe",
    )(*args)


def _s5_body(*refs, tt, nbs, has_state):
    if has_state:
        (x_ref, sh_ref, sc_ref, gm_ref, nw_ref, win_ref, wbr_ref, wbi_ref, wcr_ref, wci_ref, dsk_ref,
         wglu_ref, abr_ref, abi_ref, s0r_ref, s0i_ref, out_ref, sr_ref, si_ref, xr_s, xi_s) = refs
    else:
        (x_ref, sh_ref, sc_ref, gm_ref, nw_ref, win_ref, wbr_ref, wbi_ref, wcr_ref, wci_ref, dsk_ref,
         wglu_ref, abr_ref, abi_ref, out_ref, sr_ref, si_ref, xr_s, xi_s) = refs
    r = tt * nbs
    blocks = D_MODEL // MXU_DIM
    sw = SSM_STATE // blocks

    @pl.when(pl.program_id(1) == 0)
    def _():
        if has_state:
            sr_ref[...] = s0r_ref[...]
            si_ref[...] = s0i_ref[...]
        else:
            sr_ref[...] = jnp.zeros(sr_ref.shape, F32)
            si_ref[...] = jnp.zeros(si_ref.shape, F32)

    x = x_ref[...]
    h = (_rms(x, nw_ref[...]) * (1.0 + sc_ref[...]) + sh_ref[...]).reshape(r, D_MODEL)
    u = _mm(h, win_ref[...])
    ub = u.astype(BF16)
    for j in range(blocks):
        uj = ub[:, j * MXU_DIM:(j + 1) * MXU_DIM]
        xr_s[:, :, j * sw:(j + 1) * sw] = jnp.dot(
            uj, wbr_ref[j], preferred_element_type=F32).reshape(tt, nbs, sw)
        xi_s[:, :, j * sw:(j + 1) * sw] = jnp.dot(
            uj, wbi_ref[j], preferred_element_type=F32).reshape(tt, nbs, sw)

    for rg in range(nbs // SUBLANES):
        rows = slice(rg * SUBLANES, (rg + 1) * SUBLANES)
        for ch in range(SSM_STATE // SCAN_LANES):
            lanes = slice(ch * SCAN_LANES, (ch + 1) * SCAN_LANES)
            ar = jnp.broadcast_to(abr_ref[:, lanes], (SUBLANES, SCAN_LANES))
            ai = jnp.broadcast_to(abi_ref[:, lanes], (SUBLANES, SCAN_LANES))

            def step(t, hc):
                hr, hi = hc
                nr = ar * hr - ai * hi + xr_s[t, rows, lanes]
                ni = ar * hi + ai * hr + xi_s[t, rows, lanes]
                xr_s[t, rows, lanes] = nr
                xi_s[t, rows, lanes] = ni
                return nr, ni

            hr, hi = lax.fori_loop(0, tt, step, (sr_ref[rows, lanes], si_ref[rows, lanes]), unroll=SUBLANES)
            sr_ref[rows, lanes] = hr
            si_ref[rows, lanes] = hi

    ys = []
    for j in range(blocks):
        st_r = xr_s[:, :, j * sw:(j + 1) * sw].reshape(r, sw)
        st_i = xi_s[:, :, j * sw:(j + 1) * sw].reshape(r, sw)
        ys.append(_mm(st_r, wcr_ref[j]) - _mm(st_i, wci_ref[j]))
    y = jnp.concatenate(ys, axis=1) + dsk_ref[...] * u
    ag = _mm(jax.nn.gelu(y), wglu_ref[...])
    mix = ag[:, :D_MODEL] * jax.nn.sigmoid(ag[:, D_MODEL:])
    out_ref[...] = x + gm_ref[...] * mix.reshape(tt, nbs, D_MODEL)


def _s5(x, mod, nw, win, wbr, wbi, wcr, wci, dsk, wglu, abr, abi, s0r, s0i, *, tt, nbs):
    l, b, _ = x.shape
    has_state = s0r is not None
    xs = pl.BlockSpec((tt, nbs, D_MODEL), lambda i, j: (j, i, 0))
    mods = lambda col: pl.BlockSpec((nbs, D_MODEL), lambda i, j: (i, col))
    full = _resident
    st = pl.BlockSpec((nbs, SSM_STATE), lambda i, j: (i, 0))
    consts = [nw, win, wbr, wbi, wcr, wci, dsk, wglu, abr, abi]
    in_specs = [xs, mods(0), mods(1), mods(2)] + [full(a) for a in consts]
    args = [x, mod, mod, mod] + consts
    if has_state:
        in_specs += [st, st]
        args += [s0r, s0i]
    return pl.pallas_call(
        functools.partial(_s5_body, tt=tt, nbs=nbs, has_state=has_state),
        grid=(b // nbs, l // tt),
        in_specs=in_specs,
        out_specs=[xs, st, st],
        out_shape=[jax.ShapeDtypeStruct((l, b, D_MODEL), F32),
                   jax.ShapeDtypeStruct((b, SSM_STATE), F32),
                   jax.ShapeDtypeStruct((b, SSM_STATE), F32)],
        scratch_shapes=[pltpu.VMEM((tt, nbs, SSM_STATE), F32), pltpu.VMEM((tt, nbs, SSM_STATE), F32)],
        compiler_params=_params("parallel", "arbitrary"),
        name="s5",
    )(*args)


def _s5_discretize(lam_re, lam_im, log_dt, b_re, b_im, c_re, c_im):
    dt = jnp.exp(log_dt)[:, None]
    mag = jnp.exp(lam_re * dt)
    ang = lam_im * dt
    ab_re, ab_im = mag * jnp.cos(ang), mag * jnp.sin(ang)
    den = lam_re * lam_re + lam_im * lam_im
    f_re = ((ab_re - 1.0) * lam_re + ab_im * lam_im) / den
    f_im = (ab_im * lam_re - (ab_re - 1.0) * lam_im) / den
    bb_re = f_re[..., None] * b_re - f_im[..., None] * b_im
    bb_im = f_re[..., None] * b_im + f_im[..., None] * b_re
    blocks = D_MODEL // MXU_DIM
    gpb = SSM_GROUPS // blocks
    eye = jnp.eye(gpb, dtype=F32)

    def b_blocks(bb):
        t = bb.reshape(blocks, gpb, SSM_P, SSM_GROUP)
        w = jnp.einsum('jgpc,gh->jgchp', t, eye)
        return w.reshape(blocks, gpb * SSM_GROUP, gpb * SSM_P).astype(BF16)

    def c_blocks(cc):
        t = cc.reshape(blocks, gpb, SSM_GROUP, SSM_P)
        w = jnp.einsum('jgcp,gh->jgphc', t, eye)
        return w.reshape(blocks, gpb * SSM_P, gpb * SSM_GROUP).astype(BF16)

    return (ab_re.reshape(1, SSM_STATE), ab_im.reshape(1, SSM_STATE),
            b_blocks(bb_re), b_blocks(bb_im), c_blocks(c_re), c_blocks(c_im))


def _trunk(x, mods_mix, mods_ffn, conv0, delta0, re0, im0, wts, *, nb_tok, tt_tok, nb_moe, tt_moe, nb_delta, tt_delta, chunk,
           tt_s5, nbs_s5):
    b, l, _ = x.shape
    mm3 = [m.reshape(b, 1, 3 * D_MODEL) for m in mods_mix]
    mf3 = [m.reshape(b, 1, 3 * D_MODEL) for m in mods_ffn]

    qkv, z, gates = _gdn_front(x, mm3[0], mm3[0], wts['norm_mix'][0], wts['gdn_wq'], wts['gdn_wba'],
                               wts['gdn_alog'], wts['gdn_dtb'], nb=nb_tok, tt=tt_tok, chunk=chunk)
    o, s_delta, s_conv = _gdn_delta(qkv, z, gates, wts['gdn_cw'], wts['gdn_onorm'], delta0, conv0,
                                    nb=nb_delta, tt=tt_delta, chunk=chunk)
    prompt_like = nb_tok == 1
    moe_w = lambda i: (wts['norm_ffn'][i], wts['moe_wr'][i], wts['moe_br'][i],
                       wts['moe_w1'][i], wts['moe_w3'][i], wts['moe_w2'][i])
    x1 = _moe(x, o, wts['gdn_wout'], mm3[0], mf3[0], *moe_w(0), None, dims=(b, l), nb=nb_moe, tt=tt_moe,
              in_time_major=False, out_time_major=prompt_like)
    xt = x1.reshape(l, b, D_MODEL) if prompt_like else jnp.swapaxes(x1, 0, 1)
    x2, s_re, s_im = _s5(xt, mods_mix[1], wts['norm_mix'][1], wts['s5_win'], wts['s5_wbr'], wts['s5_wbi'],
                         wts['s5_wcr'], wts['s5_wci'], wts['s5_d'], wts['s5_wglu'], wts['s5_abr'], wts['s5_abi'],
                         re0, im0, tt=tt_s5, nbs=nbs_s5)
    x2 = x2.reshape(l, b * D_MODEL) if prompt_like else jnp.swapaxes(x2, 0, 1)
    y = _moe(x2, None, None, None, mf3[1], *moe_w(1), wts['norm_final'], dims=(b, l), nb=nb_moe, tt=tt_moe,
             in_time_major=prompt_like, out_time_major=False)
    return (y, s_delta[None], s_conv[None],
            s_re.reshape(1, b, SSM_GROUPS, SSM_P), s_im.reshape(1, b, SSM_GROUPS, SSM_P))


def kernel(x_prompt, x_sample, state_delta, state_conv, state_ssm_re, state_ssm_im, c_prompt, c_sample, norm_mix, w_mod_mix, b_mod_mix, norm_ffn, w_mod_ffn, b_mod_ffn, norm_final, gdn_w_in, gdn_conv_w, gdn_a_log, gdn_dt_bias, gdn_o_norm, gdn_w_out, s5_w_in, s5_lam_re, s5_lam_im, s5_log_dt, s5_b_re, s5_b_im, s5_c_re, s5_c_im, s5_d, s5_w_glu, moe_w_rg, moe_b_rg, moe_w_re, moe_b_re, moe_w1, moe_w3, moe_w2):
    bp = x_prompt.shape[0]
    depth = norm_mix.shape[0]

    lane_pad = lambda a, lo: jnp.pad(a, [(0, 0)] * (a.ndim - 1) + [(lo, LANES - lo - a.shape[-1])])
    abr, abi, wbr, wbi, wcr, wci = _s5_discretize(s5_lam_re[0], s5_lam_im[0], s5_log_dt[0], s5_b_re[0],
                                                  s5_b_im[0], s5_c_re[0], s5_c_im[0])
    wts = dict(
        norm_mix=norm_mix.reshape(depth, 1, D_MODEL), norm_ffn=norm_ffn.reshape(depth, 1, D_MODEL),
        norm_final=norm_final.reshape(1, D_MODEL),
        gdn_wq=gdn_w_in[0, :, :CONV_DIM + QK_DIM].astype(BF16),
        gdn_wba=lane_pad(gdn_w_in[0, :, CONV_DIM + QK_DIM:], 0).astype(BF16),
        gdn_alog=lane_pad(gdn_a_log[0][None], N_HEADS), gdn_dtb=lane_pad(gdn_dt_bias[0][None], N_HEADS),
        gdn_cw=jnp.pad(gdn_conv_w[0], ((0, SUBLANES - CONV_TAPS), (0, 0))),
        gdn_onorm=gdn_o_norm[0][None], gdn_wout=gdn_w_out[0].astype(BF16),
        s5_win=s5_w_in[0].astype(BF16), s5_wbr=wbr, s5_wbi=wbi, s5_wcr=wcr, s5_wci=wci,
        s5_d=s5_d[0].reshape(1, D_MODEL), s5_wglu=s5_w_glu[0].astype(BF16), s5_abr=abr, s5_abi=abi,
        moe_wr=lane_pad(jnp.concatenate([moe_w_rg, moe_w_re], axis=-1), 0),
        moe_br=lane_pad(jnp.concatenate([moe_b_rg, moe_b_re], axis=-1), 0)[:, None, :],
        moe_w1=jnp.swapaxes(moe_w1, 1, 2).reshape(depth, D_MODEL, N_EXPERTS * D_FF).astype(BF16),
        moe_w3=jnp.swapaxes(moe_w3, 1, 2).reshape(depth, D_MODEL, N_EXPERTS * D_FF).astype(BF16),
        moe_w2=moe_w2.reshape(depth, N_EXPERTS * D_FF, D_MODEL).astype(BF16),
    )

    c_all = jnp.concatenate([c_prompt, c_sample], axis=0)
    m_mix = _ada_mod(c_all, w_mod_mix, b_mod_mix)
    m_ffn = _ada_mod(c_all, w_mod_ffn, b_mod_ffn)

    y_p, p_delta, p_conv, p_re, p_im = _trunk(
        x_prompt, [m_mix[i, :bp] for i in range(depth)], [m_ffn[i, :bp] for i in range(depth)],
        None, None, None, None, wts,
        nb_tok=1, tt_tok=ROWS, nb_moe=1, tt_moe=MOE_ROWS, nb_delta=1, tt_delta=ROWS, chunk=CHUNK, tt_s5=ROWS // SUBLANES, nbs_s5=SUBLANES)
    ls = x_sample.shape[1]
    y_s, s_delta, s_conv, s_re, s_im = _trunk(
        x_sample, [m_mix[i, bp:] for i in range(depth)], [m_ffn[i, bp:] for i in range(depth)],
        state_conv[0], state_delta[0], state_ssm_re[0].reshape(-1, SSM_STATE),
        state_ssm_im[0].reshape(-1, SSM_STATE), wts,
        nb_tok=ROWS // ls, tt_tok=ls, nb_moe=MOE_ROWS // ls, tt_moe=ls, nb_delta=MXU_DIM // (ls * N_HEADS), tt_delta=ls, chunk=ls,
        tt_s5=ls, nbs_s5=ROWS // ls)
    return (y_p, y_s, p_delta, p_conv, p_re, p_im, s_delta, s_conv, s_re, s_im)
```

```python
import functools

import jax
import jax.numpy as jnp
from jax import lax
from jax.experimental import pallas as pl
from jax.experimental.pallas import tpu as pltpu

F32 = jnp.float32
BF16 = jnp.bfloat16

D_MODEL = 1024
N_HEADS = 8
HEAD_DIM = 128
QK_DIM = N_HEADS * HEAD_DIM
CONV_DIM = 3 * QK_DIM
CONV_TAPS = 4
CHUNK = 64
SSM_GROUPS = 64
SSM_GROUP = 16
SSM_P = 64
SSM_STATE = SSM_GROUPS * SSM_P
N_GROUPS = 4
EXPERTS_PER_GROUP = 4
N_EXPERTS = 16
D_FF = 256
EPS = 1e-6

LANES = 128
SUBLANES = 8
MXU_DIM = 256
ROWS = 256
MOE_ROWS = 512
SCAN_LANES = 512
VMEM_LIMIT = 56 * 1024 * 1024


def _mm(a, b):
    return jnp.dot(a.astype(BF16), b.astype(BF16), preferred_element_type=F32)


def _bmm(a, b):
    return lax.dot_general(a.astype(BF16), b.astype(BF16), (((2,), (1,)), ((0,), (0,))),
                           preferred_element_type=F32)


def _bmm_nt(a, b):
    return lax.dot_general(a.astype(BF16), b.astype(BF16), (((2,), (2,)), ((0,), (0,))),
                           preferred_element_type=F32)


def _split_bf16(x, terms):
    out = []
    for _ in range(terms - 1):
        p = x.astype(BF16)
        out.append(p)
        x = x - p.astype(F32)
    out.append(x.astype(BF16))
    return out


def _rms(x, w):
    return x * lax.rsqrt(jnp.mean(x * x, axis=-1, keepdims=True) + EPS) * w


def _sigmoid(x):
    return 0.5 * jnp.tanh(0.5 * x) + 0.5


def _silu(x):
    h = 0.5 * x
    return h * jnp.tanh(h) + h


def _softplus(x):
    return jnp.maximum(x, 0.0) + jnp.log1p(jnp.exp(-jnp.abs(x)))


def _lane_col(tile, lane):
    li = lax.broadcasted_iota(jnp.int32, tile.shape, 1)
    return jnp.sum(jnp.where(li == lane, tile, 0.0), axis=1, keepdims=True)


def _log2(n):
    assert n & (n - 1) == 0
    return n.bit_length() - 1


def _params(*sem):
    return pltpu.CompilerParams(dimension_semantics=sem, vmem_limit_bytes=VMEM_LIMIT)


def _resident(a):
    return pl.BlockSpec(a.shape, lambda *_: (0,) * a.ndim, pipeline_mode=pl.Buffered(1))


def _mod_body(c_ref, w_ref, b_ref, o_ref):
    o_ref[0] = _mm(_silu(c_ref[...]), w_ref[0]) + b_ref[0]


def _ada_mod(c_all, w_mod, b_mod):
    nl, nr, tn = w_mod.shape[0], c_all.shape[0], 768
    return pl.pallas_call(
        _mod_body,
        grid=(nl, 3 * D_MODEL // tn),
        in_specs=[pl.BlockSpec((nr, D_MODEL), lambda l, j: (0, 0)),
                  pl.BlockSpec((1, D_MODEL, tn), lambda l, j: (l, 0, j)),
                  pl.BlockSpec((1, 1, tn), lambda l, j: (l, 0, j))],
        out_specs=pl.BlockSpec((1, nr, tn), lambda l, j: (l, 0, j)),
        out_shape=jax.ShapeDtypeStruct((nl, nr, 3 * D_MODEL), F32),
        compiler_params=_params("parallel", "parallel"),
        name="ada_mod",
    )(c_all, w_mod, b_mod.reshape(nl, 1, 3 * D_MODEL))


def _gdn_front_math(x_ref, sh_ref, sc_ref, nw_ref, wq_ref, wba_ref, alog_ref, dtb_ref, chunk):
    x = x_ref[...]
    nb, tt, _ = x.shape
    r = nb * tt
    h = _rms(x, nw_ref[...]) * (1.0 + sc_ref[...]) + sh_ref[...]
    hb = h.reshape(r, D_MODEL).astype(BF16)
    pq = jnp.dot(hb, wq_ref[...], preferred_element_type=F32)
    ba = jnp.dot(hb, wba_ref[...], preferred_element_type=F32)
    beta = _sigmoid(ba)
    g = -jnp.exp(alog_ref[...]) * _softplus(ba + dtb_ref[...])
    ri = lax.broadcasted_iota(jnp.int32, (r, r), 0)
    ci = lax.broadcasted_iota(jnp.int32, (r, r), 1)
    same = (ri >> _log2(chunk)) == (ci >> _log2(chunk))
    low = jnp.where(same & (ri >= ci), 1.0, 0.0).astype(BF16)
    ones = jnp.where(same, 1.0, 0.0).astype(BF16)
    gc = jnp.zeros((r, LANES), F32)
    gl = jnp.zeros((r, LANES), F32)
    for piece in _split_bf16(g, 3):
        gc = gc + jnp.dot(low, piece, preferred_element_type=F32)
        gl = gl + jnp.dot(ones, piece, preferred_element_type=F32)
    return pq, jnp.concatenate([beta, gc, gl], axis=-1)


def _gdn_front_body(x_ref, sh_ref, sc_ref, nw_ref, wq_ref, wba_ref, alog_ref, dtb_ref,
                    qkv_ref, z_ref, gt_ref, *, chunk):
    nb, tt, _ = x_ref.shape
    pq, gates = _gdn_front_math(x_ref, sh_ref, sc_ref, nw_ref, wq_ref, wba_ref, alog_ref, dtb_ref, chunk)
    qkv_ref[...] = pq[:, :CONV_DIM].reshape(nb, tt, CONV_DIM)
    z_ref[...] = pq[:, CONV_DIM:].reshape(nb, tt, QK_DIM)
    gt_ref[...] = gates.reshape(nb, tt, 3 * LANES)


def _gdn_front(x, shift, scale, nw, wq, wba, alog, dtb, *, nb, tt, chunk):
    b, l, _ = x.shape
    tok = lambda w: pl.BlockSpec((nb, tt, w), lambda i, j: (i, j, 0))
    mod = lambda col: pl.BlockSpec((nb, 1, D_MODEL), lambda i, j: (i, 0, col))
    full = _resident
    return pl.pallas_call(
        functools.partial(_gdn_front_body, chunk=chunk),
        grid=(b // nb, l // tt),
        in_specs=[tok(D_MODEL), mod(0), mod(1), full(nw), full(wq), full(wba), full(alog), full(dtb)],
        out_specs=[tok(CONV_DIM), tok(QK_DIM), tok(3 * LANES)],
        out_shape=[jax.ShapeDtypeStruct((b, l, CONV_DIM), F32),
                   jax.ShapeDtypeStruct((b, l, QK_DIM), F32),
                   jax.ShapeDtypeStruct((b, l, 3 * LANES), F32)],
        compiler_params=_params("parallel", "parallel"),
        name="gdn_front",
    )(x, shift, scale, nw, wq, wba, alog, dtb)


def _inv_unit_lower(a, ri, ci, chunk):
    base = min(16, chunk)
    same = lambda size: (ri >> _log2(size)) == (ci >> _log2(size))
    n = jnp.where(same(base), -a, 0.0)
    q, m, k = n, n, 1
    while 2 * k < base:
        m = _bmm(m, m)
        q = q + m + _bmm(q, m)
        k *= 2
    size = base
    while size < chunk:
        e = jnp.where(same(2 * size) & jnp.logical_not(same(size)), a, 0.0)
        x = e + _bmm(q, e)
        q = q - (x + _bmm(x, q))
        size *= 2
    return q


def _gdn_delta_body(*refs, nb, tt, chunk, has_state, fused):
    it = iter(refs)
    take = lambda n: [next(it) for _ in range(n)]
    if fused:
        front_refs = take(8)
    else:
        qkv_ref, z_ref, gt_ref = take(3)
    cw_ref, on_ref = take(2)
    if has_state:
        s0_ref, c0_ref = take(2)
    o_ref, s_ref, cn_ref, xc, qkv_s, o_s, u_s, wq_s, qk_s, kdt_s = take(10)
    if fused:
        z_ref, gt_ref = take(2)
        pq, gates = _gdn_front_math(*front_refs, chunk)
        z_ref[...] = pq[:, CONV_DIM:].reshape(nb, tt, QK_DIM)
        gt_ref[...] = gates.reshape(nb, tt, 3 * LANES)
        qkv_in = pq[:, :CONV_DIM].reshape(nb, tt, CONV_DIM)
    else:
        qkv_in = qkv_ref[...]
    hist = CONV_TAPS - 1
    pad = SUBLANES

    @pl.when(pl.program_id(1) == 0)
    def _():
        if has_state:
            s_ref[...] = s0_ref[...]
            xc[:, pad - hist:pad, :] = c0_ref[...]
        else:
            s_ref[...] = jnp.zeros(s_ref.shape, F32)
            xc[:, pad - hist:pad, :] = jnp.zeros((nb, hist, CONV_DIM), F32)

    xc[:, pad:pad + tt, :] = qkv_in
    cw = cw_ref[...]
    y = xc[:, pad - hist:pad - hist + tt, :] * cw[0:1]
    for j in range(1, CONV_TAPS):
        y = y + xc[:, pad - hist + j:pad - hist + j + tt, :] * cw[j:j + 1]
    qkv_s[...] = _silu(y)
    last = xc[:, pad + tt - hist:pad + tt, :]
    cn_ref[...] = last
    xc[:, pad - hist:pad, :] = last

    for hd in range(N_HEADS):
        for part, scl in ((0, HEAD_DIM ** -0.5), (1, 1.0)):
            lo = part * QK_DIM + hd * HEAD_DIM
            xh = qkv_s[:, :, lo:lo + HEAD_DIM]
            xh = xh * (lax.rsqrt(jnp.sum(xh * xh, axis=-1, keepdims=True) + EPS) * scl)
            qkv_s[:, :, lo:lo + HEAD_DIM] = xh

    per_stack = MXU_DIM // chunk
    units_all = [(s, hd) for s in range(nb) for hd in range(N_HEADS)]
    assert len(units_all) % per_stack == 0
    stacks = [units_all[i:i + per_stack] for i in range(0, len(units_all), per_stack)]
    r = MXU_DIM
    ri = lax.broadcasted_iota(jnp.int32, (r, r), 0)
    ci = lax.broadcasted_iota(jnp.int32, (r, r), 1)
    same = (ri >> _log2(chunk)) == (ci >> _log2(chunk))
    tril = same & (ri >= ci)
    strict = same & (ri > ci)
    eye = ri == ci

    n_chunks = tt // chunk
    dot = lambda p, q: jnp.dot(p, q, preferred_element_type=F32)

    def gather(f):
        return jnp.stack([jnp.concatenate([f(s, hd, slice(c * chunk, (c + 1) * chunk)) for s, hd in units], axis=0)
                          for c in range(n_chunks) for units in stacks], axis=0)

    qn = gather(lambda s, hd, rows: qkv_s[s, rows, hd * HEAD_DIM:(hd + 1) * HEAD_DIM])
    kn = gather(lambda s, hd, rows: qkv_s[s, rows, QK_DIM + hd * HEAD_DIM:QK_DIM + (hd + 1) * HEAD_DIM])
    v = gather(lambda s, hd, rows: qkv_s[s, rows, 2 * QK_DIM + hd * HEAD_DIM:2 * QK_DIM + (hd + 1) * HEAD_DIM])
    beta = gather(lambda s, hd, rows: _lane_col(gt_ref[s, rows, 0:LANES], hd))
    gc = gather(lambda s, hd, rows: _lane_col(gt_ref[s, rows, LANES:2 * LANES], N_HEADS + hd))
    gl = gather(lambda s, hd, rows: _lane_col(gt_ref[s, rows, 2 * LANES:3 * LANES], N_HEADS + hd))
    ns = qn.shape[0]
    eg = jnp.exp(gc)
    kb = kn * beta
    gc_row = jnp.sum(jnp.where(eye, gc, 0.0), axis=1, keepdims=True)
    decay = jnp.where(tril, jnp.exp(jnp.where(tril, gc - gc_row, 0.0)), 0.0)
    a = jnp.where(strict, _bmm_nt(kb, kn) * decay, 0.0)
    q_inv = _inv_unit_lower(a, ri, ci, chunk)
    rhs = jnp.concatenate([v * beta, kb * eg], axis=2)
    uw = rhs + _bmm(q_inv, rhs)
    u_s[...] = uw[:, :, :HEAD_DIM]
    per_unit = lambda t: t.reshape(ns, per_stack, chunk, HEAD_DIM)
    wq_s[...] = jnp.concatenate([per_unit(uw[:, :, HEAD_DIM:]), per_unit(qn * eg)], axis=2).reshape(
        ns, 2 * r, HEAD_DIM).astype(BF16)
    qk_s[...] = jnp.where(tril, _bmm_nt(qn, kn) * decay, 0.0).astype(BF16)
    kdt_s[...] = jnp.swapaxes(kn * jnp.exp(gl - gc), 1, 2).astype(BF16)

    unit_of_row = lax.broadcasted_iota(jnp.int32, (r, HEAD_DIM), 0) >> _log2(chunk)
    for c in range(n_chunks):
        rows = slice(c * chunk, (c + 1) * chunk)
        for k, units in enumerate(stacks):
            idx = c * len(stacks) + k
            ws, qs = [], []
            for i, (s, hd) in enumerate(units):
                both = dot(wq_s[idx, 2 * i * chunk:2 * (i + 1) * chunk, :], s_ref[s, hd].astype(BF16))
                ws.append(both[:chunk])
                qs.append(both[chunk:])
            v_new = (u_s[idx] - jnp.concatenate(ws, axis=0)).astype(BF16)
            o = jnp.concatenate(qs, axis=0) + dot(qk_s[idx], v_new)
            kdt = kdt_s[idx]
            for i, (s, hd) in enumerate(units):
                o_s[s, rows, hd * HEAD_DIM:(hd + 1) * HEAD_DIM] = o[i * chunk:(i + 1) * chunk]
                g_last = jnp.exp(_lane_col(gt_ref[s, c * chunk:c * chunk + 1, 2 * LANES:3 * LANES], N_HEADS + hd))
                upd = dot(kdt, jnp.where(unit_of_row == i, v_new, jnp.zeros_like(v_new)))
                s_ref[s, hd] = s_ref[s, hd] * g_last + upd

    on = on_ref[...]
    for hd in range(N_HEADS):
        sl = slice(hd * HEAD_DIM, (hd + 1) * HEAD_DIM)
        o_ref[:, :, sl] = _rms(o_s[:, :, sl], on) * _silu(z_ref[:, :, sl])


def _gdn_delta(front, cw, onorm, s0, c0, *, dims, nb, tt, chunk, fused):
    b, l = dims
    has_state = s0 is not None
    n_stacks = nb * tt * N_HEADS // MXU_DIM
    tok = lambda w: pl.BlockSpec((nb, tt, w), lambda i, j: (i, j, 0))
    mod = lambda col: pl.BlockSpec((nb, 1, D_MODEL), lambda i, j: (i, 0, col))
    full = _resident
    st_spec = pl.BlockSpec((nb, N_HEADS, HEAD_DIM, HEAD_DIM), lambda i, j: (i, 0, 0, 0))
    cv_spec = pl.BlockSpec((nb, CONV_TAPS - 1, CONV_DIM), lambda i, j: (i, 0, 0))
    if fused:
        in_specs = [tok(D_MODEL), mod(0), mod(1)] + [full(a) for a in front[3:]]
    else:
        in_specs = [tok(CONV_DIM), tok(QK_DIM), tok(3 * LANES)]
    in_specs += [full(cw), full(onorm)]
    args = list(front) + [cw, onorm]
    if has_state:
        in_specs += [st_spec, cv_spec]
        args += [s0, c0]
    scratch = [pltpu.VMEM((nb, tt + SUBLANES, CONV_DIM), F32),
               pltpu.VMEM((nb, tt, CONV_DIM), F32),
               pltpu.VMEM((nb, tt, QK_DIM), F32),
               pltpu.VMEM((n_stacks, MXU_DIM, HEAD_DIM), F32),
               pltpu.VMEM((n_stacks, 2 * MXU_DIM, HEAD_DIM), BF16),
               pltpu.VMEM((n_stacks, MXU_DIM, MXU_DIM), BF16),
               pltpu.VMEM((n_stacks, HEAD_DIM, MXU_DIM), BF16)]
    if fused:
        scratch += [pltpu.VMEM((nb, tt, QK_DIM), F32), pltpu.VMEM((nb, tt, 3 * LANES), F32)]
    return pl.pallas_call(
        functools.partial(_gdn_delta_body, nb=nb, tt=tt, chunk=chunk, has_state=has_state, fused=fused),
        grid=(b // nb, l // tt),
        in_specs=in_specs,
        out_specs=[tok(QK_DIM), st_spec, cv_spec],
        out_shape=[jax.ShapeDtypeStruct((b, l, QK_DIM), F32),
                   jax.ShapeDtypeStruct((b, N_HEADS, HEAD_DIM, HEAD_DIM), F32),
                   jax.ShapeDtypeStruct((b, CONV_TAPS - 1, CONV_DIM), F32)],
        scratch_shapes=scratch,
        compiler_params=_params("parallel", "arbitrary"),
        name="gdn_delta",
    )(*args)


def _route(logits):
    lane = lax.broadcasted_iota(jnp.int32, logits.shape, 1)
    neg, big = -1e30, 1 << 20
    rmax = lambda t: jnp.max(t, axis=1, keepdims=True)
    rsum = lambda t: jnp.sum(t, axis=1, keepdims=True)
    first = lambda cond: jnp.min(jnp.where(cond, lane, big), axis=1, keepdims=True)
    is_g = lane < N_GROUPS
    lg = jnp.where(is_g, logits, neg)
    mg = rmax(lg)
    g_idx = first(is_g & (lg >= mg))
    p_group = 1.0 / rsum(jnp.where(is_g, jnp.exp(lg - mg), 0.0))
    lo = N_GROUPS + EXPERTS_PER_GROUP * g_idx
    in_g = (lane >= lo) & (lane < lo + EXPERTS_PER_GROUP)
    le = jnp.where(in_g, logits, neg)
    m1 = rmax(le)
    se = rsum(jnp.where(in_g, jnp.exp(le - m1), 0.0))
    i1 = first(in_g & (le >= m1))
    le2 = jnp.where(lane == i1, neg, le)
    m2 = rmax(le2)
    i2 = first(in_g & (lane != i1) & (le2 >= m2))
    p1 = 1.0 / se
    p2 = jnp.exp(m2 - m1) / se
    tot = p1 + p2
    return (jnp.where(lane == i1, p1 / tot * p_group, 0.0)
            + jnp.where(lane == i2, p2 / tot * p_group, 0.0))


def _moe_body(*refs, nb, tt, has_proj, final_norm):
    it = iter(refs)
    x_ref = next(it)
    if has_proj:
        y_ref, wo_ref, gm_ref = next(it), next(it), next(it)
    sh_ref, sc_ref, gf_ref, nw_ref, wr_ref, br_ref, w1_ref, w3_ref, w2_ref = (next(it) for _ in range(9))
    if final_norm:
        nf_ref = next(it)
    out_ref = next(it)
    r = nb * tt
    dot = lambda p, q: jnp.dot(p, q, preferred_element_type=F32)

    x = x_ref[...].reshape(nb, tt, D_MODEL)
    if has_proj:
        x = x + gm_ref[...] * _mm(y_ref[...].reshape(r, D_MODEL), wo_ref[...]).reshape(nb, tt, D_MODEL)
    h = (_rms(x, nw_ref[...]) * (1.0 + sc_ref[...]) + sh_ref[...]).reshape(r, D_MODEL)
    h_hi, h_lo = _split_bf16(h, 2)
    w_hi, w_lo = _split_bf16(wr_ref[...], 2)
    logits = dot(h_hi, w_hi) + (dot(h_hi, w_lo) + dot(h_lo, w_hi)) + br_ref[...]
    comb = _route(logits)

    gw = EXPERTS_PER_GROUP * D_FF
    acc = jnp.zeros((r, D_MODEL), F32)
    for g in range(N_GROUPS):
        cols = slice(g * gw, (g + 1) * gw)
        hid = _silu(dot(h_hi, w1_ref[:, cols])) * dot(h_hi, w3_ref[:, cols])
        parts = []
        for j in range(EXPERTS_PER_GROUP):
            ce = _lane_col(comb, N_GROUPS + g * EXPERTS_PER_GROUP + j)
            parts.append((hid[:, j * D_FF:(j + 1) * D_FF] * ce).astype(BF16))
        acc = acc + dot(jnp.concatenate(parts, axis=1), w2_ref[cols, :])

    out = x + gf_ref[...] * acc.reshape(nb, tt, D_MODEL)
    if final_norm:
        out = _rms(out, nf_ref[...])
    out_ref[...] = out.reshape(out_ref.shape)


def _moe(x, y, wo, mod_mix, mod_ffn, nw, wr, br, w1, w3, w2, nf, *, dims, nb, tt,
         in_time_major, out_time_major):
    b, l = dims
    has_proj, final_norm = y is not None, nf is not None
    grid = (b // nb, l // tt)
    bm = pl.BlockSpec((nb, tt, D_MODEL), lambda i, j: (i, j, 0))
    tm = pl.BlockSpec((tt, D_MODEL), lambda i, j: (j, i))
    mod = lambda col: pl.BlockSpec((nb, 1, D_MODEL), lambda i, j: (i, 0, col))
    full = _resident
    x_spec = tm if in_time_major else bm
    in_specs, args = [x_spec], [x]
    if has_proj:
        in_specs += [x_spec, full(wo), mod(2)]
        args += [y, wo, mod_mix]
    in_specs += [mod(0), mod(1), mod(2)] + [full(a) for a in (nw, wr, br, w1, w3, w2)]
    args += [mod_ffn, mod_ffn, mod_ffn, nw, wr, br, w1, w3, w2]
    if final_norm:
        in_specs.append(full(nf))
        args.append(nf)
    if out_time_major:
        assert nb == 1
        out_spec, out_shape = tm, jax.ShapeDtypeStruct((l, b * D_MODEL), F32)
    else:
        out_spec, out_shape = bm, jax.ShapeDtypeStruct((b, l, D_MODEL), F32)
    return pl.pallas_call(
        functools.partial(_moe_body, nb=nb, tt=tt, has_proj=has_proj, final_norm=final_norm),
        grid=grid,
        in_specs=in_specs,
        out_specs=out_spec,
        out_shape=out_shape,
        compiler_params=_params("parallel", "parallel"),
        name="moe",
    )(*args)


def _s5_body(*refs, tt, nbs, has_state):
    if has_state:
        (x_ref, sh_ref, sc_ref, gm_ref, nw_ref, win_ref, wbr_ref, wbi_ref, wcr_ref, wci_ref, dsk_ref,
         wglu_ref, abr_ref, abi_ref, s0r_ref, s0i_ref, out_ref, sr_ref, si_ref, xr_s, xi_s) = refs
    else:
        (x_ref, sh_ref, sc_ref, gm_ref, nw_ref, win_ref, wbr_ref, wbi_ref, wcr_ref, wci_ref, dsk_ref,
         wglu_ref, abr_ref, abi_ref, out_ref, sr_ref, si_ref, xr_s, xi_s) = refs
    r = tt * nbs
    blocks = D_MODEL // MXU_DIM
    sw = SSM_STATE // blocks

    @pl.when(pl.program_id(1) == 0)
    def _():
        if has_state:
            sr_ref[...] = s0r_ref[...]
            si_ref[...] = s0i_ref[...]
        else:
            sr_ref[...] = jnp.zeros(sr_ref.shape, F32)
            si_ref[...] = jnp.zeros(si_ref.shape, F32)

    x = x_ref[...]
    h = (_rms(x, nw_ref[...]) * (1.0 + sc_ref[...]) + sh_ref[...]).reshape(r, D_MODEL)
    u = _mm(h, win_ref[...])
    ub = u.astype(BF16)
    for j in range(blocks):
        uj = ub[:, j * MXU_DIM:(j + 1) * MXU_DIM]
        xr_s[:, :, j * sw:(j + 1) * sw] = jnp.dot(
            uj, wbr_ref[j], preferred_element_type=F32).reshape(tt, nbs, sw)
        xi_s[:, :, j * sw:(j + 1) * sw] = jnp.dot(
            uj, wbi_ref[j], preferred_element_type=F32).reshape(tt, nbs, sw)

    for rg in range(nbs // SUBLANES):
        rows = slice(rg * SUBLANES, (rg + 1) * SUBLANES)
        for ch in range(SSM_STATE // SCAN_LANES):
            lanes = slice(ch * SCAN_LANES, (ch + 1) * SCAN_LANES)
            ar = jnp.broadcast_to(abr_ref[:, lanes], (SUBLANES, SCAN_LANES))
            ai = jnp.broadcast_to(abi_ref[:, lanes], (SUBLANES, SCAN_LANES))

            hr, hi = sr_ref[rows, lanes], si_ref[rows, lanes]
            for t in range(tt):
                hr, hi = (ar * hr - ai * hi + xr_s[t, rows, lanes],
                          ar * hi + ai * hr + xi_s[t, rows, lanes])
                xr_s[t, rows, lanes] = hr
                xi_s[t, rows, lanes] = hi
            sr_ref[rows, lanes] = hr
            si_ref[rows, lanes] = hi

    ys = []
    for j in range(blocks):
        st_r = xr_s[:, :, j * sw:(j + 1) * sw].reshape(r, sw)
        st_i = xi_s[:, :, j * sw:(j + 1) * sw].reshape(r, sw)
        ys.append(_mm(st_r, wcr_ref[j]) - _mm(st_i, wci_ref[j]))
    y = jnp.concatenate(ys, axis=1) + dsk_ref[...] * u
    ag = _mm(jax.nn.gelu(y), wglu_ref[...])
    mix = ag[:, :D_MODEL] * _sigmoid(ag[:, D_MODEL:])
    out_ref[...] = x + gm_ref[...] * mix.reshape(tt, nbs, D_MODEL)


def _s5(x, mod, nw, win, wbr, wbi, wcr, wci, dsk, wglu, abr, abi, s0r, s0i, *, tt, nbs):
    l, b, _ = x.shape
    has_state = s0r is not None
    xs = pl.BlockSpec((tt, nbs, D_MODEL), lambda i, j: (j, i, 0))
    mods = lambda col: pl.BlockSpec((nbs, D_MODEL), lambda i, j: (i, col))
    full = _resident
    st = pl.BlockSpec((nbs, SSM_STATE), lambda i, j: (i, 0))
    consts = [nw, win, wbr, wbi, wcr, wci, dsk, wglu, abr, abi]
    in_specs = [xs, mods(0), mods(1), mods(2)] + [full(a) for a in consts]
    args = [x, mod, mod, mod] + consts
    if has_state:
        in_specs += [st, st]
        args += [s0r, s0i]
    return pl.pallas_call(
        functools.partial(_s5_body, tt=tt, nbs=nbs, has_state=has_state),
        grid=(b // nbs, l // tt),
        in_specs=in_specs,
        out_specs=[xs, st, st],
        out_shape=[jax.ShapeDtypeStruct((l, b, D_MODEL), F32),
                   jax.ShapeDtypeStruct((b, SSM_STATE), F32),
                   jax.ShapeDtypeStruct((b, SSM_STATE), F32)],
        scratch_shapes=[pltpu.VMEM((tt, nbs, SSM_STATE), F32), pltpu.VMEM((tt, nbs, SSM_STATE), F32)],
        compiler_params=_params("parallel", "arbitrary"),
        name="s5",
    )(*args)


def _s5_discretize(lam_re, lam_im, log_dt, b_re, b_im, c_re, c_im):
    dt = jnp.exp(log_dt)[:, None]
    mag = jnp.exp(lam_re * dt)
    ang = lam_im * dt
    ab_re, ab_im = mag * jnp.cos(ang), mag * jnp.sin(ang)
    den = lam_re * lam_re + lam_im * lam_im
    f_re = ((ab_re - 1.0) * lam_re + ab_im * lam_im) / den
    f_im = (ab_im * lam_re - (ab_re - 1.0) * lam_im) / den
    bb_re = f_re[..., None] * b_re - f_im[..., None] * b_im
    bb_im = f_re[..., None] * b_im + f_im[..., None] * b_re
    blocks = D_MODEL // MXU_DIM
    gpb = SSM_GROUPS // blocks
    eye = jnp.eye(gpb, dtype=F32)

    def b_blocks(bb):
        t = bb.reshape(blocks, gpb, SSM_P, SSM_GROUP)
        w = jnp.einsum('jgpc,gh->jgchp', t, eye)
        return w.reshape(blocks, gpb * SSM_GROUP, gpb * SSM_P).astype(BF16)

    def c_blocks(cc):
        t = cc.reshape(blocks, gpb, SSM_GROUP, SSM_P)
        w = jnp.einsum('jgcp,gh->jgphc', t, eye)
        return w.reshape(blocks, gpb * SSM_P, gpb * SSM_GROUP).astype(BF16)

    return (ab_re.reshape(1, SSM_STATE), ab_im.reshape(1, SSM_STATE),
            b_blocks(bb_re), b_blocks(bb_im), c_blocks(c_re), c_blocks(c_im))


def _trunk(x, mods_mix, mods_ffn, conv0, delta0, re0, im0, wts, *, nb_tok, tt_tok, nb_moe, tt_moe, nb_delta, tt_delta, chunk,
           tt_s5, nbs_s5):
    b, l, _ = x.shape
    mm3 = [m.reshape(b, 1, 3 * D_MODEL) for m in mods_mix]
    mf3 = [m.reshape(b, 1, 3 * D_MODEL) for m in mods_ffn]

    front = (x, mm3[0], mm3[0], wts['norm_mix'][0], wts['gdn_wq'], wts['gdn_wba'], wts['gdn_alog'], wts['gdn_dtb'])
    fused = nb_delta * tt_delta >= ROWS
    if not fused:
        front = _gdn_front(*front, nb=nb_tok, tt=tt_tok, chunk=chunk)
    o, s_delta, s_conv = _gdn_delta(front, wts['gdn_cw'], wts['gdn_onorm'], delta0, conv0, dims=(b, l),
                                    nb=nb_delta, tt=tt_delta, chunk=chunk, fused=fused)
    prompt_like = nb_tok == 1
    moe_w = lambda i: (wts['norm_ffn'][i], wts['moe_wr'][i], wts['moe_br'][i],
                       wts['moe_w1'][i], wts['moe_w3'][i], wts['moe_w2'][i])
    x1 = _moe(x, o, wts['gdn_wout'], mm3[0], mf3[0], *moe_w(0), None, dims=(b, l), nb=nb_moe, tt=tt_moe,
              in_time_major=False, out_time_major=prompt_like)
    xt = x1.reshape(l, b, D_MODEL) if prompt_like else jnp.swapaxes(x1, 0, 1)
    x2, s_re, s_im = _s5(xt, mods_mix[1], wts['norm_mix'][1], wts['s5_win'], wts['s5_wbr'], wts['s5_wbi'],
                         wts['s5_wcr'], wts['s5_wci'], wts['s5_d'], wts['s5_wglu'], wts['s5_abr'], wts['s5_abi'],
                         re0, im0, tt=tt_s5, nbs=nbs_s5)
    x2 = x2.reshape(l, b * D_MODEL) if prompt_like else jnp.swapaxes(x2, 0, 1)
    y = _moe(x2, None, None, None, mf3[1], *moe_w(1), wts['norm_final'], dims=(b, l), nb=nb_moe, tt=tt_moe,
             in_time_major=prompt_like, out_time_major=False)
    return (y, s_delta[None], s_conv[None],
            s_re.reshape(1, b, SSM_GROUPS, SSM_P), s_im.reshape(1, b, SSM_GROUPS, SSM_P))


def kernel(x_prompt, x_sample, state_delta, state_conv, state_ssm_re, state_ssm_im, c_prompt, c_sample, norm_mix, w_mod_mix, b_mod_mix, norm_ffn, w_mod_ffn, b_mod_ffn, norm_final, gdn_w_in, gdn_conv_w, gdn_a_log, gdn_dt_bias, gdn_o_norm, gdn_w_out, s5_w_in, s5_lam_re, s5_lam_im, s5_log_dt, s5_b_re, s5_b_im, s5_c_re, s5_c_im, s5_d, s5_w_glu, moe_w_rg, moe_b_rg, moe_w_re, moe_b_re, moe_w1, moe_w3, moe_w2):
    bp = x_prompt.shape[0]
    depth = norm_mix.shape[0]

    lane_pad = lambda a, lo: jnp.pad(a, [(0, 0)] * (a.ndim - 1) + [(lo, LANES - lo - a.shape[-1])])
    abr, abi, wbr, wbi, wcr, wci = _s5_discretize(s5_lam_re[0], s5_lam_im[0], s5_log_dt[0], s5_b_re[0],
                                                  s5_b_im[0], s5_c_re[0], s5_c_im[0])
    wts = dict(
        norm_mix=norm_mix.reshape(depth, 1, D_MODEL), norm_ffn=norm_ffn.reshape(depth, 1, D_MODEL),
        norm_final=norm_final.reshape(1, D_MODEL),
        gdn_wq=gdn_w_in[0, :, :CONV_DIM + QK_DIM].astype(BF16),
        gdn_wba=lane_pad(gdn_w_in[0, :, CONV_DIM + QK_DIM:], 0).astype(BF16),
        gdn_alog=lane_pad(gdn_a_log[0][None], N_HEADS), gdn_dtb=lane_pad(gdn_dt_bias[0][None], N_HEADS),
        gdn_cw=jnp.pad(gdn_conv_w[0], ((0, SUBLANES - CONV_TAPS), (0, 0))),
        gdn_onorm=gdn_o_norm[0][None], gdn_wout=gdn_w_out[0].astype(BF16),
        s5_win=s5_w_in[0].astype(BF16), s5_wbr=wbr, s5_wbi=wbi, s5_wcr=wcr, s5_wci=wci,
        s5_d=s5_d[0].reshape(1, D_MODEL), s5_wglu=s5_w_glu[0].astype(BF16), s5_abr=abr, s5_abi=abi,
        moe_wr=lane_pad(jnp.concatenate([moe_w_rg, moe_w_re], axis=-1), 0),
        moe_br=lane_pad(jnp.concatenate([moe_b_rg, moe_b_re], axis=-1), 0)[:, None, :],
        moe_w1=jnp.swapaxes(moe_w1, 1, 2).reshape(depth, D_MODEL, N_EXPERTS * D_FF).astype(BF16),
        moe_w3=jnp.swapaxes(moe_w3, 1, 2).reshape(depth, D_MODEL, N_EXPERTS * D_FF).astype(BF16),
        moe_w2=moe_w2.reshape(depth, N_EXPERTS * D_FF, D_MODEL).astype(BF16),
    )

    c_all = jnp.concatenate([c_prompt, c_sample], axis=0)
    m_mix = _ada_mod(c_all, w_mod_mix, b_mod_mix)
    m_ffn = _ada_mod(c_all, w_mod_ffn, b_mod_ffn)

    y_p, p_delta, p_conv, p_re, p_im = _trunk(
        x_prompt, [m_mix[i, :bp] for i in range(depth)], [m_ffn[i, :bp] for i in range(depth)],
        None, None, None, None, wts,
        nb_tok=1, tt_tok=ROWS, nb_moe=1, tt_moe=MOE_ROWS, nb_delta=1, tt_delta=ROWS, chunk=CHUNK, tt_s5=ROWS // SUBLANES, nbs_s5=SUBLANES)
    ls = x_sample.shape[1]
    y_s, s_delta, s_conv, s_re, s_im = _trunk(
        x_sample, [m_mix[i, bp:] for i in range(depth)], [m_ffn[i, bp:] for i in range(depth)],
        state_conv[0], state_delta[0], state_ssm_re[0].reshape(-1, SSM_STATE),
        state_ssm_im[0].reshape(-1, SSM_STATE), wts,
        nb_tok=ROWS // ls, tt_tok=ls, nb_moe=MOE_ROWS // ls, tt_moe=ls, nb_delta=MXU_DIM // (ls * N_HEADS), tt_delta=ls, chunk=ls,
        tt_s5=ls, nbs_s5=ROWS // ls)
    return (y_p, y_s, p_delta, p_conv, p_re, p_im, s_delta, s_conv, s_re, s_im)
```

```python
import functools

import jax
import jax.numpy as jnp
from jax import lax
from jax.experimental import pallas as pl
from jax.experimental.pallas import tpu as pltpu

F32 = jnp.float32
BF16 = jnp.bfloat16

D_MODEL = 1024
N_HEADS = 8
HEAD_DIM = 128
QK_DIM = N_HEADS * HEAD_DIM
CONV_DIM = 3 * QK_DIM
CONV_TAPS = 4
CHUNK = 64
SSM_GROUPS = 64
SSM_GROUP = 16
SSM_P = 64
SSM_STATE = SSM_GROUPS * SSM_P
N_GROUPS = 4
EXPERTS_PER_GROUP = 4
N_EXPERTS = 16
D_FF = 256
EPS = 1e-6

LANES = 128
SUBLANES = 8
MXU_DIM = 256
ROWS = 256
MOE_ROWS = 512
SCAN_LANES = 512
VMEM_LIMIT = 56 * 1024 * 1024


def _mm(a, b):
    return jnp.dot(a.astype(BF16), b.astype(BF16), preferred_element_type=F32)


def _bmm(a, b):
    return lax.dot_general(a.astype(BF16), b.astype(BF16), (((2,), (1,)), ((0,), (0,))),
                           preferred_element_type=F32)


def _bmm_nt(a, b):
    return lax.dot_general(a.astype(BF16), b.astype(BF16), (((2,), (2,)), ((0,), (0,))),
                           preferred_element_type=F32)


def _split_bf16(x, terms):
    out = []
    for _ in range(terms - 1):
        p = x.astype(BF16)
        out.append(p)
        x = x - p.astype(F32)
    out.append(x.astype(BF16))
    return out


def _rms(x, w):
    return x * lax.rsqrt(jnp.mean(x * x, axis=-1, keepdims=True) + EPS) * w


def _sigmoid(x):
    return 0.5 * jnp.tanh(0.5 * x) + 0.5


def _silu(x):
    h = 0.5 * x
    return h * jnp.tanh(h) + h


def _softplus(x):
    return jnp.maximum(x, 0.0) + jnp.log1p(jnp.exp(-jnp.abs(x)))


def _lane_col(tile, lane):
    li = lax.broadcasted_iota(jnp.int32, tile.shape, 1)
    return jnp.sum(jnp.where(li == lane, tile, 0.0), axis=1, keepdims=True)


def _log2(n):
    assert n & (n - 1) == 0
    return n.bit_length() - 1


def _params(*sem):
    return pltpu.CompilerParams(dimension_semantics=sem, vmem_limit_bytes=VMEM_LIMIT)


def _resident(a):
    return pl.BlockSpec(a.shape, lambda *_: (0,) * a.ndim, pipeline_mode=pl.Buffered(1))


def _mod_body(c_ref, w_ref, b_ref, o_ref):
    o_ref[0] = _mm(_silu(c_ref[...]), w_ref[0]) + b_ref[0]


def _ada_mod(c_all, w_mod, b_mod):
    nl, nr, tn = w_mod.shape[0], c_all.shape[0], 768
    return pl.pallas_call(
        _mod_body,
        grid=(nl, 3 * D_MODEL // tn),
        in_specs=[pl.BlockSpec((nr, D_MODEL), lambda l, j: (0, 0)),
                  pl.BlockSpec((1, D_MODEL, tn), lambda l, j: (l, 0, j)),
                  pl.BlockSpec((1, 1, tn), lambda l, j: (l, 0, j))],
        out_specs=pl.BlockSpec((1, nr, tn), lambda l, j: (l, 0, j)),
        out_shape=jax.ShapeDtypeStruct((nl, nr, 3 * D_MODEL), F32),
        compiler_params=_params("parallel", "parallel"),
        name="ada_mod",
    )(c_all, w_mod, b_mod.reshape(nl, 1, 3 * D_MODEL))


def _gdn_front_math(x_ref, sh_ref, sc_ref, nw_ref, wq_ref, wba_ref, alog_ref, dtb_ref, chunk):
    x = x_ref[...]
    nb, tt, _ = x.shape
    r = nb * tt
    h = _rms(x, nw_ref[...]) * (1.0 + sc_ref[...]) + sh_ref[...]
    hb = h.reshape(r, D_MODEL).astype(BF16)
    pq = jnp.dot(hb, wq_ref[...], preferred_element_type=F32)
    ba = jnp.dot(hb, wba_ref[...], preferred_element_type=F32)
    beta = _sigmoid(ba)
    g = -jnp.exp(alog_ref[...]) * _softplus(ba + dtb_ref[...])
    ri = lax.broadcasted_iota(jnp.int32, (r, r), 0)
    ci = lax.broadcasted_iota(jnp.int32, (r, r), 1)
    same = (ri >> _log2(chunk)) == (ci >> _log2(chunk))
    low = jnp.where(same & (ri >= ci), 1.0, 0.0).astype(BF16)
    ones = jnp.where(same, 1.0, 0.0).astype(BF16)
    gc = jnp.zeros((r, LANES), F32)
    gl = jnp.zeros((r, LANES), F32)
    for piece in _split_bf16(g, 3):
        gc = gc + jnp.dot(low, piece, preferred_element_type=F32)
        gl = gl + jnp.dot(ones, piece, preferred_element_type=F32)
    return pq, jnp.concatenate([beta, gc, gl], axis=-1)


def _gdn_front_body(x_ref, sh_ref, sc_ref, nw_ref, wq_ref, wba_ref, alog_ref, dtb_ref,
                    qkv_ref, z_ref, gt_ref, *, chunk):
    nb, tt, _ = x_ref.shape
    pq, gates = _gdn_front_math(x_ref, sh_ref, sc_ref, nw_ref, wq_ref, wba_ref, alog_ref, dtb_ref, chunk)
    qkv_ref[...] = pq[:, :CONV_DIM].reshape(nb, tt, CONV_DIM)
    z_ref[...] = pq[:, CONV_DIM:].reshape(nb, tt, QK_DIM)
    gt_ref[...] = gates.reshape(nb, tt, 3 * LANES)


def _gdn_front(x, shift, scale, nw, wq, wba, alog, dtb, *, nb, tt, chunk):
    b, l, _ = x.shape
    tok = lambda w: pl.BlockSpec((nb, tt, w), lambda i, j: (i, j, 0))
    mod = lambda col: pl.BlockSpec((nb, 1, D_MODEL), lambda i, j: (i, 0, col))
    full = _resident
    return pl.pallas_call(
        functools.partial(_gdn_front_body, chunk=chunk),
        grid=(b // nb, l // tt),
        in_specs=[tok(D_MODEL), mod(0), mod(1), full(nw), full(wq), full(wba), full(alog), full(dtb)],
        out_specs=[tok(CONV_DIM), tok(QK_DIM), tok(3 * LANES)],
        out_shape=[jax.ShapeDtypeStruct((b, l, CONV_DIM), F32),
                   jax.ShapeDtypeStruct((b, l, QK_DIM), F32),
                   jax.ShapeDtypeStruct((b, l, 3 * LANES), F32)],
        compiler_params=_params("parallel", "parallel"),
        name="gdn_front",
    )(x, shift, scale, nw, wq, wba, alog, dtb)


def _inv_unit_lower(a, ri, ci, chunk):
    base = min(16, chunk)
    same = lambda size: (ri >> _log2(size)) == (ci >> _log2(size))
    n = jnp.where(same(base), -a, 0.0)
    q, m, k = n, n, 1
    while 2 * k < base:
        m = _bmm(m, m)
        q = q + m + _bmm(q, m)
        k *= 2
    size = base
    while size < chunk:
        e = jnp.where(same(2 * size) & jnp.logical_not(same(size)), a, 0.0)
        x = e + _bmm(q, e)
        q = q - (x + _bmm(x, q))
        size *= 2
    return q


def _gdn_delta_body(*refs, nb, tt, chunk, has_state, fused):
    it = iter(refs)
    take = lambda n: [next(it) for _ in range(n)]
    if fused:
        front_refs = take(8)
    else:
        qkv_ref, z_ref, gt_ref = take(3)
    cw_ref, on_ref = take(2)
    if has_state:
        s0_ref, c0_ref = take(2)
    o_ref, s_ref, cn_ref, xc, qkv_s, o_s, u_s, wq_s, qk_s, kdt_s = take(10)
    if fused:
        z_ref, gt_ref = take(2)
        pq, gates = _gdn_front_math(*front_refs, chunk)
        z_ref[...] = pq[:, CONV_DIM:].reshape(nb, tt, QK_DIM)
        gt_ref[...] = gates.reshape(nb, tt, 3 * LANES)
        qkv_in = pq[:, :CONV_DIM].reshape(nb, tt, CONV_DIM)
    else:
        qkv_in = qkv_ref[...]
    hist = CONV_TAPS - 1
    pad = SUBLANES

    @pl.when(pl.program_id(1) == 0)
    def _():
        if has_state:
            s_ref[...] = s0_ref[...]
            xc[:, pad - hist:pad, :] = c0_ref[...]
        else:
            s_ref[...] = jnp.zeros(s_ref.shape, F32)
            xc[:, pad - hist:pad, :] = jnp.zeros((nb, hist, CONV_DIM), F32)

    xc[:, pad:pad + tt, :] = qkv_in
    cw = cw_ref[...]
    y = xc[:, pad - hist:pad - hist + tt, :] * cw[0:1]
    for j in range(1, CONV_TAPS):
        y = y + xc[:, pad - hist + j:pad - hist + j + tt, :] * cw[j:j + 1]
    qkv_s[...] = _silu(y)
    last = xc[:, pad + tt - hist:pad + tt, :]
    cn_ref[...] = last
    xc[:, pad - hist:pad, :] = last

    for hd in range(N_HEADS):
        for part, scl in ((0, HEAD_DIM ** -0.5), (1, 1.0)):
            lo = part * QK_DIM + hd * HEAD_DIM
            xh = qkv_s[:, :, lo:lo + HEAD_DIM]
            xh = xh * (lax.rsqrt(jnp.sum(xh * xh, axis=-1, keepdims=True) + EPS) * scl)
            qkv_s[:, :, lo:lo + HEAD_DIM] = xh

    per_stack = MXU_DIM // chunk
    units_all = [(s, hd) for s in range(nb) for hd in range(N_HEADS)]
    assert len(units_all) % per_stack == 0
    stacks = [units_all[i:i + per_stack] for i in range(0, len(units_all), per_stack)]
    r = MXU_DIM
    ri = lax.broadcasted_iota(jnp.int32, (r, r), 0)
    ci = lax.broadcasted_iota(jnp.int32, (r, r), 1)
    same = (ri >> _log2(chunk)) == (ci >> _log2(chunk))
    tril = same & (ri >= ci)
    strict = same & (ri > ci)
    eye = ri == ci

    n_chunks = tt // chunk
    dot = lambda p, q: jnp.dot(p, q, preferred_element_type=F32)

    def gather(f):
        return jnp.stack([jnp.concatenate([f(s, hd, slice(c * chunk, (c + 1) * chunk)) for s, hd in units], axis=0)
                          for c in range(n_chunks) for units in stacks], axis=0)

    qn = gather(lambda s, hd, rows: qkv_s[s, rows, hd * HEAD_DIM:(hd + 1) * HEAD_DIM])
    kn = gather(lambda s, hd, rows: qkv_s[s, rows, QK_DIM + hd * HEAD_DIM:QK_DIM + (hd + 1) * HEAD_DIM])
    v = gather(lambda s, hd, rows: qkv_s[s, rows, 2 * QK_DIM + hd * HEAD_DIM:2 * QK_DIM + (hd + 1) * HEAD_DIM])
    beta = gather(lambda s, hd, rows: _lane_col(gt_ref[s, rows, 0:LANES], hd))
    gc = gather(lambda s, hd, rows: _lane_col(gt_ref[s, rows, LANES:2 * LANES], N_HEADS + hd))
    gl = gather(lambda s, hd, rows: _lane_col(gt_ref[s, rows, 2 * LANES:3 * LANES], N_HEADS + hd))
    ns = qn.shape[0]
    eg = jnp.exp(gc)
    kb = kn * beta
    gc_row = jnp.sum(jnp.where(eye, gc, 0.0), axis=1, keepdims=True)
    decay = jnp.where(tril, jnp.exp(jnp.where(tril, gc - gc_row, 0.0)), 0.0)
    a = jnp.where(strict, _bmm_nt(kb, kn) * decay, 0.0)
    q_inv = _inv_unit_lower(a, ri, ci, chunk)
    rhs = jnp.concatenate([v * beta, kb * eg], axis=2)
    uw = rhs + _bmm(q_inv, rhs)
    u_s[...] = uw[:, :, :HEAD_DIM]
    per_unit = lambda t: t.reshape(ns, per_stack, chunk, HEAD_DIM)
    wq_s[...] = jnp.concatenate([per_unit(uw[:, :, HEAD_DIM:]), per_unit(qn * eg)], axis=2).reshape(
        ns, 2 * r, HEAD_DIM).astype(BF16)
    qk_s[...] = jnp.where(tril, _bmm_nt(qn, kn) * decay, 0.0).astype(BF16)
    kdt_s[...] = jnp.swapaxes(kn * jnp.exp(gl - gc), 1, 2).astype(BF16)

    unit_of_row = lax.broadcasted_iota(jnp.int32, (r, HEAD_DIM), 0) >> _log2(chunk)
    for c in range(n_chunks):
        rows = slice(c * chunk, (c + 1) * chunk)
        for k, units in enumerate(stacks):
            idx = c * len(stacks) + k
            ws, qs = [], []
            for i, (s, hd) in enumerate(units):
                both = dot(wq_s[idx, 2 * i * chunk:2 * (i + 1) * chunk, :], s_ref[s, hd].astype(BF16))
                ws.append(both[:chunk])
                qs.append(both[chunk:])
            v_new = (u_s[idx] - jnp.concatenate(ws, axis=0)).astype(BF16)
            o = jnp.concatenate(qs, axis=0) + dot(qk_s[idx], v_new)
            kdt = kdt_s[idx]
            for i, (s, hd) in enumerate(units):
                o_s[s, rows, hd * HEAD_DIM:(hd + 1) * HEAD_DIM] = o[i * chunk:(i + 1) * chunk]
                g_last = jnp.exp(_lane_col(gt_ref[s, c * chunk:c * chunk + 1, 2 * LANES:3 * LANES], N_HEADS + hd))
                upd = dot(kdt, jnp.where(unit_of_row == i, v_new, jnp.zeros_like(v_new)))
                s_ref[s, hd] = s_ref[s, hd] * g_last + upd

    on = on_ref[...]
    for hd in range(N_HEADS):
        sl = slice(hd * HEAD_DIM, (hd + 1) * HEAD_DIM)
        o_ref[:, :, sl] = _rms(o_s[:, :, sl], on) * _silu(z_ref[:, :, sl])


def _gdn_delta(front, cw, onorm, s0, c0, *, dims, nb, tt, chunk, fused):
    b, l = dims
    has_state = s0 is not None
    n_stacks = nb * tt * N_HEADS // MXU_DIM
    tok = lambda w: pl.BlockSpec((nb, tt, w), lambda i, j: (i, j, 0))
    mod = lambda col: pl.BlockSpec((nb, 1, D_MODEL), lambda i, j: (i, 0, col))
    full = _resident
    st_spec = pl.BlockSpec((nb, N_HEADS, HEAD_DIM, HEAD_DIM), lambda i, j: (i, 0, 0, 0))
    cv_spec = pl.BlockSpec((nb, CONV_TAPS - 1, CONV_DIM), lambda i, j: (i, 0, 0))
    if fused:
        in_specs = [tok(D_MODEL), mod(0), mod(1)] + [full(a) for a in front[3:]]
    else:
        in_specs = [tok(CONV_DIM), tok(QK_DIM), tok(3 * LANES)]
    in_specs += [full(cw), full(onorm)]
    args = list(front) + [cw, onorm]
    if has_state:
        in_specs += [st_spec, cv_spec]
        args += [s0, c0]
    scratch = [pltpu.VMEM((nb, tt + SUBLANES, CONV_DIM), F32),
               pltpu.VMEM((nb, tt, CONV_DIM), F32),
               pltpu.VMEM((nb, tt, QK_DIM), F32),
               pltpu.VMEM((n_stacks, MXU_DIM, HEAD_DIM), F32),
               pltpu.VMEM((n_stacks, 2 * MXU_DIM, HEAD_DIM), BF16),
               pltpu.VMEM((n_stacks, MXU_DIM, MXU_DIM), BF16),
               pltpu.VMEM((n_stacks, HEAD_DIM, MXU_DIM), BF16)]
    if fused:
        scratch += [pltpu.VMEM((nb, tt, QK_DIM), F32), pltpu.VMEM((nb, tt, 3 * LANES), F32)]
    return pl.pallas_call(
        functools.partial(_gdn_delta_body, nb=nb, tt=tt, chunk=chunk, has_state=has_state, fused=fused),
        grid=(b // nb, l // tt),
        in_specs=in_specs,
        out_specs=[tok(QK_DIM), st_spec, cv_spec],
        out_shape=[jax.ShapeDtypeStruct((b, l, QK_DIM), F32),
                   jax.ShapeDtypeStruct((b, N_HEADS, HEAD_DIM, HEAD_DIM), F32),
                   jax.ShapeDtypeStruct((b, CONV_TAPS - 1, CONV_DIM), F32)],
        scratch_shapes=scratch,
        compiler_params=_params("parallel", "arbitrary"),
        name="gdn_delta",
    )(*args)


def _route(logits):
    lane = lax.broadcasted_iota(jnp.int32, logits.shape, 1)
    neg, big = -1e30, 1 << 20
    rmax = lambda t: jnp.max(t, axis=1, keepdims=True)
    rsum = lambda t: jnp.sum(t, axis=1, keepdims=True)
    first = lambda cond: jnp.min(jnp.where(cond, lane, big), axis=1, keepdims=True)
    is_g = lane < N_GROUPS
    lg = jnp.where(is_g, logits, neg)
    mg = rmax(lg)
    g_idx = first(is_g & (lg >= mg))
    p_group = 1.0 / rsum(jnp.where(is_g, jnp.exp(lg - mg), 0.0))
    lo = N_GROUPS + EXPERTS_PER_GROUP * g_idx
    in_g = (lane >= lo) & (lane < lo + EXPERTS_PER_GROUP)
    le = jnp.where(in_g, logits, neg)
    m1 = rmax(le)
    se = rsum(jnp.where(in_g, jnp.exp(le - m1), 0.0))
    i1 = first(in_g & (le >= m1))
    le2 = jnp.where(lane == i1, neg, le)
    m2 = rmax(le2)
    i2 = first(in_g & (lane != i1) & (le2 >= m2))
    p1 = 1.0 / se
    p2 = jnp.exp(m2 - m1) / se
    tot = p1 + p2
    local = (jnp.where(lane == i1 - lo, p1 / tot * p_group, 0.0)
             + jnp.where(lane == i2 - lo, p2 / tot * p_group, 0.0))
    return g_idx, local


META_GROUP, META_RANK = 4, 5
PIECE_LANES = 8
SORT_PAD = 16
EXPERT_ROWS = MOE_ROWS // N_GROUPS + 32
SORT_ROWS = MOE_ROWS + N_GROUPS * SORT_PAD + EXPERT_ROWS + 96
USED_ROWS = MOE_ROWS + LANES


def _moe_route_body(*refs, nb, tt, has_proj):
    it = iter(refs)
    x_ref = next(it)
    if has_proj:
        y_ref, wo_ref, gm_ref = next(it), next(it), next(it)
    sh_ref, sc_ref, nw_ref, wr_ref, br_ref, x1_ref, h_ref, meta_ref, cnt_ref = (next(it) for _ in range(9))
    r = nb * tt
    dot = lambda p, q: jnp.dot(p, q, preferred_element_type=F32)

    x = x_ref[...].reshape(nb, tt, D_MODEL)
    if has_proj:
        x = x + gm_ref[...] * _mm(y_ref[...].reshape(r, D_MODEL), wo_ref[...]).reshape(nb, tt, D_MODEL)
    h = (_rms(x, nw_ref[...]) * (1.0 + sc_ref[...]) + sh_ref[...]).reshape(r, D_MODEL)
    x1_ref[...] = x.reshape(r, D_MODEL)
    h_hi, h_lo = _split_bf16(h, 2)
    h_ref[...] = h_hi
    w_hi, w_lo = _split_bf16(wr_ref[...], 2)
    logits = dot(h_hi, w_hi) + (dot(h_hi, w_lo) + dot(h_lo, w_hi)) + br_ref[...]
    g_idx, local = _route(logits)

    lane = lax.broadcasted_iota(jnp.int32, (r, LANES), 1)
    onehot = jnp.where(lane == g_idx, 1.0, 0.0)
    ri = lax.broadcasted_iota(jnp.int32, (r, r), 0)
    ci = lax.broadcasted_iota(jnp.int32, (r, r), 1)
    earlier = jnp.where(ri > ci, 1.0, 0.0).astype(BF16)
    rank = jnp.sum(onehot * dot(earlier, onehot.astype(BF16)), axis=1, keepdims=True)
    meta_ref[...] = jnp.where(lane == META_GROUP, g_idx.astype(F32),
                              jnp.where(lane == META_RANK, rank, local))
    counts = jnp.sum(onehot, axis=0, keepdims=True).astype(jnp.int32)
    cnt_ref[...] = jnp.broadcast_to(counts, (1, SUBLANES, LANES))


def _moe_expert_body(cnt_ref, *refs, nb, tt, n_time_tiles, final_norm):
    it = iter(refs)
    x1_ref, h_ref, meta_ref, gf_ref, w1_ref, w3_ref, w2_ref = (next(it) for _ in range(7))
    if final_norm:
        nf_ref = next(it)
    out_ref, hs_s, ms_s, acc_s = (next(it) for _ in range(4))
    r = nb * tt
    dot = lambda p, q: jnp.dot(p, q, preferred_element_type=F32)
    tile = pl.program_id(0) * n_time_tiles + pl.program_id(1)

    counts = [cnt_ref[tile, g] for g in range(N_GROUPS)]
    starts, nxt = [], 0
    for g in range(N_GROUPS):
        starts.append(nxt)
        nxt = nxt + (((counts[g] + (SORT_PAD - 1)) >> _log2(SORT_PAD)) << _log2(SORT_PAD))

    meta = meta_ref[...]
    g_col = _lane_col(meta, META_GROUP)
    pos = _lane_col(meta, META_RANK)
    for g in range(N_GROUPS):
        pos = pos + jnp.where(g_col == g, jnp.asarray(starts[g], jnp.int32).astype(F32), 0.0)
    pos_row = jnp.broadcast_to(pos, (r, LANES)).T[0:1, :].astype(jnp.int32)
    perm = jnp.where(lax.broadcasted_iota(jnp.int32, (SORT_ROWS, r), 0) == pos_row, 1.0, 0.0).astype(BF16)
    unperm = jnp.where(lax.broadcasted_iota(jnp.int32, (r, USED_ROWS), 1) == pos.astype(jnp.int32),
                       1.0, 0.0).astype(BF16)

    hs_s[...] = dot(perm, h_ref[...]).astype(BF16)
    lane = lax.broadcasted_iota(jnp.int32, (r, LANES), 1)
    pieces = _split_bf16(jnp.where(lane < EXPERTS_PER_GROUP, meta, 0.0), 3)
    packed = pieces[0].astype(F32)
    for k in (1, 2):
        packed = packed + pltpu.roll(pieces[k].astype(F32), k * PIECE_LANES, 1)
    ms = dot(perm, packed.astype(BF16))
    ms_s[...] = ms + pltpu.roll(ms, LANES - PIECE_LANES, 1) + pltpu.roll(ms, LANES - 2 * PIECE_LANES, 1)
    acc_s[...] = jnp.zeros(acc_s.shape, F32)

    gw = EXPERTS_PER_GROUP * D_FF
    row = lax.broadcasted_iota(jnp.int32, (EXPERT_ROWS, 1), 0)
    for g in range(N_GROUPS):
        cols = slice(g * gw, (g + 1) * gw)

        def block(k, carry, g=g, cols=cols):
            rows = pl.ds(pl.multiple_of(starts[g] + k * EXPERT_ROWS, SORT_PAD), EXPERT_ROWS)
            hb = hs_s[rows, :]
            mb = jnp.where(row + k * EXPERT_ROWS < counts[g], ms_s[rows, :], 0.0)
            hid = _silu(dot(hb, w1_ref[:, cols])) * dot(hb, w3_ref[:, cols])
            parts = [(hid[:, j * D_FF:(j + 1) * D_FF] * _lane_col(mb, j)).astype(BF16)
                     for j in range(EXPERTS_PER_GROUP)]
            acc_s[rows, :] += dot(jnp.concatenate(parts, axis=1), w2_ref[cols, :])
            return carry

        lax.fori_loop(0, (counts[g] + (EXPERT_ROWS - 1)) // EXPERT_ROWS, block, 0)

    a_hi, a_lo = _split_bf16(acc_s[0:USED_ROWS, :], 2)
    moe = dot(unperm, a_hi) + dot(unperm, a_lo)
    out = x1_ref[...].reshape(nb, tt, D_MODEL) + gf_ref[...] * moe.reshape(nb, tt, D_MODEL)
    if final_norm:
        out = _rms(out, nf_ref[...])
    out_ref[...] = out.reshape(out_ref.shape)


def _moe(x, y, wo, mod_mix, mod_ffn, nw, wr, br, w1, w3, w2, nf, *, layer, dims, nb, tt,
         in_time_major, out_time_major):
    b, l = dims
    has_proj, final_norm = y is not None, nf is not None
    gb, gt = b // nb, l // tt
    r, n_tiles = nb * tt, gb * gt
    bm = pl.BlockSpec((nb, tt, D_MODEL), lambda i, j, *_: (i, j, 0))
    tm = pl.BlockSpec((tt, D_MODEL), lambda i, j, *_: (j, i))
    mod = lambda col: pl.BlockSpec((nb, 1, D_MODEL), lambda i, j, *_: (i, 0, col))
    flat = lambda w: pl.BlockSpec((r, w), lambda i, j, *_: (i * gt + j, 0))
    x_spec = tm if in_time_major else bm
    in_specs, args = [x_spec], [x]
    if has_proj:
        in_specs += [x_spec, _resident(wo), mod(2)]
        args += [y, wo, mod_mix]
    in_specs += [mod(0), mod(1)] + [_resident(a) for a in (nw, wr, br)]
    args += [mod_ffn, mod_ffn, nw, wr, br]
    x1, h, meta, cnt = pl.pallas_call(
        functools.partial(_moe_route_body, nb=nb, tt=tt, has_proj=has_proj),
        grid=(gb, gt),
        in_specs=in_specs,
        out_specs=[flat(D_MODEL), flat(D_MODEL), flat(LANES),
                   pl.BlockSpec((1, SUBLANES, LANES), lambda i, j: (i * gt + j, 0, 0))],
        out_shape=[jax.ShapeDtypeStruct((n_tiles * r, D_MODEL), F32),
                   jax.ShapeDtypeStruct((n_tiles * r, D_MODEL), BF16),
                   jax.ShapeDtypeStruct((n_tiles * r, LANES), F32),
                   jax.ShapeDtypeStruct((n_tiles, SUBLANES, LANES), jnp.int32)],
        compiler_params=_params("parallel", "parallel"),
        name="moe_route",
    )(*args)

    layer_w = lambda a: pl.BlockSpec((None,) + a.shape[1:], lambda *_: (layer, 0, 0), pipeline_mode=pl.Buffered(1))
    in_specs = [flat(D_MODEL), flat(D_MODEL), flat(LANES), mod(2)] + [layer_w(a) for a in (w1, w3, w2)]
    args = [x1, h, meta, mod_ffn, w1, w3, w2]
    if final_norm:
        in_specs.append(_resident(nf))
        args.append(nf)
    if out_time_major:
        assert nb == 1
        out_spec, out_shape = tm, jax.ShapeDtypeStruct((l, b * D_MODEL), F32)
    else:
        out_spec, out_shape = bm, jax.ShapeDtypeStruct((b, l, D_MODEL), F32)
    return pl.pallas_call(
        functools.partial(_moe_expert_body, nb=nb, tt=tt, n_time_tiles=gt, final_norm=final_norm),
        grid_spec=pltpu.PrefetchScalarGridSpec(
            num_scalar_prefetch=1,
            grid=(gb, gt),
            in_specs=in_specs,
            out_specs=out_spec,
            scratch_shapes=[pltpu.VMEM((SORT_ROWS, D_MODEL), BF16), pltpu.VMEM((SORT_ROWS, LANES), F32),
                            pltpu.VMEM((SORT_ROWS, D_MODEL), F32)]),
        out_shape=out_shape,
        compiler_params=_params("parallel", "parallel"),
        name="moe_experts",
    )(cnt[:, 0, :N_GROUPS], *args)


def _s5_body(*refs, tt, nbs, has_state):
    if has_state:
        (x_ref, sh_ref, sc_ref, gm_ref, nw_ref, win_ref, wbr_ref, wbi_ref, wcr_ref, wci_ref, dsk_ref,
         wglu_ref, abr_ref, abi_ref, s0r_ref, s0i_ref, out_ref, sr_ref, si_ref, xr_s, xi_s) = refs
    else:
        (x_ref, sh_ref, sc_ref, gm_ref, nw_ref, win_ref, wbr_ref, wbi_ref, wcr_ref, wci_ref, dsk_ref,
         wglu_ref, abr_ref, abi_ref, out_ref, sr_ref, si_ref, xr_s, xi_s) = refs
    r = tt * nbs
    blocks = D_MODEL // MXU_DIM
    sw = SSM_STATE // blocks

    @pl.when(pl.program_id(1) == 0)
    def _():
        if has_state:
            sr_ref[...] = s0r_ref[...]
            si_ref[...] = s0i_ref[...]
        else:
            sr_ref[...] = jnp.zeros(sr_ref.shape, F32)
            si_ref[...] = jnp.zeros(si_ref.shape, F32)

    x = x_ref[...]
    h = (_rms(x, nw_ref[...]) * (1.0 + sc_ref[...]) + sh_ref[...]).reshape(r, D_MODEL)
    u = _mm(h, win_ref[...])
    ub = u.astype(BF16)
    for j in range(blocks):
        uj = ub[:, j * MXU_DIM:(j + 1) * MXU_DIM]
        xr_s[:, :, j * sw:(j + 1) * sw] = jnp.dot(
            uj, wbr_ref[j], preferred_element_type=F32).reshape(tt, nbs, sw)
        xi_s[:, :, j * sw:(j + 1) * sw] = jnp.dot(
            uj, wbi_ref[j], preferred_element_type=F32).reshape(tt, nbs, sw)

    for rg in range(nbs // SUBLANES):
        rows = slice(rg * SUBLANES, (rg + 1) * SUBLANES)
        for ch in range(SSM_STATE // SCAN_LANES):
            lanes = slice(ch * SCAN_LANES, (ch + 1) * SCAN_LANES)
            ar = jnp.broadcast_to(abr_ref[:, lanes], (SUBLANES, SCAN_LANES))
            ai = jnp.broadcast_to(abi_ref[:, lanes], (SUBLANES, SCAN_LANES))

            hr, hi = sr_ref[rows, lanes], si_ref[rows, lanes]
            for t in range(tt):
                hr, hi = (ar * hr - ai * hi + xr_s[t, rows, lanes],
                          ar * hi + ai * hr + xi_s[t, rows, lanes])
                xr_s[t, rows, lanes] = hr
                xi_s[t, rows, lanes] = hi
            sr_ref[rows, lanes] = hr
            si_ref[rows, lanes] = hi

    ys = []
    for j in range(blocks):
        st_r = xr_s[:, :, j * sw:(j + 1) * sw].reshape(r, sw)
        st_i = xi_s[:, :, j * sw:(j + 1) * sw].reshape(r, sw)
        ys.append(_mm(st_r, wcr_ref[j]) - _mm(st_i, wci_ref[j]))
    y = jnp.concatenate(ys, axis=1) + dsk_ref[...] * u
    ag = _mm(jax.nn.gelu(y), wglu_ref[...])
    mix = ag[:, :D_MODEL] * _sigmoid(ag[:, D_MODEL:])
    out_ref[...] = x + gm_ref[...] * mix.reshape(tt, nbs, D_MODEL)


def _s5(x, mod, nw, win, wbr, wbi, wcr, wci, dsk, wglu, abr, abi, s0r, s0i, *, tt, nbs):
    l, b, _ = x.shape
    has_state = s0r is not None
    xs = pl.BlockSpec((tt, nbs, D_MODEL), lambda i, j: (j, i, 0))
    mods = lambda col: pl.BlockSpec((nbs, D_MODEL), lambda i, j: (i, col))
    full = _resident
    st = pl.BlockSpec((nbs, SSM_STATE), lambda i, j: (i, 0))
    consts = [nw, win, wbr, wbi, wcr, wci, dsk, wglu, abr, abi]
    in_specs = [xs, mods(0), mods(1), mods(2)] + [full(a) for a in consts]
    args = [x, mod, mod, mod] + consts
    if has_state:
        in_specs += [st, st]
        args += [s0r, s0i]
    return pl.pallas_call(
        functools.partial(_s5_body, tt=tt, nbs=nbs, has_state=has_state),
        grid=(b // nbs, l // tt),
        in_specs=in_specs,
        out_specs=[xs, st, st],
        out_shape=[jax.ShapeDtypeStruct((l, b, D_MODEL), F32),
                   jax.ShapeDtypeStruct((b, SSM_STATE), F32),
                   jax.ShapeDtypeStruct((b, SSM_STATE), F32)],
        scratch_shapes=[pltpu.VMEM((tt, nbs, SSM_STATE), F32), pltpu.VMEM((tt, nbs, SSM_STATE), F32)],
        compiler_params=_params("parallel", "arbitrary"),
        name="s5",
    )(*args)


def _s5_discretize(lam_re, lam_im, log_dt, b_re, b_im, c_re, c_im):
    dt = jnp.exp(log_dt)[:, None]
    mag = jnp.exp(lam_re * dt)
    ang = lam_im * dt
    ab_re, ab_im = mag * jnp.cos(ang), mag * jnp.sin(ang)
    den = lam_re * lam_re + lam_im * lam_im
    f_re = ((ab_re - 1.0) * lam_re + ab_im * lam_im) / den
    f_im = (ab_im * lam_re - (ab_re - 1.0) * lam_im) / den
    bb_re = f_re[..., None] * b_re - f_im[..., None] * b_im
    bb_im = f_re[..., None] * b_im + f_im[..., None] * b_re
    blocks = D_MODEL // MXU_DIM
    gpb = SSM_GROUPS // blocks

    def diag_blocks(t, rows_per_group, cols_per_group):
        tiled = jnp.tile(t.reshape(blocks, gpb * rows_per_group, cols_per_group), (1, 1, gpb))
        rg = lax.broadcasted_iota(jnp.int32, tiled.shape, 1) // rows_per_group
        cg = lax.broadcasted_iota(jnp.int32, tiled.shape, 2) // cols_per_group
        return jnp.where(rg == cg, tiled, 0.0).astype(BF16)

    b_blocks = lambda bb: diag_blocks(jnp.swapaxes(bb, 1, 2), SSM_GROUP, SSM_P)
    c_blocks = lambda cc: diag_blocks(jnp.swapaxes(cc, 1, 2), SSM_P, SSM_GROUP)

    return (ab_re.reshape(1, SSM_STATE), ab_im.reshape(1, SSM_STATE),
            b_blocks(bb_re), b_blocks(bb_im), c_blocks(c_re), c_blocks(c_im))


def _trunk(x, mods_mix, mods_ffn, conv0, delta0, re0, im0, wts, *, nb_tok, tt_tok, nb_moe, tt_moe, nb_delta, tt_delta, chunk,
           tt_s5, nbs_s5):
    b, l, _ = x.shape
    mm3 = [m.reshape(b, 1, 3 * D_MODEL) for m in mods_mix]
    mf3 = [m.reshape(b, 1, 3 * D_MODEL) for m in mods_ffn]

    front = (x, mm3[0], mm3[0], wts['norm_mix'][0], wts['gdn_wq'], wts['gdn_wba'], wts['gdn_alog'], wts['gdn_dtb'])
    fused = nb_delta * tt_delta >= ROWS
    if not fused:
        front = _gdn_front(*front, nb=nb_tok, tt=tt_tok, chunk=chunk)
    o, s_delta, s_conv = _gdn_delta(front, wts['gdn_cw'], wts['gdn_onorm'], delta0, conv0, dims=(b, l),
                                    nb=nb_delta, tt=tt_delta, chunk=chunk, fused=fused)
    prompt_like = nb_tok == 1
    moe_w = lambda i: (wts['norm_ffn'][i], wts['moe_wr'][i], wts['moe_br'][i],
                       wts['moe_w1'], wts['moe_w3'], wts['moe_w2'])
    x1 = _moe(x, o, wts['gdn_wout'], mm3[0], mf3[0], *moe_w(0), None, layer=0, dims=(b, l), nb=nb_moe, tt=tt_moe,
              in_time_major=False, out_time_major=prompt_like)
    xt = x1.reshape(l, b, D_MODEL) if prompt_like else jnp.swapaxes(x1, 0, 1)
    x2, s_re, s_im = _s5(xt, mods_mix[1], wts['norm_mix'][1], wts['s5_win'], wts['s5_wbr'], wts['s5_wbi'],
                         wts['s5_wcr'], wts['s5_wci'], wts['s5_d'], wts['s5_wglu'], wts['s5_abr'], wts['s5_abi'],
                         re0, im0, tt=tt_s5, nbs=nbs_s5)
    x2 = x2.reshape(l, b * D_MODEL) if prompt_like else jnp.swapaxes(x2, 0, 1)
    y = _moe(x2, None, None, None, mf3[1], *moe_w(1), wts['norm_final'], layer=1, dims=(b, l), nb=nb_moe, tt=tt_moe,
             in_time_major=prompt_like, out_time_major=False)
    return (y, s_delta[None], s_conv[None],
            s_re.reshape(1, b, SSM_GROUPS, SSM_P), s_im.reshape(1, b, SSM_GROUPS, SSM_P))


def kernel(x_prompt, x_sample, state_delta, state_conv, state_ssm_re, state_ssm_im, c_prompt, c_sample, norm_mix, w_mod_mix, b_mod_mix, norm_ffn, w_mod_ffn, b_mod_ffn, norm_final, gdn_w_in, gdn_conv_w, gdn_a_log, gdn_dt_bias, gdn_o_norm, gdn_w_out, s5_w_in, s5_lam_re, s5_lam_im, s5_log_dt, s5_b_re, s5_b_im, s5_c_re, s5_c_im, s5_d, s5_w_glu, moe_w_rg, moe_b_rg, moe_w_re, moe_b_re, moe_w1, moe_w3, moe_w2):
    bp = x_prompt.shape[0]
    depth = norm_mix.shape[0]

    lane_pad = lambda a, lo: jnp.pad(a, [(0, 0)] * (a.ndim - 1) + [(lo, LANES - lo - a.shape[-1])])
    abr, abi, wbr, wbi, wcr, wci = _s5_discretize(s5_lam_re[0], s5_lam_im[0], s5_log_dt[0], s5_b_re[0],
                                                  s5_b_im[0], s5_c_re[0], s5_c_im[0])
    wts = dict(
        norm_mix=norm_mix.reshape(depth, 1, D_MODEL), norm_ffn=norm_ffn.reshape(depth, 1, D_MODEL),
        norm_final=norm_final.reshape(1, D_MODEL),
        gdn_wq=gdn_w_in[0, :, :CONV_DIM + QK_DIM].astype(BF16),
        gdn_wba=lane_pad(gdn_w_in[0, :, CONV_DIM + QK_DIM:], 0).astype(BF16),
        gdn_alog=lane_pad(gdn_a_log[0][None], N_HEADS), gdn_dtb=lane_pad(gdn_dt_bias[0][None], N_HEADS),
        gdn_cw=jnp.pad(gdn_conv_w[0], ((0, SUBLANES - CONV_TAPS), (0, 0))),
        gdn_onorm=gdn_o_norm[0][None], gdn_wout=gdn_w_out[0].astype(BF16),
        s5_win=s5_w_in[0].astype(BF16), s5_wbr=wbr, s5_wbi=wbi, s5_wcr=wcr, s5_wci=wci,
        s5_d=s5_d[0].reshape(1, D_MODEL), s5_wglu=s5_w_glu[0].astype(BF16), s5_abr=abr, s5_abi=abi,
        moe_wr=lane_pad(jnp.concatenate([moe_w_rg, moe_w_re], axis=-1), 0),
        moe_br=lane_pad(jnp.concatenate([moe_b_rg, moe_b_re], axis=-1), 0)[:, None, :],
        moe_w1=jnp.swapaxes(moe_w1, 1, 2).reshape(depth, D_MODEL, N_EXPERTS * D_FF).astype(BF16),
        moe_w3=jnp.swapaxes(moe_w3, 1, 2).reshape(depth, D_MODEL, N_EXPERTS * D_FF).astype(BF16),
        moe_w2=moe_w2.reshape(depth, N_EXPERTS * D_FF, D_MODEL).astype(BF16),
    )

    c_all = jnp.concatenate([c_prompt, c_sample], axis=0)
    m_mix = _ada_mod(c_all, w_mod_mix, b_mod_mix)
    m_ffn = _ada_mod(c_all, w_mod_ffn, b_mod_ffn)

    y_p, p_delta, p_conv, p_re, p_im = _trunk(
        x_prompt, [m_mix[i, :bp] for i in range(depth)], [m_ffn[i, :bp] for i in range(depth)],
        None, None, None, None, wts,
        nb_tok=1, tt_tok=ROWS, nb_moe=1, tt_moe=MOE_ROWS, nb_delta=1, tt_delta=ROWS, chunk=CHUNK, tt_s5=ROWS // SUBLANES, nbs_s5=SUBLANES)
    ls = x_sample.shape[1]
    y_s, s_delta, s_conv, s_re, s_im = _trunk(
        x_sample, [m_mix[i, bp:] for i in range(depth)], [m_ffn[i, bp:] for i in range(depth)],
        state_conv[0], state_delta[0], state_ssm_re[0].reshape(-1, SSM_STATE),
        state_ssm_im[0].reshape(-1, SSM_STATE), wts,
        nb_tok=ROWS // ls, tt_tok=ls, nb_moe=MOE_ROWS // ls, tt_moe=ls, nb_delta=MXU_DIM // (ls * N_HEADS), tt_delta=ls, chunk=ls,
        tt_s5=ls, nbs_s5=ROWS // ls)
    return (y_p, y_s, p_delta, p_conv, p_re, p_im, s_delta, s_conv, s_re, s_im)
```

```python
import functools

import jax
import jax.numpy as jnp
from jax import lax
from jax.experimental import pallas as pl
from jax.experimental.pallas import tpu as pltpu

F32 = jnp.float32
BF16 = jnp.bfloat16

D_MODEL = 1024
N_HEADS = 8
HEAD_DIM = 128
QK_DIM = N_HEADS * HEAD_DIM
CONV_DIM = 3 * QK_DIM
CONV_TAPS = 4
CHUNK = 64
SSM_GROUPS = 64
SSM_GROUP = 16
SSM_P = 64
SSM_STATE = SSM_GROUPS * SSM_P
N_GROUPS = 4
EXPERTS_PER_GROUP = 4
N_EXPERTS = 16
D_FF = 256
EPS = 1e-6

LANES = 128
SUBLANES = 8
MXU_DIM = 256
ROWS = 256
MOE_ROWS = 512
S5_ROWS = 512
SCAN_LANES = 512
VMEM_LIMIT = 56 * 1024 * 1024


def _mm(a, b):
    return jnp.dot(a.astype(BF16), b.astype(BF16), preferred_element_type=F32)


def _bmm(a, b):
    return lax.dot_general(a.astype(BF16), b.astype(BF16), (((2,), (1,)), ((0,), (0,))),
                           preferred_element_type=F32)


def _bmm_nt(a, b):
    return lax.dot_general(a.astype(BF16), b.astype(BF16), (((2,), (2,)), ((0,), (0,))),
                           preferred_element_type=F32)


def _split_bf16(x, terms):
    out = []
    for _ in range(terms - 1):
        p = x.astype(BF16)
        out.append(p)
        x = x - p.astype(F32)
    out.append(x.astype(BF16))
    return out


def _rms(x, w):
    return x * lax.rsqrt(jnp.mean(x * x, axis=-1, keepdims=True) + EPS) * w


def _sigmoid(x):
    return 0.5 * jnp.tanh(0.5 * x) + 0.5


def _silu(x):
    h = 0.5 * x
    return h * jnp.tanh(h) + h


def _softplus(x):
    return jnp.maximum(x, 0.0) + jnp.log1p(jnp.exp(-jnp.abs(x)))


def _lane_col(tile, lane):
    li = lax.broadcasted_iota(jnp.int32, tile.shape, 1)
    return jnp.sum(jnp.where(li == lane, tile, 0.0), axis=1, keepdims=True)


def _log2(n):
    assert n & (n - 1) == 0
    return n.bit_length() - 1


def _params(*sem):
    return pltpu.CompilerParams(dimension_semantics=sem, vmem_limit_bytes=VMEM_LIMIT)


def _resident(a):
    return pl.BlockSpec(a.shape, lambda *_: (0,) * a.ndim, pipeline_mode=pl.Buffered(1))


def _mod_body(c_ref, w_ref, b_ref, o_ref):
    o_ref[0] = _mm(_silu(c_ref[...]), w_ref[0]) + b_ref[0]


def _ada_mod(c_all, w_mod, b_mod):
    nl, nr, tn = w_mod.shape[0], c_all.shape[0], 768
    return pl.pallas_call(
        _mod_body,
        grid=(nl, 3 * D_MODEL // tn),
        in_specs=[pl.BlockSpec((nr, D_MODEL), lambda l, j: (0, 0)),
                  pl.BlockSpec((1, D_MODEL, tn), lambda l, j: (l, 0, j)),
                  pl.BlockSpec((1, 1, tn), lambda l, j: (l, 0, j))],
        out_specs=pl.BlockSpec((1, nr, tn), lambda l, j: (l, 0, j)),
        out_shape=jax.ShapeDtypeStruct((nl, nr, 3 * D_MODEL), F32),
        compiler_params=_params("parallel", "parallel"),
        name="ada_mod",
    )(c_all, w_mod, b_mod.reshape(nl, 1, 3 * D_MODEL))


def _gdn_front_math(x_ref, sh_ref, sc_ref, nw_ref, wq_ref, wba_ref, alog_ref, dtb_ref, chunk):
    x = x_ref[...]
    nb, tt, _ = x.shape
    r = nb * tt
    h = _rms(x, nw_ref[...]) * (1.0 + sc_ref[...]) + sh_ref[...]
    hb = h.reshape(r, D_MODEL).astype(BF16)
    pq = jnp.dot(hb, wq_ref[...], preferred_element_type=F32)
    ba = jnp.dot(hb, wba_ref[...], preferred_element_type=F32)
    beta = _sigmoid(ba)
    g = -jnp.exp(alog_ref[...]) * _softplus(ba + dtb_ref[...])
    ri = lax.broadcasted_iota(jnp.int32, (r, r), 0)
    ci = lax.broadcasted_iota(jnp.int32, (r, r), 1)
    same = (ri >> _log2(chunk)) == (ci >> _log2(chunk))
    low = jnp.where(same & (ri >= ci), 1.0, 0.0).astype(BF16)
    ones = jnp.where(same, 1.0, 0.0).astype(BF16)
    gc = jnp.zeros((r, LANES), F32)
    gl = jnp.zeros((r, LANES), F32)
    for piece in _split_bf16(g, 3):
        gc = gc + jnp.dot(low, piece, preferred_element_type=F32)
        gl = gl + jnp.dot(ones, piece, preferred_element_type=F32)
    return pq, jnp.concatenate([beta, gc, gl], axis=-1)


def _gdn_front_body(x_ref, sh_ref, sc_ref, nw_ref, wq_ref, wba_ref, alog_ref, dtb_ref,
                    qkv_ref, z_ref, gt_ref, *, chunk):
    nb, tt, _ = x_ref.shape
    pq, gates = _gdn_front_math(x_ref, sh_ref, sc_ref, nw_ref, wq_ref, wba_ref, alog_ref, dtb_ref, chunk)
    qkv_ref[...] = pq[:, :CONV_DIM].reshape(nb, tt, CONV_DIM)
    z_ref[...] = pq[:, CONV_DIM:].reshape(nb, tt, QK_DIM)
    gt_ref[...] = gates.reshape(nb, tt, 3 * LANES)


def _gdn_front(x, shift, scale, nw, wq, wba, alog, dtb, *, nb, tt, chunk):
    b, l, _ = x.shape
    tok = lambda w: pl.BlockSpec((nb, tt, w), lambda i, j: (i, j, 0))
    mod = lambda col: pl.BlockSpec((nb, 1, D_MODEL), lambda i, j: (i, 0, col))
    full = _resident
    return pl.pallas_call(
        functools.partial(_gdn_front_body, chunk=chunk),
        grid=(b // nb, l // tt),
        in_specs=[tok(D_MODEL), mod(0), mod(1), full(nw), full(wq), full(wba), full(alog), full(dtb)],
        out_specs=[tok(CONV_DIM), tok(QK_DIM), tok(3 * LANES)],
        out_shape=[jax.ShapeDtypeStruct((b, l, CONV_DIM), F32),
                   jax.ShapeDtypeStruct((b, l, QK_DIM), F32),
                   jax.ShapeDtypeStruct((b, l, 3 * LANES), F32)],
        compiler_params=_params("parallel", "parallel"),
        name="gdn_front",
    )(x, shift, scale, nw, wq, wba, alog, dtb)


def _inv_unit_lower(a, ri, ci, chunk):
    base = min(16, chunk)
    same = lambda size: (ri >> _log2(size)) == (ci >> _log2(size))
    n = jnp.where(same(base), -a, 0.0)
    q, m, k = n, n, 1
    while 2 * k < base:
        m = _bmm(m, m)
        q = q + m + _bmm(q, m)
        k *= 2
    size = base
    while size < chunk:
        e = jnp.where(same(2 * size) & jnp.logical_not(same(size)), a, 0.0)
        x = e + _bmm(q, e)
        q = q - (x + _bmm(x, q))
        size *= 2
    return q


def _gdn_delta_body(*refs, nb, tt, chunk, has_state, fused):
    it = iter(refs)
    take = lambda n: [next(it) for _ in range(n)]
    if fused:
        front_refs = take(8)
    else:
        qkv_ref, z_ref, gt_ref = take(3)
    cw_ref, on_ref = take(2)
    if has_state:
        s0_ref, c0_ref = take(2)
    o_ref, s_ref, cn_ref, xc, qkv_s, o_s, u_s, wq_s, qk_s, kdt_s = take(10)
    if fused:
        z_ref, gt_ref = take(2)
        pq, gates = _gdn_front_math(*front_refs, chunk)
        z_ref[...] = pq[:, CONV_DIM:].reshape(nb, tt, QK_DIM)
        gt_ref[...] = gates.reshape(nb, tt, 3 * LANES)
        qkv_in = pq[:, :CONV_DIM].reshape(nb, tt, CONV_DIM)
    else:
        qkv_in = qkv_ref[...]
    hist = CONV_TAPS - 1
    pad = SUBLANES

    @pl.when(pl.program_id(1) == 0)
    def _():
        if has_state:
            s_ref[...] = s0_ref[...]
            xc[:, pad - hist:pad, :] = c0_ref[...]
        else:
            s_ref[...] = jnp.zeros(s_ref.shape, F32)
            xc[:, pad - hist:pad, :] = jnp.zeros((nb, hist, CONV_DIM), F32)

    xc[:, pad:pad + tt, :] = qkv_in
    cw = cw_ref[...]
    y = xc[:, pad - hist:pad - hist + tt, :] * cw[0:1]
    for j in range(1, CONV_TAPS):
        y = y + xc[:, pad - hist + j:pad - hist + j + tt, :] * cw[j:j + 1]
    qkv_s[...] = _silu(y)
    last = xc[:, pad + tt - hist:pad + tt, :]
    cn_ref[...] = last
    xc[:, pad - hist:pad, :] = last

    for hd in range(N_HEADS):
        for part, scl in ((0, HEAD_DIM ** -0.5), (1, 1.0)):
            lo = part * QK_DIM + hd * HEAD_DIM
            xh = qkv_s[:, :, lo:lo + HEAD_DIM]
            xh = xh * (lax.rsqrt(jnp.sum(xh * xh, axis=-1, keepdims=True) + EPS) * scl)
            qkv_s[:, :, lo:lo + HEAD_DIM] = xh

    per_stack = MXU_DIM // chunk
    units_all = [(s, hd) for s in range(nb) for hd in range(N_HEADS)]
    assert len(units_all) % per_stack == 0
    stacks = [units_all[i:i + per_stack] for i in range(0, len(units_all), per_stack)]
    r = MXU_DIM
    ri = lax.broadcasted_iota(jnp.int32, (r, r), 0)
    ci = lax.broadcasted_iota(jnp.int32, (r, r), 1)
    same = (ri >> _log2(chunk)) == (ci >> _log2(chunk))
    tril = same & (ri >= ci)
    strict = same & (ri > ci)
    eye = ri == ci

    n_chunks = tt // chunk
    dot = lambda p, q: jnp.dot(p, q, preferred_element_type=F32)

    def gather(f):
        return jnp.stack([jnp.concatenate([f(s, hd, slice(c * chunk, (c + 1) * chunk)) for s, hd in units], axis=0)
                          for c in range(n_chunks) for units in stacks], axis=0)

    qn = gather(lambda s, hd, rows: qkv_s[s, rows, hd * HEAD_DIM:(hd + 1) * HEAD_DIM])
    kn = gather(lambda s, hd, rows: qkv_s[s, rows, QK_DIM + hd * HEAD_DIM:QK_DIM + (hd + 1) * HEAD_DIM])
    v = gather(lambda s, hd, rows: qkv_s[s, rows, 2 * QK_DIM + hd * HEAD_DIM:2 * QK_DIM + (hd + 1) * HEAD_DIM])
    beta = gather(lambda s, hd, rows: _lane_col(gt_ref[s, rows, 0:LANES], hd))
    gc = gather(lambda s, hd, rows: _lane_col(gt_ref[s, rows, LANES:2 * LANES], N_HEADS + hd))
    gl = gather(lambda s, hd, rows: _lane_col(gt_ref[s, rows, 2 * LANES:3 * LANES], N_HEADS + hd))
    ns = qn.shape[0]
    eg = jnp.exp(gc)
    kb = kn * beta
    gc_row = jnp.sum(jnp.where(eye, gc, 0.0), axis=1, keepdims=True)
    decay = jnp.where(tril, jnp.exp(jnp.where(tril, gc - gc_row, 0.0)), 0.0)
    a = jnp.where(strict, _bmm_nt(kb, kn) * decay, 0.0)
    q_inv = _inv_unit_lower(a, ri, ci, chunk)
    rhs = jnp.concatenate([v * beta, kb * eg], axis=2)
    uw = rhs + _bmm(q_inv, rhs)
    u_s[...] = uw[:, :, :HEAD_DIM]
    per_unit = lambda t: t.reshape(ns, per_stack, chunk, HEAD_DIM)
    wq_s[...] = jnp.concatenate([per_unit(uw[:, :, HEAD_DIM:]), per_unit(qn * eg)], axis=2).reshape(
        ns, 2 * r, HEAD_DIM).astype(BF16)
    qk_s[...] = jnp.where(tril, _bmm_nt(qn, kn) * decay, 0.0).astype(BF16)
    kdt_s[...] = jnp.swapaxes(kn * jnp.exp(gl - gc), 1, 2).astype(BF16)

    unit_of_row = lax.broadcasted_iota(jnp.int32, (r, HEAD_DIM), 0) >> _log2(chunk)
    for c in range(n_chunks):
        rows = slice(c * chunk, (c + 1) * chunk)
        for k, units in enumerate(stacks):
            idx = c * len(stacks) + k
            ws, qs = [], []
            for i, (s, hd) in enumerate(units):
                both = dot(wq_s[idx, 2 * i * chunk:2 * (i + 1) * chunk, :], s_ref[s, hd].astype(BF16))
                ws.append(both[:chunk])
                qs.append(both[chunk:])
            v_new = (u_s[idx] - jnp.concatenate(ws, axis=0)).astype(BF16)
            o = jnp.concatenate(qs, axis=0) + dot(qk_s[idx], v_new)
            kdt = kdt_s[idx]
            for i, (s, hd) in enumerate(units):
                o_s[s, rows, hd * HEAD_DIM:(hd + 1) * HEAD_DIM] = o[i * chunk:(i + 1) * chunk]
                g_last = jnp.exp(_lane_col(gt_ref[s, c * chunk:c * chunk + 1, 2 * LANES:3 * LANES], N_HEADS + hd))
                upd = dot(kdt, jnp.where(unit_of_row == i, v_new, jnp.zeros_like(v_new)))
                s_ref[s, hd] = s_ref[s, hd] * g_last + upd

    on = on_ref[...]
    for hd in range(N_HEADS):
        sl = slice(hd * HEAD_DIM, (hd + 1) * HEAD_DIM)
        o_ref[:, :, sl] = _rms(o_s[:, :, sl], on) * _silu(z_ref[:, :, sl])


def _gdn_delta(front, cw, onorm, s0, c0, *, dims, nb, tt, chunk, fused):
    b, l = dims
    has_state = s0 is not None
    n_stacks = nb * tt * N_HEADS // MXU_DIM
    tok = lambda w: pl.BlockSpec((nb, tt, w), lambda i, j: (i, j, 0))
    mod = lambda col: pl.BlockSpec((nb, 1, D_MODEL), lambda i, j: (i, 0, col))
    full = _resident
    st_spec = pl.BlockSpec((nb, N_HEADS, HEAD_DIM, HEAD_DIM), lambda i, j: (i, 0, 0, 0))
    cv_spec = pl.BlockSpec((nb, CONV_TAPS - 1, CONV_DIM), lambda i, j: (i, 0, 0))
    if fused:
        in_specs = [tok(D_MODEL), mod(0), mod(1)] + [full(a) for a in front[3:]]
    else:
        in_specs = [tok(CONV_DIM), tok(QK_DIM), tok(3 * LANES)]
    in_specs += [full(cw), full(onorm)]
    args = list(front) + [cw, onorm]
    if has_state:
        in_specs += [st_spec, cv_spec]
        args += [s0, c0]
    scratch = [pltpu.VMEM((nb, tt + SUBLANES, CONV_DIM), F32),
               pltpu.VMEM((nb, tt, CONV_DIM), F32),
               pltpu.VMEM((nb, tt, QK_DIM), F32),
               pltpu.VMEM((n_stacks, MXU_DIM, HEAD_DIM), F32),
               pltpu.VMEM((n_stacks, 2 * MXU_DIM, HEAD_DIM), BF16),
               pltpu.VMEM((n_stacks, MXU_DIM, MXU_DIM), BF16),
               pltpu.VMEM((n_stacks, HEAD_DIM, MXU_DIM), BF16)]
    if fused:
        scratch += [pltpu.VMEM((nb, tt, QK_DIM), F32), pltpu.VMEM((nb, tt, 3 * LANES), F32)]
    return pl.pallas_call(
        functools.partial(_gdn_delta_body, nb=nb, tt=tt, chunk=chunk, has_state=has_state, fused=fused),
        grid=(b // nb, l // tt),
        in_specs=in_specs,
        out_specs=[tok(QK_DIM), st_spec, cv_spec],
        out_shape=[jax.ShapeDtypeStruct((b, l, QK_DIM), F32),
                   jax.ShapeDtypeStruct((b, N_HEADS, HEAD_DIM, HEAD_DIM), F32),
                   jax.ShapeDtypeStruct((b, CONV_TAPS - 1, CONV_DIM), F32)],
        scratch_shapes=scratch,
        compiler_params=_params("parallel", "arbitrary"),
        name="gdn_delta",
    )(*args)


def _route(logits):
    lane = lax.broadcasted_iota(jnp.int32, logits.shape, 1)
    neg, big = -1e30, 1 << 20
    rmax = lambda t: jnp.max(t, axis=1, keepdims=True)
    rsum = lambda t: jnp.sum(t, axis=1, keepdims=True)
    first = lambda cond: jnp.min(jnp.where(cond, lane, big), axis=1, keepdims=True)
    is_g = lane < N_GROUPS
    lg = jnp.where(is_g, logits, neg)
    mg = rmax(lg)
    g_idx = first(is_g & (lg >= mg))
    p_group = 1.0 / rsum(jnp.where(is_g, jnp.exp(lg - mg), 0.0))
    lo = N_GROUPS + EXPERTS_PER_GROUP * g_idx
    in_g = (lane >= lo) & (lane < lo + EXPERTS_PER_GROUP)
    le = jnp.where(in_g, logits, neg)
    m1 = rmax(le)
    se = rsum(jnp.where(in_g, jnp.exp(le - m1), 0.0))
    i1 = first(in_g & (le >= m1))
    le2 = jnp.where(lane == i1, neg, le)
    m2 = rmax(le2)
    i2 = first(in_g & (lane != i1) & (le2 >= m2))
    p1 = 1.0 / se
    p2 = jnp.exp(m2 - m1) / se
    tot = p1 + p2
    local = (jnp.where(lane == i1 - lo, p1 / tot * p_group, 0.0)
             + jnp.where(lane == i2 - lo, p2 / tot * p_group, 0.0))
    return g_idx, local


META_GROUP, META_RANK = 4, 5
PIECE_LANES = 8
SORT_PAD = 16
EXPERT_ROWS = MOE_ROWS // N_GROUPS + 32
SORT_ROWS = MOE_ROWS + N_GROUPS * SORT_PAD + EXPERT_ROWS + 96
USED_ROWS = MOE_ROWS + LANES
MOE_HALVES = 2


def _moe_route_body(*refs, nb, tt, has_proj):
    it = iter(refs)
    x_ref = next(it)
    if has_proj:
        y_ref, wo_ref, gm_ref = next(it), next(it), next(it)
    sh_ref, sc_ref, nw_ref, wr_ref, br_ref, x1_ref, h_ref, meta_ref, cnt_ref = (next(it) for _ in range(9))
    r = nb * tt
    dot = lambda p, q: jnp.dot(p, q, preferred_element_type=F32)

    x = x_ref[...].reshape(nb, tt, D_MODEL)
    if has_proj:
        x = x + gm_ref[...] * _mm(y_ref[...].reshape(r, D_MODEL), wo_ref[...]).reshape(nb, tt, D_MODEL)
    h = (_rms(x, nw_ref[...]) * (1.0 + sc_ref[...]) + sh_ref[...]).reshape(r, D_MODEL)
    x1_ref[...] = x.reshape(r, D_MODEL)
    h_hi, h_lo = _split_bf16(h, 2)
    h_ref[...] = h_hi
    w_hi, w_lo = _split_bf16(wr_ref[...], 2)
    logits = dot(h_hi, w_hi) + (dot(h_hi, w_lo) + dot(h_lo, w_hi)) + br_ref[...]
    g_idx, local = _route(logits)

    lane = lax.broadcasted_iota(jnp.int32, (r, LANES), 1)
    onehot = jnp.where(lane == g_idx, 1.0, 0.0)
    ri = lax.broadcasted_iota(jnp.int32, (r, r), 0)
    ci = lax.broadcasted_iota(jnp.int32, (r, r), 1)
    earlier = jnp.where(ri > ci, 1.0, 0.0).astype(BF16)
    rank = jnp.sum(onehot * dot(earlier, onehot.astype(BF16)), axis=1, keepdims=True)
    meta_ref[...] = jnp.where(lane == META_GROUP, g_idx.astype(F32),
                              jnp.where(lane == META_RANK, rank, local))
    counts = jnp.sum(onehot, axis=0, keepdims=True).astype(jnp.int32)
    cnt_ref[...] = jnp.broadcast_to(counts, (1, SUBLANES, LANES))


def _moe_expert_body(cnt_ref, *refs, nb, tt, n_time_tiles, final_norm):
    it = iter(refs)
    x1_ref, h_ref, meta_ref, gf_ref, w1_ref, w3_ref, w2_ref = (next(it) for _ in range(7))
    if final_norm:
        nf_ref = next(it)
    out_ref, hs_s, ms_s, acc_s = (next(it) for _ in range(4))
    r = MOE_ROWS
    dot = lambda p, q: jnp.dot(p, q, preferred_element_type=F32)
    step = pl.program_id(0) * n_time_tiles + pl.program_id(1)
    grp = pl.program_id(2)

    def layout(half):
        counts = [cnt_ref[MOE_HALVES * step + half, g] for g in range(N_GROUPS)]
        starts, nxt = [], 0
        for g in range(N_GROUPS):
            starts.append(nxt)
            nxt = nxt + (((counts[g] + (SORT_PAD - 1)) >> _log2(SORT_PAD)) << _log2(SORT_PAD))
        return counts, starts

    def positions(half, starts):
        meta = meta_ref[half * r:(half + 1) * r, :]
        g_col = _lane_col(meta, META_GROUP)
        pos = _lane_col(meta, META_RANK)
        for g in range(N_GROUPS):
            pos = pos + jnp.where(g_col == g, jnp.asarray(starts[g], jnp.int32).astype(F32), 0.0)
        return meta, pos

    @pl.when(grp == 0)
    def _():
        for half in range(MOE_HALVES):
            meta, pos = positions(half, layout(half)[1])
            pos_row = jnp.broadcast_to(pos, (r, LANES)).T[0:1, :].astype(jnp.int32)
            perm = jnp.where(lax.broadcasted_iota(jnp.int32, (SORT_ROWS, r), 0) == pos_row, 1.0, 0.0).astype(BF16)
            hs_s[half] = dot(perm, h_ref[half * r:(half + 1) * r, :]).astype(BF16)
            lane = lax.broadcasted_iota(jnp.int32, (r, LANES), 1)
            pieces = _split_bf16(jnp.where(lane < EXPERTS_PER_GROUP, meta, 0.0), 3)
            packed = pieces[0].astype(F32)
            for k in (1, 2):
                packed = packed + pltpu.roll(pieces[k].astype(F32), k * PIECE_LANES, 1)
            ms = dot(perm, packed.astype(BF16))
            ms_s[half] = ms + pltpu.roll(ms, LANES - PIECE_LANES, 1) + pltpu.roll(ms, LANES - 2 * PIECE_LANES, 1)
        acc_s[...] = jnp.zeros(acc_s.shape, F32)

    pick = lambda vals: functools.reduce(lambda acc, gv: jnp.where(grp == gv[0], gv[1], acc),
                                         list(enumerate(vals))[1:], jnp.asarray(vals[0], jnp.int32))
    lay = [layout(half) for half in range(MOE_HALVES)]
    cnt_g = [pick(c) for c, _ in lay]
    start_g = [pick(s) for _, s in lay]
    n_blocks = functools.reduce(jnp.maximum, [(c + (EXPERT_ROWS - 1)) // EXPERT_ROWS for c in cnt_g])
    row = lax.broadcasted_iota(jnp.int32, (EXPERT_ROWS, 1), 0)

    def block(k, carry):
        rows = [pl.ds(pl.multiple_of(s + k * EXPERT_ROWS, SORT_PAD), EXPERT_ROWS) for s in start_g]
        hb = jnp.concatenate([hs_s[half, rows[half], :] for half in range(MOE_HALVES)], axis=0)
        mb = jnp.concatenate([jnp.where(row + k * EXPERT_ROWS < cnt_g[half], ms_s[half, rows[half], :], 0.0)
                              for half in range(MOE_HALVES)], axis=0)
        hid = _silu(dot(hb, w1_ref[...])) * dot(hb, w3_ref[...])
        parts = [(hid[:, j * D_FF:(j + 1) * D_FF] * _lane_col(mb, j)).astype(BF16)
                 for j in range(EXPERTS_PER_GROUP)]
        res = dot(jnp.concatenate(parts, axis=1), w2_ref[...])
        for half in range(MOE_HALVES):
            acc_s[half, rows[half], :] += res[half * EXPERT_ROWS:(half + 1) * EXPERT_ROWS]
        return carry

    lax.fori_loop(0, n_blocks, block, 0)

    @pl.when(grp == N_GROUPS - 1)
    def _():
        moes = []
        for half in range(MOE_HALVES):
            _, pos = positions(half, layout(half)[1])
            unperm = jnp.where(lax.broadcasted_iota(jnp.int32, (r, USED_ROWS), 1) == pos.astype(jnp.int32),
                               1.0, 0.0).astype(BF16)
            a_hi, a_lo = _split_bf16(acc_s[half, 0:USED_ROWS, :], 2)
            moes.append(dot(unperm, a_hi) + dot(unperm, a_lo))
        moe = jnp.concatenate(moes, axis=0)
        out = x1_ref[...].reshape(nb, tt, D_MODEL) + gf_ref[...] * moe.reshape(nb, tt, D_MODEL)
        if final_norm:
            out = _rms(out, nf_ref[...])
        out_ref[...] = out.reshape(out_ref.shape)


def _moe(x, y, wo, mod_mix, mod_ffn, nw, wr, br, w1, w3, w2, nf, *, layer, dims, nb, tt,
         in_time_major, out_time_major):
    b, l = dims
    has_proj, final_norm = y is not None, nf is not None
    gb, gt = b // nb, l // tt
    r, n_tiles = nb * tt, gb * gt
    bm = pl.BlockSpec((nb, tt, D_MODEL), lambda i, j, *_: (i, j, 0))
    tm = pl.BlockSpec((tt, D_MODEL), lambda i, j, *_: (j, i))
    mod = lambda col: pl.BlockSpec((nb, 1, D_MODEL), lambda i, j, *_: (i, 0, col))
    flat = lambda w: pl.BlockSpec((r, w), lambda i, j, *_: (i * gt + j, 0))
    x_spec = tm if in_time_major else bm
    in_specs, args = [x_spec], [x]
    if has_proj:
        in_specs += [x_spec, _resident(wo), mod(2)]
        args += [y, wo, mod_mix]
    in_specs += [mod(0), mod(1)] + [_resident(a) for a in (nw, wr, br)]
    args += [mod_ffn, mod_ffn, nw, wr, br]
    x1, h, meta, cnt = pl.pallas_call(
        functools.partial(_moe_route_body, nb=nb, tt=tt, has_proj=has_proj),
        grid=(gb, gt),
        in_specs=in_specs,
        out_specs=[flat(D_MODEL), flat(D_MODEL), flat(LANES),
                   pl.BlockSpec((1, SUBLANES, LANES), lambda i, j: (i * gt + j, 0, 0))],
        out_shape=[jax.ShapeDtypeStruct((n_tiles * r, D_MODEL), F32),
                   jax.ShapeDtypeStruct((n_tiles * r, D_MODEL), BF16),
                   jax.ShapeDtypeStruct((n_tiles * r, LANES), F32),
                   jax.ShapeDtypeStruct((n_tiles, SUBLANES, LANES), jnp.int32)],
        compiler_params=_params("parallel", "parallel"),
        name="moe_route",
    )(*args)

    nb2, tt2 = (nb, MOE_HALVES * tt) if nb == 1 else (MOE_HALVES * nb, tt)
    gb2, gt2 = b // nb2, l // tt2
    assert r == MOE_ROWS and gb2 * gt2 * MOE_HALVES == n_tiles
    gw = EXPERTS_PER_GROUP * D_FF
    bm2 = pl.BlockSpec((nb2, tt2, D_MODEL), lambda i, j, g, *_: (i, j, 0))
    tm2 = pl.BlockSpec((tt2, D_MODEL), lambda i, j, g, *_: (j, i))
    flat2 = lambda w: pl.BlockSpec((MOE_HALVES * r, w), lambda i, j, g, *_: (i * gt2 + j, 0))
    in_specs = [flat2(D_MODEL), flat2(D_MODEL), flat2(LANES),
                pl.BlockSpec((nb2, 1, D_MODEL), lambda i, j, g, *_: (i, 0, 2)),
                pl.BlockSpec((None, D_MODEL, gw), lambda i, j, g, *_: (layer, 0, g)),
                pl.BlockSpec((None, D_MODEL, gw), lambda i, j, g, *_: (layer, 0, g)),
                pl.BlockSpec((None, gw, D_MODEL), lambda i, j, g, *_: (layer, g, 0))]
    args = [x1, h, meta, mod_ffn, w1, w3, w2]
    if final_norm:
        in_specs.append(_resident(nf))
        args.append(nf)
    if out_time_major:
        assert nb == 1
        out_spec, out_shape = tm2, jax.ShapeDtypeStruct((l, b * D_MODEL), F32)
    else:
        out_spec, out_shape = bm2, jax.ShapeDtypeStruct((b, l, D_MODEL), F32)
    return pl.pallas_call(
        functools.partial(_moe_expert_body, nb=nb2, tt=tt2, n_time_tiles=gt2, final_norm=final_norm),
        grid_spec=pltpu.PrefetchScalarGridSpec(
            num_scalar_prefetch=1,
            grid=(gb2, gt2, N_GROUPS),
            in_specs=in_specs,
            out_specs=out_spec,
            scratch_shapes=[pltpu.VMEM((MOE_HALVES, SORT_ROWS, D_MODEL), BF16),
                            pltpu.VMEM((MOE_HALVES, SORT_ROWS, LANES), F32),
                            pltpu.VMEM((MOE_HALVES, SORT_ROWS, D_MODEL), F32)]),
        out_shape=out_shape,
        compiler_params=_params("parallel", "parallel", "arbitrary"),
        name="moe_experts",
    )(cnt[:, 0, :N_GROUPS], *args)


def _s5_body(*refs, tt, nbs, has_state):
    if has_state:
        (x_ref, sh_ref, sc_ref, gm_ref, nw_ref, win_ref, wbr_ref, wbi_ref, wcr_ref, wci_ref, dsk_ref,
         wglu_ref, abr_ref, abi_ref, s0r_ref, s0i_ref, out_ref, sr_ref, si_ref, xr_s, xi_s) = refs
    else:
        (x_ref, sh_ref, sc_ref, gm_ref, nw_ref, win_ref, wbr_ref, wbi_ref, wcr_ref, wci_ref, dsk_ref,
         wglu_ref, abr_ref, abi_ref, out_ref, sr_ref, si_ref, xr_s, xi_s) = refs
    r = tt * nbs
    blocks = D_MODEL // MXU_DIM
    sw = SSM_STATE // blocks

    @pl.when(pl.program_id(1) == 0)
    def _():
        if has_state:
            sr_ref[...] = s0r_ref[...]
            si_ref[...] = s0i_ref[...]
        else:
            sr_ref[...] = jnp.zeros(sr_ref.shape, F32)
            si_ref[...] = jnp.zeros(si_ref.shape, F32)

    x = x_ref[...]
    h = (_rms(x, nw_ref[...]) * (1.0 + sc_ref[...]) + sh_ref[...]).reshape(r, D_MODEL)
    u = _mm(h, win_ref[...])
    ub = u.astype(BF16)
    for j in range(blocks):
        uj = ub[:, j * MXU_DIM:(j + 1) * MXU_DIM]
        xr_s[:, :, j * sw:(j + 1) * sw] = jnp.dot(
            uj, wbr_ref[j], preferred_element_type=F32).reshape(tt, nbs, sw)
        xi_s[:, :, j * sw:(j + 1) * sw] = jnp.dot(
            uj, wbi_ref[j], preferred_element_type=F32).reshape(tt, nbs, sw)

    for rg in range(nbs // SUBLANES):
        rows = slice(rg * SUBLANES, (rg + 1) * SUBLANES)
        for ch in range(SSM_STATE // SCAN_LANES):
            lanes = slice(ch * SCAN_LANES, (ch + 1) * SCAN_LANES)
            ar = jnp.broadcast_to(abr_ref[:, lanes], (SUBLANES, SCAN_LANES))
            ai = jnp.broadcast_to(abi_ref[:, lanes], (SUBLANES, SCAN_LANES))

            hr, hi = sr_ref[rows, lanes], si_ref[rows, lanes]
            for t in range(tt):
                hr, hi = (ar * hr - ai * hi + xr_s[t, rows, lanes],
                          ar * hi + ai * hr + xi_s[t, rows, lanes])
                xr_s[t, rows, lanes] = hr
                xi_s[t, rows, lanes] = hi
            sr_ref[rows, lanes] = hr
            si_ref[rows, lanes] = hi

    ys = []
    for j in range(blocks):
        st_r = xr_s[:, :, j * sw:(j + 1) * sw].reshape(r, sw)
        st_i = xi_s[:, :, j * sw:(j + 1) * sw].reshape(r, sw)
        ys.append(_mm(st_r, wcr_ref[j]) - _mm(st_i, wci_ref[j]))
    y = jnp.concatenate(ys, axis=1) + dsk_ref[...] * u
    ag = _mm(jax.nn.gelu(y), wglu_ref[...])
    mix = ag[:, :D_MODEL] * _sigmoid(ag[:, D_MODEL:])
    out_ref[...] = x + gm_ref[...] * mix.reshape(tt, nbs, D_MODEL)


def _s5(x, mod, nw, win, wbr, wbi, wcr, wci, dsk, wglu, abr, abi, s0r, s0i, *, tt, nbs):
    l, b, _ = x.shape
    has_state = s0r is not None
    xs = pl.BlockSpec((tt, nbs, D_MODEL), lambda i, j: (j, i, 0))
    mods = lambda col: pl.BlockSpec((nbs, D_MODEL), lambda i, j: (i, col))
    full = _resident
    st = pl.BlockSpec((nbs, SSM_STATE), lambda i, j: (i, 0))
    consts = [nw, win, wbr, wbi, wcr, wci, dsk, wglu, abr, abi]
    in_specs = [xs, mods(0), mods(1), mods(2)] + [full(a) for a in consts]
    args = [x, mod, mod, mod] + consts
    if has_state:
        in_specs += [st, st]
        args += [s0r, s0i]
    return pl.pallas_call(
        functools.partial(_s5_body, tt=tt, nbs=nbs, has_state=has_state),
        grid=(b // nbs, l // tt),
        in_specs=in_specs,
        out_specs=[xs, st, st],
        out_shape=[jax.ShapeDtypeStruct((l, b, D_MODEL), F32),
                   jax.ShapeDtypeStruct((b, SSM_STATE), F32),
                   jax.ShapeDtypeStruct((b, SSM_STATE), F32)],
        scratch_shapes=[pltpu.VMEM((tt, nbs, SSM_STATE), F32), pltpu.VMEM((tt, nbs, SSM_STATE), F32)],
        compiler_params=_params("parallel", "arbitrary"),
        name="s5",
    )(*args)


def _s5_discretize(lam_re, lam_im, log_dt, b_re, b_im, c_re, c_im):
    dt = jnp.exp(log_dt)[:, None]
    mag = jnp.exp(lam_re * dt)
    ang = lam_im * dt
    ab_re, ab_im = mag * jnp.cos(ang), mag * jnp.sin(ang)
    den = lam_re * lam_re + lam_im * lam_im
    f_re = ((ab_re - 1.0) * lam_re + ab_im * lam_im) / den
    f_im = (ab_im * lam_re - (ab_re - 1.0) * lam_im) / den
    bb_re = f_re[..., None] * b_re - f_im[..., None] * b_im
    bb_im = f_re[..., None] * b_im + f_im[..., None] * b_re
    blocks = D_MODEL // MXU_DIM
    gpb = SSM_GROUPS // blocks

    def diag_blocks(t, rows_per_group, cols_per_group):
        tiled = jnp.tile(t.reshape(blocks, gpb * rows_per_group, cols_per_group), (1, 1, gpb))
        rg = lax.broadcasted_iota(jnp.int32, tiled.shape, 1) // rows_per_group
        cg = lax.broadcasted_iota(jnp.int32, tiled.shape, 2) // cols_per_group
        return jnp.where(rg == cg, tiled, 0.0).astype(BF16)

    b_blocks = lambda bb: diag_blocks(jnp.swapaxes(bb, 1, 2), SSM_GROUP, SSM_P)
    c_blocks = lambda cc: diag_blocks(jnp.swapaxes(cc, 1, 2), SSM_P, SSM_GROUP)

    return (ab_re.reshape(1, SSM_STATE), ab_im.reshape(1, SSM_STATE),
            b_blocks(bb_re), b_blocks(bb_im), c_blocks(c_re), c_blocks(c_im))


def _trunk(x, mods_mix, mods_ffn, conv0, delta0, re0, im0, wts, *, nb_tok, tt_tok, nb_moe, tt_moe, nb_delta, tt_delta, chunk,
           tt_s5, nbs_s5):
    b, l, _ = x.shape
    mm3 = [m.reshape(b, 1, 3 * D_MODEL) for m in mods_mix]
    mf3 = [m.reshape(b, 1, 3 * D_MODEL) for m in mods_ffn]

    front = (x, mm3[0], mm3[0], wts['norm_mix'][0], wts['gdn_wq'], wts['gdn_wba'], wts['gdn_alog'], wts['gdn_dtb'])
    fused = nb_delta * tt_delta >= ROWS
    if not fused:
        front = _gdn_front(*front, nb=nb_tok, tt=tt_tok, chunk=chunk)
    o, s_delta, s_conv = _gdn_delta(front, wts['gdn_cw'], wts['gdn_onorm'], delta0, conv0, dims=(b, l),
                                    nb=nb_delta, tt=tt_delta, chunk=chunk, fused=fused)
    prompt_like = nb_tok == 1
    moe_w = lambda i: (wts['norm_ffn'][i], wts['moe_wr'][i], wts['moe_br'][i],
                       wts['moe_w1'], wts['moe_w3'], wts['moe_w2'])
    x1 = _moe(x, o, wts['gdn_wout'], mm3[0], mf3[0], *moe_w(0), None, layer=0, dims=(b, l), nb=nb_moe, tt=tt_moe,
              in_time_major=False, out_time_major=prompt_like)
    xt = x1.reshape(l, b, D_MODEL) if prompt_like else jnp.swapaxes(x1, 0, 1)
    x2, s_re, s_im = _s5(xt, mods_mix[1], wts['norm_mix'][1], wts['s5_win'], wts['s5_wbr'], wts['s5_wbi'],
                         wts['s5_wcr'], wts['s5_wci'], wts['s5_d'], wts['s5_wglu'], wts['s5_abr'], wts['s5_abi'],
                         re0, im0, tt=tt_s5, nbs=nbs_s5)
    x2 = x2.reshape(l, b * D_MODEL) if prompt_like else jnp.swapaxes(x2, 0, 1)
    y = _moe(x2, None, None, None, mf3[1], *moe_w(1), wts['norm_final'], layer=1, dims=(b, l), nb=nb_moe, tt=tt_moe,
             in_time_major=prompt_like, out_time_major=False)
    return (y, s_delta[None], s_conv[None],
            s_re.reshape(1, b, SSM_GROUPS, SSM_P), s_im.reshape(1, b, SSM_GROUPS, SSM_P))


def kernel(x_prompt, x_sample, state_delta, state_conv, state_ssm_re, state_ssm_im, c_prompt, c_sample, norm_mix, w_mod_mix, b_mod_mix, norm_ffn, w_mod_ffn, b_mod_ffn, norm_final, gdn_w_in, gdn_conv_w, gdn_a_log, gdn_dt_bias, gdn_o_norm, gdn_w_out, s5_w_in, s5_lam_re, s5_lam_im, s5_log_dt, s5_b_re, s5_b_im, s5_c_re, s5_c_im, s5_d, s5_w_glu, moe_w_rg, moe_b_rg, moe_w_re, moe_b_re, moe_w1, moe_w3, moe_w2):
    bp = x_prompt.shape[0]
    depth = norm_mix.shape[0]

    lane_pad = lambda a, lo: jnp.pad(a, [(0, 0)] * (a.ndim - 1) + [(lo, LANES - lo - a.shape[-1])])
    abr, abi, wbr, wbi, wcr, wci = _s5_discretize(s5_lam_re[0], s5_lam_im[0], s5_log_dt[0], s5_b_re[0],
                                                  s5_b_im[0], s5_c_re[0], s5_c_im[0])
    wts = dict(
        norm_mix=norm_mix.reshape(depth, 1, D_MODEL), norm_ffn=norm_ffn.reshape(depth, 1, D_MODEL),
        norm_final=norm_final.reshape(1, D_MODEL),
        gdn_wq=gdn_w_in[0, :, :CONV_DIM + QK_DIM].astype(BF16),
        gdn_wba=lane_pad(gdn_w_in[0, :, CONV_DIM + QK_DIM:], 0).astype(BF16),
        gdn_alog=lane_pad(gdn_a_log[0][None], N_HEADS), gdn_dtb=lane_pad(gdn_dt_bias[0][None], N_HEADS),
        gdn_cw=jnp.pad(gdn_conv_w[0], ((0, SUBLANES - CONV_TAPS), (0, 0))),
        gdn_onorm=gdn_o_norm[0][None], gdn_wout=gdn_w_out[0].astype(BF16),
        s5_win=s5_w_in[0].astype(BF16), s5_wbr=wbr, s5_wbi=wbi, s5_wcr=wcr, s5_wci=wci,
        s5_d=s5_d[0].reshape(1, D_MODEL), s5_wglu=s5_w_glu[0].astype(BF16), s5_abr=abr, s5_abi=abi,
        moe_wr=lane_pad(jnp.concatenate([moe_w_rg, moe_w_re], axis=-1), 0),
        moe_br=lane_pad(jnp.concatenate([moe_b_rg, moe_b_re], axis=-1), 0)[:, None, :],
        moe_w1=jnp.swapaxes(moe_w1, 1, 2).reshape(depth, D_MODEL, N_EXPERTS * D_FF).astype(BF16),
        moe_w3=jnp.swapaxes(moe_w3, 1, 2).reshape(depth, D_MODEL, N_EXPERTS * D_FF).astype(BF16),
        moe_w2=moe_w2.reshape(depth, N_EXPERTS * D_FF, D_MODEL).astype(BF16),
    )

    c_all = jnp.concatenate([c_prompt, c_sample], axis=0)
    m_mix = _ada_mod(c_all, w_mod_mix, b_mod_mix)
    m_ffn = _ada_mod(c_all, w_mod_ffn, b_mod_ffn)

    y_p, p_delta, p_conv, p_re, p_im = _trunk(
        x_prompt, [m_mix[i, :bp] for i in range(depth)], [m_ffn[i, :bp] for i in range(depth)],
        None, None, None, None, wts,
        nb_tok=1, tt_tok=ROWS, nb_moe=1, tt_moe=MOE_ROWS, nb_delta=1, tt_delta=ROWS, chunk=CHUNK, tt_s5=S5_ROWS // SUBLANES, nbs_s5=SUBLANES)
    ls = x_sample.shape[1]
    y_s, s_delta, s_conv, s_re, s_im = _trunk(
        x_sample, [m_mix[i, bp:] for i in range(depth)], [m_ffn[i, bp:] for i in range(depth)],
        state_conv[0], state_delta[0], state_ssm_re[0].reshape(-1, SSM_STATE),
        state_ssm_im[0].reshape(-1, SSM_STATE), wts,
        nb_tok=ROWS // ls, tt_tok=ls, nb_moe=MOE_ROWS // ls, tt_moe=ls, nb_delta=MXU_DIM // (ls * N_HEADS), tt_delta=ls, chunk=ls,
        tt_s5=ls, nbs_s5=ROWS // ls)
    return (y_p, y_s, p_delta, p_conv, p_re, p_im, s_delta, s_conv, s_re, s_im)
```

```python
import functools

import jax
import jax.numpy as jnp
from jax import lax
from jax.experimental import pallas as pl
from jax.experimental.pallas import tpu as pltpu

F32 = jnp.float32
BF16 = jnp.bfloat16

D_MODEL = 1024
N_HEADS = 8
HEAD_DIM = 128
QK_DIM = N_HEADS * HEAD_DIM
CONV_DIM = 3 * QK_DIM
CONV_TAPS = 4
CHUNK = 64
SSM_GROUPS = 64
SSM_GROUP = 16
SSM_P = 64
SSM_STATE = SSM_GROUPS * SSM_P
N_GROUPS = 4
EXPERTS_PER_GROUP = 4
N_EXPERTS = 16
D_FF = 256
EPS = 1e-6

LANES = 128
SUBLANES = 8
MXU_DIM = 256
ROWS = 256
MOE_ROWS = 512
S5_ROWS = 512
SCAN_LANES = 512
VMEM_LIMIT = 56 * 1024 * 1024


def _mm(a, b):
    return jnp.dot(a.astype(BF16), b.astype(BF16), preferred_element_type=F32)


def _bmm(a, b):
    return lax.dot_general(a.astype(BF16), b.astype(BF16), (((2,), (1,)), ((0,), (0,))),
                           preferred_element_type=F32)


def _bmm_nt(a, b):
    return lax.dot_general(a.astype(BF16), b.astype(BF16), (((2,), (2,)), ((0,), (0,))),
                           preferred_element_type=F32)


def _split_bf16(x, terms):
    out = []
    for _ in range(terms - 1):
        p = x.astype(BF16)
        out.append(p)
        x = x - p.astype(F32)
    out.append(x.astype(BF16))
    return out


def _rms(x, w):
    return x * lax.rsqrt(jnp.mean(x * x, axis=-1, keepdims=True) + EPS) * w


def _sigmoid(x):
    return 0.5 * jnp.tanh(0.5 * x) + 0.5


def _silu(x):
    h = 0.5 * x
    return h * jnp.tanh(h) + h


def _softplus(x):
    return jnp.maximum(x, 0.0) + jnp.log1p(jnp.exp(-jnp.abs(x)))


def _lane_col(tile, lane):
    li = lax.broadcasted_iota(jnp.int32, tile.shape, 1)
    return jnp.sum(jnp.where(li == lane, tile, 0.0), axis=1, keepdims=True)


def _log2(n):
    assert n & (n - 1) == 0
    return n.bit_length() - 1


def _params(*sem):
    return pltpu.CompilerParams(dimension_semantics=sem, vmem_limit_bytes=VMEM_LIMIT)


def _resident(a):
    return pl.BlockSpec(a.shape, lambda *_: (0,) * a.ndim, pipeline_mode=pl.Buffered(1))


def _mod_body(c_ref, w_ref, b_ref, o_ref):
    o_ref[0] = _mm(_silu(c_ref[...]), w_ref[0]) + b_ref[0]


def _ada_mod(c_all, w_mod, b_mod):
    nl, nr, tn = w_mod.shape[0], c_all.shape[0], 768
    return pl.pallas_call(
        _mod_body,
        grid=(nl, 3 * D_MODEL // tn),
        in_specs=[pl.BlockSpec((nr, D_MODEL), lambda l, j: (0, 0)),
                  pl.BlockSpec((1, D_MODEL, tn), lambda l, j: (l, 0, j)),
                  pl.BlockSpec((1, 1, tn), lambda l, j: (l, 0, j))],
        out_specs=pl.BlockSpec((1, nr, tn), lambda l, j: (l, 0, j)),
        out_shape=jax.ShapeDtypeStruct((nl, nr, 3 * D_MODEL), F32),
        compiler_params=_params("parallel", "parallel"),
        name="ada_mod",
    )(c_all, w_mod, b_mod.reshape(nl, 1, 3 * D_MODEL))


def _gdn_front_math(x_ref, sh_ref, sc_ref, nw_ref, wq_ref, wba_ref, alog_ref, dtb_ref, chunk):
    x = x_ref[...]
    nb, tt, _ = x.shape
    r = nb * tt
    h = _rms(x, nw_ref[...]) * (1.0 + sc_ref[...]) + sh_ref[...]
    hb = h.reshape(r, D_MODEL).astype(BF16)
    pq = jnp.dot(hb, wq_ref[...], preferred_element_type=F32)
    ba = jnp.dot(hb, wba_ref[...], preferred_element_type=F32)
    beta = _sigmoid(ba)
    g = -jnp.exp(alog_ref[...]) * _softplus(ba + dtb_ref[...])
    ri = lax.broadcasted_iota(jnp.int32, (r, r), 0)
    ci = lax.broadcasted_iota(jnp.int32, (r, r), 1)
    same = (ri >> _log2(chunk)) == (ci >> _log2(chunk))
    low = jnp.where(same & (ri >= ci), 1.0, 0.0).astype(BF16)
    ones = jnp.where(same, 1.0, 0.0).astype(BF16)
    gc = jnp.zeros((r, LANES), F32)
    gl = jnp.zeros((r, LANES), F32)
    for piece in _split_bf16(g, 3):
        gc = gc + jnp.dot(low, piece, preferred_element_type=F32)
        gl = gl + jnp.dot(ones, piece, preferred_element_type=F32)
    return pq, jnp.concatenate([beta, gc, gl], axis=-1)


def _gdn_front_body(x_ref, sh_ref, sc_ref, nw_ref, wq_ref, wba_ref, alog_ref, dtb_ref,
                    qkv_ref, z_ref, gt_ref, *, chunk):
    nb, tt, _ = x_ref.shape
    pq, gates = _gdn_front_math(x_ref, sh_ref, sc_ref, nw_ref, wq_ref, wba_ref, alog_ref, dtb_ref, chunk)
    qkv_ref[...] = pq[:, :CONV_DIM].reshape(nb, tt, CONV_DIM)
    z_ref[...] = pq[:, CONV_DIM:].reshape(nb, tt, QK_DIM)
    gt_ref[...] = gates.reshape(nb, tt, 3 * LANES)


def _gdn_front(x, shift, scale, nw, wq, wba, alog, dtb, *, nb, tt, chunk):
    b, l, _ = x.shape
    tok = lambda w: pl.BlockSpec((nb, tt, w), lambda i, j: (i, j, 0))
    mod = lambda col: pl.BlockSpec((nb, 1, D_MODEL), lambda i, j: (i, 0, col))
    full = _resident
    return pl.pallas_call(
        functools.partial(_gdn_front_body, chunk=chunk),
        grid=(b // nb, l // tt),
        in_specs=[tok(D_MODEL), mod(0), mod(1), full(nw), full(wq), full(wba), full(alog), full(dtb)],
        out_specs=[tok(CONV_DIM), tok(QK_DIM), tok(3 * LANES)],
        out_shape=[jax.ShapeDtypeStruct((b, l, CONV_DIM), F32),
                   jax.ShapeDtypeStruct((b, l, QK_DIM), F32),
                   jax.ShapeDtypeStruct((b, l, 3 * LANES), F32)],
        compiler_params=_params("parallel", "parallel"),
        name="gdn_front",
    )(x, shift, scale, nw, wq, wba, alog, dtb)


def _inv_unit_lower(a, ri, ci, chunk):
    base = min(16, chunk)
    same = lambda size: (ri >> _log2(size)) == (ci >> _log2(size))
    n = jnp.where(same(base), -a, 0.0)
    q, m, k = n, n, 1
    while 2 * k < base:
        m = _bmm(m, m)
        q = q + m + _bmm(q, m)
        k *= 2
    size = base
    while size < chunk:
        e = jnp.where(same(2 * size) & jnp.logical_not(same(size)), a, 0.0)
        x = e + _bmm(q, e)
        q = q - (x + _bmm(x, q))
        size *= 2
    return q


def _gdn_delta_body(*refs, nb, tt, chunk, has_state, fused):
    it = iter(refs)
    take = lambda n: [next(it) for _ in range(n)]
    if fused:
        front_refs = take(8)
    else:
        qkv_ref, z_ref, gt_ref = take(3)
    cw_ref, on_ref = take(2)
    if has_state:
        s0_ref, c0_ref = take(2)
    o_ref, s_ref, cn_ref, xc, qkv_s, o_s, u_s, wq_s, qk_s, kdt_s = take(10)
    if fused:
        z_ref, gt_ref = take(2)
        pq, gates = _gdn_front_math(*front_refs, chunk)
        z_ref[...] = pq[:, CONV_DIM:].reshape(nb, tt, QK_DIM)
        gt_ref[...] = gates.reshape(nb, tt, 3 * LANES)
        qkv_in = pq[:, :CONV_DIM].reshape(nb, tt, CONV_DIM)
    else:
        qkv_in = qkv_ref[...]
    hist = CONV_TAPS - 1
    pad = SUBLANES

    @pl.when(pl.program_id(1) == 0)
    def _():
        if has_state:
            s_ref[...] = s0_ref[...]
            xc[:, pad - hist:pad, :] = c0_ref[...]
        else:
            s_ref[...] = jnp.zeros(s_ref.shape, F32)
            xc[:, pad - hist:pad, :] = jnp.zeros((nb, hist, CONV_DIM), F32)

    xc[:, pad:pad + tt, :] = qkv_in
    cw = cw_ref[...]
    y = xc[:, pad - hist:pad - hist + tt, :] * cw[0:1]
    for j in range(1, CONV_TAPS):
        y = y + xc[:, pad - hist + j:pad - hist + j + tt, :] * cw[j:j + 1]
    qkv_s[...] = _silu(y)
    last = xc[:, pad + tt - hist:pad + tt, :]
    cn_ref[...] = last
    xc[:, pad - hist:pad, :] = last

    for hd in range(N_HEADS):
        for part, scl in ((0, HEAD_DIM ** -0.5), (1, 1.0)):
            lo = part * QK_DIM + hd * HEAD_DIM
            xh = qkv_s[:, :, lo:lo + HEAD_DIM]
            xh = xh * (lax.rsqrt(jnp.sum(xh * xh, axis=-1, keepdims=True) + EPS) * scl)
            qkv_s[:, :, lo:lo + HEAD_DIM] = xh

    per_stack = MXU_DIM // chunk
    units_all = [(s, hd) for s in range(nb) for hd in range(N_HEADS)]
    assert len(units_all) % per_stack == 0
    stacks = [units_all[i:i + per_stack] for i in range(0, len(units_all), per_stack)]
    r = MXU_DIM
    ri = lax.broadcasted_iota(jnp.int32, (r, r), 0)
    ci = lax.broadcasted_iota(jnp.int32, (r, r), 1)
    same = (ri >> _log2(chunk)) == (ci >> _log2(chunk))
    tril = same & (ri >= ci)
    strict = same & (ri > ci)
    eye = ri == ci

    n_chunks = tt // chunk
    dot = lambda p, q: jnp.dot(p, q, preferred_element_type=F32)

    def gather(f):
        return jnp.stack([jnp.concatenate([f(s, hd, slice(c * chunk, (c + 1) * chunk)) for s, hd in units], axis=0)
                          for c in range(n_chunks) for units in stacks], axis=0)

    qn = gather(lambda s, hd, rows: qkv_s[s, rows, hd * HEAD_DIM:(hd + 1) * HEAD_DIM])
    kn = gather(lambda s, hd, rows: qkv_s[s, rows, QK_DIM + hd * HEAD_DIM:QK_DIM + (hd + 1) * HEAD_DIM])
    v = gather(lambda s, hd, rows: qkv_s[s, rows, 2 * QK_DIM + hd * HEAD_DIM:2 * QK_DIM + (hd + 1) * HEAD_DIM])
    beta = gather(lambda s, hd, rows: _lane_col(gt_ref[s, rows, 0:LANES], hd))
    gc = gather(lambda s, hd, rows: _lane_col(gt_ref[s, rows, LANES:2 * LANES], N_HEADS + hd))
    gl = gather(lambda s, hd, rows: _lane_col(gt_ref[s, rows, 2 * LANES:3 * LANES], N_HEADS + hd))
    ns = qn.shape[0]
    eg = jnp.exp(gc)
    kb = kn * beta
    gc_row = jnp.sum(jnp.where(eye, gc, 0.0), axis=1, keepdims=True)
    decay = jnp.where(tril, jnp.exp(jnp.where(tril, gc - gc_row, 0.0)), 0.0)
    a = jnp.where(strict, _bmm_nt(kb, kn) * decay, 0.0)
    q_inv = _inv_unit_lower(a, ri, ci, chunk)
    rhs = jnp.concatenate([v * beta, kb * eg], axis=2)
    uw = rhs + _bmm(q_inv, rhs)
    u_s[...] = uw[:, :, :HEAD_DIM]
    per_unit = lambda t: t.reshape(ns, per_stack, chunk, HEAD_DIM)
    wq_s[...] = jnp.concatenate([per_unit(uw[:, :, HEAD_DIM:]), per_unit(qn * eg)], axis=2).reshape(
        ns, 2 * r, HEAD_DIM).astype(BF16)
    qk_s[...] = jnp.where(tril, _bmm_nt(qn, kn) * decay, 0.0).astype(BF16)
    kdt_s[...] = jnp.swapaxes(kn * jnp.exp(gl - gc), 1, 2).astype(BF16)

    unit_of_row = lax.broadcasted_iota(jnp.int32, (r, HEAD_DIM), 0) >> _log2(chunk)
    for c in range(n_chunks):
        rows = slice(c * chunk, (c + 1) * chunk)
        for k, units in enumerate(stacks):
            idx = c * len(stacks) + k
            ws, qs = [], []
            for i, (s, hd) in enumerate(units):
                both = dot(wq_s[idx, 2 * i * chunk:2 * (i + 1) * chunk, :], s_ref[s, hd].astype(BF16))
                ws.append(both[:chunk])
                qs.append(both[chunk:])
            v_new = (u_s[idx] - jnp.concatenate(ws, axis=0)).astype(BF16)
            o = jnp.concatenate(qs, axis=0) + dot(qk_s[idx], v_new)
            kdt = kdt_s[idx]
            for i, (s, hd) in enumerate(units):
                o_s[s, rows, hd * HEAD_DIM:(hd + 1) * HEAD_DIM] = o[i * chunk:(i + 1) * chunk]
                g_last = jnp.exp(_lane_col(gt_ref[s, c * chunk:c * chunk + 1, 2 * LANES:3 * LANES], N_HEADS + hd))
                upd = dot(kdt, jnp.where(unit_of_row == i, v_new, jnp.zeros_like(v_new)))
                s_ref[s, hd] = s_ref[s, hd] * g_last + upd

    on = on_ref[...]
    for hd in range(N_HEADS):
        sl = slice(hd * HEAD_DIM, (hd + 1) * HEAD_DIM)
        o_ref[:, :, sl] = _rms(o_s[:, :, sl], on) * _silu(z_ref[:, :, sl])


def _gdn_delta(front, cw, onorm, s0, c0, *, dims, nb, tt, chunk, fused):
    b, l = dims
    has_state = s0 is not None
    n_stacks = nb * tt * N_HEADS // MXU_DIM
    tok = lambda w: pl.BlockSpec((nb, tt, w), lambda i, j: (i, j, 0))
    mod = lambda col: pl.BlockSpec((nb, 1, D_MODEL), lambda i, j: (i, 0, col))
    full = _resident
    st_spec = pl.BlockSpec((nb, N_HEADS, HEAD_DIM, HEAD_DIM), lambda i, j: (i, 0, 0, 0))
    cv_spec = pl.BlockSpec((nb, CONV_TAPS - 1, CONV_DIM), lambda i, j: (i, 0, 0))
    if fused:
        in_specs = [tok(D_MODEL), mod(0), mod(1)] + [full(a) for a in front[3:]]
    else:
        in_specs = [tok(CONV_DIM), tok(QK_DIM), tok(3 * LANES)]
    in_specs += [full(cw), full(onorm)]
    args = list(front) + [cw, onorm]
    if has_state:
        in_specs += [st_spec, cv_spec]
        args += [s0, c0]
    scratch = [pltpu.VMEM((nb, tt + SUBLANES, CONV_DIM), F32),
               pltpu.VMEM((nb, tt, CONV_DIM), F32),
               pltpu.VMEM((nb, tt, QK_DIM), F32),
               pltpu.VMEM((n_stacks, MXU_DIM, HEAD_DIM), F32),
               pltpu.VMEM((n_stacks, 2 * MXU_DIM, HEAD_DIM), BF16),
               pltpu.VMEM((n_stacks, MXU_DIM, MXU_DIM), BF16),
               pltpu.VMEM((n_stacks, HEAD_DIM, MXU_DIM), BF16)]
    if fused:
        scratch += [pltpu.VMEM((nb, tt, QK_DIM), F32), pltpu.VMEM((nb, tt, 3 * LANES), F32)]
    return pl.pallas_call(
        functools.partial(_gdn_delta_body, nb=nb, tt=tt, chunk=chunk, has_state=has_state, fused=fused),
        grid=(b // nb, l // tt),
        in_specs=in_specs,
        out_specs=[tok(QK_DIM), st_spec, cv_spec],
        out_shape=[jax.ShapeDtypeStruct((b, l, QK_DIM), F32),
                   jax.ShapeDtypeStruct((b, N_HEADS, HEAD_DIM, HEAD_DIM), F32),
                   jax.ShapeDtypeStruct((b, CONV_TAPS - 1, CONV_DIM), F32)],
        scratch_shapes=scratch,
        compiler_params=_params("parallel", "arbitrary"),
        name="gdn_delta",
    )(*args)


def _route(logits):
    lane = lax.broadcasted_iota(jnp.int32, logits.shape, 1)
    neg, big = -1e30, 1 << 20
    rmax = lambda t: jnp.max(t, axis=1, keepdims=True)
    rsum = lambda t: jnp.sum(t, axis=1, keepdims=True)
    first = lambda cond: jnp.min(jnp.where(cond, lane, big), axis=1, keepdims=True)
    is_g = lane < N_GROUPS
    lg = jnp.where(is_g, logits, neg)
    mg = rmax(lg)
    g_idx = first(is_g & (lg >= mg))
    p_group = 1.0 / rsum(jnp.where(is_g, jnp.exp(lg - mg), 0.0))
    lo = N_GROUPS + EXPERTS_PER_GROUP * g_idx
    in_g = (lane >= lo) & (lane < lo + EXPERTS_PER_GROUP)
    le = jnp.where(in_g, logits, neg)
    m1 = rmax(le)
    se = rsum(jnp.where(in_g, jnp.exp(le - m1), 0.0))
    i1 = first(in_g & (le >= m1))
    le2 = jnp.where(lane == i1, neg, le)
    m2 = rmax(le2)
    i2 = first(in_g & (lane != i1) & (le2 >= m2))
    p1 = 1.0 / se
    p2 = jnp.exp(m2 - m1) / se
    tot = p1 + p2
    local = (jnp.where(lane == i1 - lo, p1 / tot * p_group, 0.0)
             + jnp.where(lane == i2 - lo, p2 / tot * p_group, 0.0))
    return g_idx, local


META_GROUP, META_RANK = 4, 5
PIECE_LANES = 8
SORT_PAD = 16
EXPERT_ROWS = 256
EXPERT_STEP = 64
SORT_ROWS = MOE_ROWS + N_GROUPS * SORT_PAD + EXPERT_ROWS
USED_ROWS = MOE_ROWS + LANES
MOE_HALVES = 2


def _moe_route_body(*refs, nb, tt, has_proj):
    it = iter(refs)
    x_ref = next(it)
    if has_proj:
        y_ref, wo_ref, gm_ref = next(it), next(it), next(it)
    sh_ref, sc_ref, nw_ref, wr_ref, br_ref, x1_ref, h_ref, meta_ref, cnt_ref = (next(it) for _ in range(9))
    r = nb * tt
    dot = lambda p, q: jnp.dot(p, q, preferred_element_type=F32)

    x = x_ref[...].reshape(nb, tt, D_MODEL)
    if has_proj:
        x = x + gm_ref[...] * _mm(y_ref[...].reshape(r, D_MODEL), wo_ref[...]).reshape(nb, tt, D_MODEL)
    h = (_rms(x, nw_ref[...]) * (1.0 + sc_ref[...]) + sh_ref[...]).reshape(r, D_MODEL)
    x1_ref[...] = x.reshape(r, D_MODEL)
    h_hi, h_lo = _split_bf16(h, 2)
    h_ref[...] = h_hi
    w_hi, w_lo = _split_bf16(wr_ref[...], 2)
    logits = dot(h_hi, w_hi) + (dot(h_hi, w_lo) + dot(h_lo, w_hi)) + br_ref[...]
    g_idx, local = _route(logits)

    lane = lax.broadcasted_iota(jnp.int32, (r, LANES), 1)
    onehot = jnp.where(lane == g_idx, 1.0, 0.0)
    ri = lax.broadcasted_iota(jnp.int32, (r, r), 0)
    ci = lax.broadcasted_iota(jnp.int32, (r, r), 1)
    earlier = jnp.where(ri > ci, 1.0, 0.0).astype(BF16)
    rank = jnp.sum(onehot * dot(earlier, onehot.astype(BF16)), axis=1, keepdims=True)
    meta_ref[...] = jnp.where(lane == META_GROUP, g_idx.astype(F32),
                              jnp.where(lane == META_RANK, rank, local))
    counts = jnp.sum(onehot, axis=0, keepdims=True).astype(jnp.int32)
    cnt_ref[...] = jnp.broadcast_to(counts, (1, SUBLANES, LANES))


def _moe_expert_body(cnt_ref, *refs, nb, tt, n_time_tiles, final_norm):
    it = iter(refs)
    x1_ref, h_ref, meta_ref, gf_ref, w1_ref, w3_ref, w2_ref = (next(it) for _ in range(7))
    if final_norm:
        nf_ref = next(it)
    out_ref, hs_s, ms_s, acc_s = (next(it) for _ in range(4))
    r = MOE_ROWS
    dot = lambda p, q: jnp.dot(p, q, preferred_element_type=F32)
    step = pl.program_id(0) * n_time_tiles + pl.program_id(1)
    grp = pl.program_id(2)

    def layout(half):
        counts = [cnt_ref[MOE_HALVES * step + half, g] for g in range(N_GROUPS)]
        starts, nxt = [], 0
        for g in range(N_GROUPS):
            starts.append(nxt)
            nxt = nxt + (((counts[g] + (SORT_PAD - 1)) >> _log2(SORT_PAD)) << _log2(SORT_PAD))
        return counts, starts

    def positions(half, starts):
        meta = meta_ref[half * r:(half + 1) * r, :]
        g_col = _lane_col(meta, META_GROUP)
        pos = _lane_col(meta, META_RANK)
        for g in range(N_GROUPS):
            pos = pos + jnp.where(g_col == g, jnp.asarray(starts[g], jnp.int32).astype(F32), 0.0)
        return meta, pos

    @pl.when(grp == 0)
    def _():
        for half in range(MOE_HALVES):
            meta, pos = positions(half, layout(half)[1])
            pos_row = jnp.broadcast_to(pos, (r, LANES)).T[0:1, :].astype(jnp.int32)
            perm = jnp.where(lax.broadcasted_iota(jnp.int32, (SORT_ROWS, r), 0) == pos_row, 1.0, 0.0).astype(BF16)
            hs_s[half] = dot(perm, h_ref[half * r:(half + 1) * r, :]).astype(BF16)
            lane = lax.broadcasted_iota(jnp.int32, (r, LANES), 1)
            pieces = _split_bf16(jnp.where(lane < EXPERTS_PER_GROUP, meta, 0.0), 3)
            packed = pieces[0].astype(F32)
            for k in (1, 2):
                packed = packed + pltpu.roll(pieces[k].astype(F32), k * PIECE_LANES, 1)
            ms = dot(perm, packed.astype(BF16))
            ms_s[half] = ms + pltpu.roll(ms, LANES - PIECE_LANES, 1) + pltpu.roll(ms, LANES - 2 * PIECE_LANES, 1)
        acc_s[...] = jnp.zeros(acc_s.shape, F32)

    pick = lambda vals: functools.reduce(lambda acc, gv: jnp.where(grp == gv[0], gv[1], acc),
                                         list(enumerate(vals))[1:], jnp.asarray(vals[0], jnp.int32))
    lay = [layout(half) for half in range(MOE_HALVES)]
    cnt_g = [pick(c) for c, _ in lay]
    start_g = [pick(s) for _, s in lay]
    def block(first, size):
        base = [jnp.minimum(s + first, SORT_ROWS - size) for s in start_g]
        rows = [pl.ds(pl.multiple_of(bs, SORT_PAD), size) for bs in base]
        iota = lax.broadcasted_iota(jnp.int32, (size, 1), 0)
        hb = jnp.concatenate([hs_s[half, rows[half], :] for half in range(MOE_HALVES)], axis=0)
        mb = []
        for half in range(MOE_HALVES):
            seg_row = iota + (base[half] - start_g[half])
            mb.append(jnp.where((seg_row >= first) & (seg_row < cnt_g[half]), ms_s[half, rows[half], :], 0.0))
        mb = jnp.concatenate(mb, axis=0)
        hid = _silu(dot(hb, w1_ref[...])) * dot(hb, w3_ref[...])
        parts = [(hid[:, j * D_FF:(j + 1) * D_FF] * _lane_col(mb, j)).astype(BF16)
                 for j in range(EXPERTS_PER_GROUP)]
        res = dot(jnp.concatenate(parts, axis=1), w2_ref[...])
        for half in range(MOE_HALVES):
            acc_s[half, rows[half], :] += res[half * size:(half + 1) * size]

    longest = functools.reduce(jnp.maximum, cnt_g)
    n_full = longest // EXPERT_ROWS

    def full_block(k, carry):
        block(k * EXPERT_ROWS, EXPERT_ROWS)
        return carry

    lax.fori_loop(0, n_full, full_block, 0)
    tail = longest - n_full * EXPERT_ROWS
    for size in range(EXPERT_STEP, EXPERT_ROWS + 1, EXPERT_STEP):
        pl.when((tail > size - EXPERT_STEP) & (tail <= size))(
            functools.partial(block, n_full * EXPERT_ROWS, size))

    @pl.when(grp == N_GROUPS - 1)
    def _():
        moes = []
        for half in range(MOE_HALVES):
            _, pos = positions(half, layout(half)[1])
            unperm = jnp.where(lax.broadcasted_iota(jnp.int32, (r, USED_ROWS), 1) == pos.astype(jnp.int32),
                               1.0, 0.0).astype(BF16)
            a_hi, a_lo = _split_bf16(acc_s[half, 0:USED_ROWS, :], 2)
            moes.append(dot(unperm, a_hi) + dot(unperm, a_lo))
        moe = jnp.concatenate(moes, axis=0)
        out = x1_ref[...].reshape(nb, tt, D_MODEL) + gf_ref[...] * moe.reshape(nb, tt, D_MODEL)
        if final_norm:
            out = _rms(out, nf_ref[...])
        out_ref[...] = out.reshape(out_ref.shape)


def _moe(x, y, wo, mod_mix, mod_ffn, nw, wr, br, w1, w3, w2, nf, *, layer, dims, nb, tt,
         in_time_major, out_time_major):
    b, l = dims
    has_proj, final_norm = y is not None, nf is not None
    gb, gt = b // nb, l // tt
    r, n_tiles = nb * tt, gb * gt
    bm = pl.BlockSpec((nb, tt, D_MODEL), lambda i, j, *_: (i, j, 0))
    tm = pl.BlockSpec((tt, D_MODEL), lambda i, j, *_: (j, i))
    mod = lambda col: pl.BlockSpec((nb, 1, D_MODEL), lambda i, j, *_: (i, 0, col))
    flat = lambda w: pl.BlockSpec((r, w), lambda i, j, *_: (i * gt + j, 0))
    x_spec = tm if in_time_major else bm
    in_specs, args = [x_spec], [x]
    if has_proj:
        in_specs += [x_spec, _resident(wo), mod(2)]
        args += [y, wo, mod_mix]
    in_specs += [mod(0), mod(1)] + [_resident(a) for a in (nw, wr, br)]
    args += [mod_ffn, mod_ffn, nw, wr, br]
    x1, h, meta, cnt = pl.pallas_call(
        functools.partial(_moe_route_body, nb=nb, tt=tt, has_proj=has_proj),
        grid=(gb, gt),
        in_specs=in_specs,
        out_specs=[flat(D_MODEL), flat(D_MODEL), flat(LANES),
                   pl.BlockSpec((1, SUBLANES, LANES), lambda i, j: (i * gt + j, 0, 0))],
        out_shape=[jax.ShapeDtypeStruct((n_tiles * r, D_MODEL), F32),
                   jax.ShapeDtypeStruct((n_tiles * r, D_MODEL), BF16),
                   jax.ShapeDtypeStruct((n_tiles * r, LANES), F32),
                   jax.ShapeDtypeStruct((n_tiles, SUBLANES, LANES), jnp.int32)],
        compiler_params=_params("parallel", "parallel"),
        name="moe_route",
    )(*args)

    nb2, tt2 = (nb, MOE_HALVES * tt) if nb == 1 else (MOE_HALVES * nb, tt)
    gb2, gt2 = b // nb2, l // tt2
    assert r == MOE_ROWS and gb2 * gt2 * MOE_HALVES == n_tiles
    gw = EXPERTS_PER_GROUP * D_FF
    bm2 = pl.BlockSpec((nb2, tt2, D_MODEL), lambda i, j, g, *_: (i, j, 0))
    tm2 = pl.BlockSpec((tt2, D_MODEL), lambda i, j, g, *_: (j, i))
    flat2 = lambda w: pl.BlockSpec((MOE_HALVES * r, w), lambda i, j, g, *_: (i * gt2 + j, 0))
    in_specs = [flat2(D_MODEL), flat2(D_MODEL), flat2(LANES),
                pl.BlockSpec((nb2, 1, D_MODEL), lambda i, j, g, *_: (i, 0, 2)),
                pl.BlockSpec((None, D_MODEL, gw), lambda i, j, g, *_: (layer, 0, g)),
                pl.BlockSpec((None, D_MODEL, gw), lambda i, j, g, *_: (layer, 0, g)),
                pl.BlockSpec((None, gw, D_MODEL), lambda i, j, g, *_: (layer, g, 0))]
    args = [x1, h, meta, mod_ffn, w1, w3, w2]
    if final_norm:
        in_specs.append(_resident(nf))
        args.append(nf)
    if out_time_major:
        assert nb == 1
        out_spec, out_shape = tm2, jax.ShapeDtypeStruct((l, b * D_MODEL), F32)
    else:
        out_spec, out_shape = bm2, jax.ShapeDtypeStruct((b, l, D_MODEL), F32)
    return pl.pallas_call(
        functools.partial(_moe_expert_body, nb=nb2, tt=tt2, n_time_tiles=gt2, final_norm=final_norm),
        grid_spec=pltpu.PrefetchScalarGridSpec(
            num_scalar_prefetch=1,
            grid=(gb2, gt2, N_GROUPS),
            in_specs=in_specs,
            out_specs=out_spec,
            scratch_shapes=[pltpu.VMEM((MOE_HALVES, SORT_ROWS, D_MODEL), BF16),
                            pltpu.VMEM((MOE_HALVES, SORT_ROWS, LANES), F32),
                            pltpu.VMEM((MOE_HALVES, SORT_ROWS, D_MODEL), F32)]),
        out_shape=out_shape,
        compiler_params=_params("parallel", "parallel", "arbitrary"),
        name="moe_experts",
    )(cnt[:, 0, :N_GROUPS], *args)


def _s5_body(*refs, tt, nbs, has_state):
    if has_state:
        (x_ref, sh_ref, sc_ref, gm_ref, nw_ref, win_ref, wbr_ref, wbi_ref, wcr_ref, wci_ref, dsk_ref,
         wglu_ref, abr_ref, abi_ref, s0r_ref, s0i_ref, out_ref, sr_ref, si_ref, xr_s, xi_s) = refs
    else:
        (x_ref, sh_ref, sc_ref, gm_ref, nw_ref, win_ref, wbr_ref, wbi_ref, wcr_ref, wci_ref, dsk_ref,
         wglu_ref, abr_ref, abi_ref, out_ref, sr_ref, si_ref, xr_s, xi_s) = refs
    r = tt * nbs
    blocks = D_MODEL // MXU_DIM
    sw = SSM_STATE // blocks

    @pl.when(pl.program_id(1) == 0)
    def _():
        if has_state:
            sr_ref[...] = s0r_ref[...]
            si_ref[...] = s0i_ref[...]
        else:
            sr_ref[...] = jnp.zeros(sr_ref.shape, F32)
            si_ref[...] = jnp.zeros(si_ref.shape, F32)

    x = x_ref[...]
    h = (_rms(x, nw_ref[...]) * (1.0 + sc_ref[...]) + sh_ref[...]).reshape(r, D_MODEL)
    u = _mm(h, win_ref[...])
    ub = u.astype(BF16)
    for j in range(blocks):
        uj = ub[:, j * MXU_DIM:(j + 1) * MXU_DIM]
        xr_s[:, :, j * sw:(j + 1) * sw] = jnp.dot(
            uj, wbr_ref[j], preferred_element_type=F32).reshape(tt, nbs, sw)
        xi_s[:, :, j * sw:(j + 1) * sw] = jnp.dot(
            uj, wbi_ref[j], preferred_element_type=F32).reshape(tt, nbs, sw)

    for rg in range(nbs // SUBLANES):
        rows = slice(rg * SUBLANES, (rg + 1) * SUBLANES)
        for ch in range(SSM_STATE // SCAN_LANES):
            lanes = slice(ch * SCAN_LANES, (ch + 1) * SCAN_LANES)
            ar = jnp.broadcast_to(abr_ref[:, lanes], (SUBLANES, SCAN_LANES))
            ai = jnp.broadcast_to(abi_ref[:, lanes], (SUBLANES, SCAN_LANES))

            hr, hi = sr_ref[rows, lanes], si_ref[rows, lanes]
            for t in range(tt):
                hr, hi = (ar * hr - ai * hi + xr_s[t, rows, lanes],
                          ar * hi + ai * hr + xi_s[t, rows, lanes])
                xr_s[t, rows, lanes] = hr
                xi_s[t, rows, lanes] = hi
            sr_ref[rows, lanes] = hr
            si_ref[rows, lanes] = hi

    ys = []
    for j in range(blocks):
        st_r = xr_s[:, :, j * sw:(j + 1) * sw].reshape(r, sw)
        st_i = xi_s[:, :, j * sw:(j + 1) * sw].reshape(r, sw)
        ys.append(_mm(st_r, wcr_ref[j]) - _mm(st_i, wci_ref[j]))
    y = jnp.concatenate(ys, axis=1) + dsk_ref[...] * u
    ag = _mm(jax.nn.gelu(y), wglu_ref[...])
    mix = ag[:, :D_MODEL] * _sigmoid(ag[:, D_MODEL:])
    out_ref[...] = x + gm_ref[...] * mix.reshape(tt, nbs, D_MODEL)


def _s5(x, mod, nw, win, wbr, wbi, wcr, wci, dsk, wglu, abr, abi, s0r, s0i, *, tt, nbs):
    l, b, _ = x.shape
    has_state = s0r is not None
    xs = pl.BlockSpec((tt, nbs, D_MODEL), lambda i, j: (j, i, 0))
    mods = lambda col: pl.BlockSpec((nbs, D_MODEL), lambda i, j: (i, col))
    full = _resident
    st = pl.BlockSpec((nbs, SSM_STATE), lambda i, j: (i, 0))
    consts = [nw, win, wbr, wbi, wcr, wci, dsk, wglu, abr, abi]
    in_specs = [xs, mods(0), mods(1), mods(2)] + [full(a) for a in consts]
    args = [x, mod, mod, mod] + consts
    if has_state:
        in_specs += [st, st]
        args += [s0r, s0i]
    return pl.pallas_call(
        functools.partial(_s5_body, tt=tt, nbs=nbs, has_state=has_state),
        grid=(b // nbs, l // tt),
        in_specs=in_specs,
        out_specs=[xs, st, st],
        out_shape=[jax.ShapeDtypeStruct((l, b, D_MODEL), F32),
                   jax.ShapeDtypeStruct((b, SSM_STATE), F32),
                   jax.ShapeDtypeStruct((b, SSM_STATE), F32)],
        scratch_shapes=[pltpu.VMEM((tt, nbs, SSM_STATE), F32), pltpu.VMEM((tt, nbs, SSM_STATE), F32)],
        compiler_params=_params("parallel", "arbitrary"),
        name="s5",
    )(*args)


def _s5_discretize(lam_re, lam_im, log_dt, b_re, b_im, c_re, c_im):
    dt = jnp.exp(log_dt)[:, None]
    mag = jnp.exp(lam_re * dt)
    ang = lam_im * dt
    ab_re, ab_im = mag * jnp.cos(ang), mag * jnp.sin(ang)
    den = lam_re * lam_re + lam_im * lam_im
    f_re = ((ab_re - 1.0) * lam_re + ab_im * lam_im) / den
    f_im = (ab_im * lam_re - (ab_re - 1.0) * lam_im) / den
    bb_re = f_re[..., None] * b_re - f_im[..., None] * b_im
    bb_im = f_re[..., None] * b_im + f_im[..., None] * b_re
    blocks = D_MODEL // MXU_DIM
    gpb = SSM_GROUPS // blocks

    def diag_blocks(t, rows_per_group, cols_per_group):
        tiled = jnp.tile(t.reshape(blocks, gpb * rows_per_group, cols_per_group), (1, 1, gpb))
        rg = lax.broadcasted_iota(jnp.int32, tiled.shape, 1) // rows_per_group
        cg = lax.broadcasted_iota(jnp.int32, tiled.shape, 2) // cols_per_group
        return jnp.where(rg == cg, tiled, 0.0).astype(BF16)

    b_blocks = lambda bb: diag_blocks(jnp.swapaxes(bb, 1, 2), SSM_GROUP, SSM_P)
    c_blocks = lambda cc: diag_blocks(jnp.swapaxes(cc, 1, 2), SSM_P, SSM_GROUP)

    return (ab_re.reshape(1, SSM_STATE), ab_im.reshape(1, SSM_STATE),
            b_blocks(bb_re), b_blocks(bb_im), c_blocks(c_re), c_blocks(c_im))


def _trunk(x, mods_mix, mods_ffn, conv0, delta0, re0, im0, wts, *, nb_tok, tt_tok, nb_moe, tt_moe, nb_delta, tt_delta, chunk,
           tt_s5, nbs_s5):
    b, l, _ = x.shape
    mm3 = [m.reshape(b, 1, 3 * D_MODEL) for m in mods_mix]
    mf3 = [m.reshape(b, 1, 3 * D_MODEL) for m in mods_ffn]

    front = (x, mm3[0], mm3[0], wts['norm_mix'][0], wts['gdn_wq'], wts['gdn_wba'], wts['gdn_alog'], wts['gdn_dtb'])
    fused = nb_delta * tt_delta >= ROWS
    if not fused:
        front = _gdn_front(*front, nb=nb_tok, tt=tt_tok, chunk=chunk)
    o, s_delta, s_conv = _gdn_delta(front, wts['gdn_cw'], wts['gdn_onorm'], delta0, conv0, dims=(b, l),
                                    nb=nb_delta, tt=tt_delta, chunk=chunk, fused=fused)
    prompt_like = nb_tok == 1
    moe_w = lambda i: (wts['norm_ffn'][i], wts['moe_wr'][i], wts['moe_br'][i],
                       wts['moe_w1'], wts['moe_w3'], wts['moe_w2'])
    x1 = _moe(x, o, wts['gdn_wout'], mm3[0], mf3[0], *moe_w(0), None, layer=0, dims=(b, l), nb=nb_moe, tt=tt_moe,
              in_time_major=False, out_time_major=prompt_like)
    xt = x1.reshape(l, b, D_MODEL) if prompt_like else jnp.swapaxes(x1, 0, 1)
    x2, s_re, s_im = _s5(xt, mods_mix[1], wts['norm_mix'][1], wts['s5_win'], wts['s5_wbr'], wts['s5_wbi'],
                         wts['s5_wcr'], wts['s5_wci'], wts['s5_d'], wts['s5_wglu'], wts['s5_abr'], wts['s5_abi'],
                         re0, im0, tt=tt_s5, nbs=nbs_s5)
    x2 = x2.reshape(l, b * D_MODEL) if prompt_like else jnp.swapaxes(x2, 0, 1)
    y = _moe(x2, None, None, None, mf3[1], *moe_w(1), wts['norm_final'], layer=1, dims=(b, l), nb=nb_moe, tt=tt_moe,
             in_time_major=prompt_like, out_time_major=False)
    return (y, s_delta[None], s_conv[None],
            s_re.reshape(1, b, SSM_GROUPS, SSM_P), s_im.reshape(1, b, SSM_GROUPS, SSM_P))


def kernel(x_prompt, x_sample, state_delta, state_conv, state_ssm_re, state_ssm_im, c_prompt, c_sample, norm_mix, w_mod_mix, b_mod_mix, norm_ffn, w_mod_ffn, b_mod_ffn, norm_final, gdn_w_in, gdn_conv_w, gdn_a_log, gdn_dt_bias, gdn_o_norm, gdn_w_out, s5_w_in, s5_lam_re, s5_lam_im, s5_log_dt, s5_b_re, s5_b_im, s5_c_re, s5_c_im, s5_d, s5_w_glu, moe_w_rg, moe_b_rg, moe_w_re, moe_b_re, moe_w1, moe_w3, moe_w2):
    bp = x_prompt.shape[0]
    depth = norm_mix.shape[0]

    lane_pad = lambda a, lo: jnp.pad(a, [(0, 0)] * (a.ndim - 1) + [(lo, LANES - lo - a.shape[-1])])
    abr, abi, wbr, wbi, wcr, wci = _s5_discretize(s5_lam_re[0], s5_lam_im[0], s5_log_dt[0], s5_b_re[0],
                                                  s5_b_im[0], s5_c_re[0], s5_c_im[0])
    wts = dict(
        norm_mix=norm_mix.reshape(depth, 1, D_MODEL), norm_ffn=norm_ffn.reshape(depth, 1, D_MODEL),
        norm_final=norm_final.reshape(1, D_MODEL),
        gdn_wq=gdn_w_in[0, :, :CONV_DIM + QK_DIM].astype(BF16),
        gdn_wba=lane_pad(gdn_w_in[0, :, CONV_DIM + QK_DIM:], 0).astype(BF16),
        gdn_alog=lane_pad(gdn_a_log[0][None], N_HEADS), gdn_dtb=lane_pad(gdn_dt_bias[0][None], N_HEADS),
        gdn_cw=jnp.pad(gdn_conv_w[0], ((0, SUBLANES - CONV_TAPS), (0, 0))),
        gdn_onorm=gdn_o_norm[0][None], gdn_wout=gdn_w_out[0].astype(BF16),
        s5_win=s5_w_in[0].astype(BF16), s5_wbr=wbr, s5_wbi=wbi, s5_wcr=wcr, s5_wci=wci,
        s5_d=s5_d[0].reshape(1, D_MODEL), s5_wglu=s5_w_glu[0].astype(BF16), s5_abr=abr, s5_abi=abi,
        moe_wr=lane_pad(jnp.concatenate([moe_w_rg, moe_w_re], axis=-1), 0),
        moe_br=lane_pad(jnp.concatenate([moe_b_rg, moe_b_re], axis=-1), 0)[:, None, :],
        moe_w1=jnp.swapaxes(moe_w1, 1, 2).reshape(depth, D_MODEL, N_EXPERTS * D_FF).astype(BF16),
        moe_w3=jnp.swapaxes(moe_w3, 1, 2).reshape(depth, D_MODEL, N_EXPERTS * D_FF).astype(BF16),
        moe_w2=moe_w2.reshape(depth, N_EXPERTS * D_FF, D_MODEL).astype(BF16),
    )

    c_all = jnp.concatenate([c_prompt, c_sample], axis=0)
    m_mix = _ada_mod(c_all, w_mod_mix, b_mod_mix)
    m_ffn = _ada_mod(c_all, w_mod_ffn, b_mod_ffn)

    y_p, p_delta, p_conv, p_re, p_im = _trunk(
        x_prompt, [m_mix[i, :bp] for i in range(depth)], [m_ffn[i, :bp] for i in range(depth)],
        None, None, None, None, wts,
        nb_tok=1, tt_tok=ROWS, nb_moe=1, tt_moe=MOE_ROWS, nb_delta=1, tt_delta=ROWS, chunk=CHUNK, tt_s5=S5_ROWS // SUBLANES, nbs_s5=SUBLANES)
    ls = x_sample.shape[1]
    y_s, s_delta, s_conv, s_re, s_im = _trunk(
        x_sample, [m_mix[i, bp:] for i in range(depth)], [m_ffn[i, bp:] for i in range(depth)],
        state_conv[0], state_delta[0], state_ssm_re[0].reshape(-1, SSM_STATE),
        state_ssm_im[0].reshape(-1, SSM_STATE), wts,
        nb_tok=ROWS // ls, tt_tok=ls, nb_moe=MOE_ROWS // ls, tt_moe=ls, nb_delta=2 * MXU_DIM // (ls * N_HEADS), tt_delta=ls, chunk=ls,
        tt_s5=ls, nbs_s5=ROWS // ls)
    return (y_p, y_s, p_delta, p_conv, p_re, p_im, s_delta, s_conv, s_re, s_im)
```

```python
import functools

import jax
import jax.numpy as jnp
from jax import lax
from jax.experimental import pallas as pl
from jax.experimental.pallas import tpu as pltpu

F32 = jnp.float32
BF16 = jnp.bfloat16

D_MODEL = 1024
N_HEADS = 8
HEAD_DIM = 128
QK_DIM = N_HEADS * HEAD_DIM
CONV_DIM = 3 * QK_DIM
CONV_TAPS = 4
CONV_PAD = 16
CHUNK = 64
SSM_GROUPS = 64
SSM_GROUP = 16
SSM_P = 64
SSM_STATE = SSM_GROUPS * SSM_P
N_GROUPS = 4
EXPERTS_PER_GROUP = 4
N_EXPERTS = 16
D_FF = 256
EPS = 1e-6

LANES = 128
SUBLANES = 8
MXU_DIM = 256
ROWS = 256
MOE_ROWS = 512
S5_ROWS = 512
SCAN_LANES = 512
VMEM_LIMIT = 56 * 1024 * 1024


def _mm(a, b):
    return jnp.dot(a.astype(BF16), b.astype(BF16), preferred_element_type=F32)


def _bmm(a, b):
    return lax.dot_general(a.astype(BF16), b.astype(BF16), (((2,), (1,)), ((0,), (0,))),
                           preferred_element_type=F32)


def _bmm_nt(a, b):
    return lax.dot_general(a.astype(BF16), b.astype(BF16), (((2,), (2,)), ((0,), (0,))),
                           preferred_element_type=F32)


def _split_bf16(x, terms):
    out = []
    for _ in range(terms - 1):
        p = x.astype(BF16)
        out.append(p)
        x = x - p.astype(F32)
    out.append(x.astype(BF16))
    return out


def _rms(x, w):
    return x * lax.rsqrt(jnp.mean(x * x, axis=-1, keepdims=True) + EPS) * w


def _sigmoid(x):
    return 0.5 * jnp.tanh(0.5 * x) + 0.5


def _silu(x):
    h = 0.5 * x
    return h * jnp.tanh(h) + h


def _softplus(x):
    return jnp.maximum(x, 0.0) + jnp.log1p(jnp.exp(-jnp.abs(x)))


def _lane_col(tile, lane):
    li = lax.broadcasted_iota(jnp.int32, tile.shape, 1)
    return jnp.sum(jnp.where(li == lane, tile, 0.0), axis=1, keepdims=True)


def _log2(n):
    assert n & (n - 1) == 0
    return n.bit_length() - 1


def _params(*sem):
    return pltpu.CompilerParams(dimension_semantics=sem, vmem_limit_bytes=VMEM_LIMIT)


def _resident(a):
    return pl.BlockSpec(a.shape, lambda *_: (0,) * a.ndim, pipeline_mode=pl.Buffered(1))


def _mod_body(c_ref, w_ref, b_ref, o_ref):
    o_ref[0] = _mm(_silu(c_ref[...]), w_ref[0]) + b_ref[0]


def _ada_mod(c_all, w_mod, b_mod):
    nl, nr, tn = w_mod.shape[0], c_all.shape[0], 768
    return pl.pallas_call(
        _mod_body,
        grid=(nl, 3 * D_MODEL // tn),
        in_specs=[pl.BlockSpec((nr, D_MODEL), lambda l, j: (0, 0)),
                  pl.BlockSpec((1, D_MODEL, tn), lambda l, j: (l, 0, j)),
                  pl.BlockSpec((1, 1, tn), lambda l, j: (l, 0, j))],
        out_specs=pl.BlockSpec((1, nr, tn), lambda l, j: (l, 0, j)),
        out_shape=jax.ShapeDtypeStruct((nl, nr, 3 * D_MODEL), F32),
        compiler_params=_params("parallel", "parallel"),
        name="ada_mod",
    )(c_all, w_mod, b_mod.reshape(nl, 1, 3 * D_MODEL))


def _gdn_front_h(x_ref, sh_ref, sc_ref, nw_ref):
    x = x_ref[...]
    h = _rms(x, nw_ref[...]) * (1.0 + sc_ref[...]) + sh_ref[...]
    return h.reshape(x.shape[0] * x.shape[1], D_MODEL).astype(BF16)


def _gdn_front_gates(hb, wba_ref, alog_ref, dtb_ref, chunk):
    r = hb.shape[0]
    ba = jnp.dot(hb, wba_ref[...], preferred_element_type=F32)
    beta = _sigmoid(ba)
    g = -jnp.exp(alog_ref[...]) * _softplus(ba + dtb_ref[...])
    ri = lax.broadcasted_iota(jnp.int32, (r, r), 0)
    ci = lax.broadcasted_iota(jnp.int32, (r, r), 1)
    same = (ri >> _log2(chunk)) == (ci >> _log2(chunk))
    low = jnp.where(same & (ri >= ci), 1.0, 0.0).astype(BF16)
    ones = jnp.where(same, 1.0, 0.0).astype(BF16)
    gc = jnp.zeros((r, LANES), F32)
    gl = jnp.zeros((r, LANES), F32)
    for piece in _split_bf16(g, 3):
        gc = gc + jnp.dot(low, piece, preferred_element_type=F32)
        gl = gl + jnp.dot(ones, piece, preferred_element_type=F32)
    return jnp.concatenate([beta, gc, gl], axis=-1)


def _gdn_front_body(x_ref, sh_ref, sc_ref, nw_ref, wq_ref, wba_ref, alog_ref, dtb_ref,
                    qkv_ref, z_ref, gt_ref, *, chunk):
    nb, tt, _ = x_ref.shape
    hb = _gdn_front_h(x_ref, sh_ref, sc_ref, nw_ref)
    pq = jnp.dot(hb, wq_ref[...], preferred_element_type=F32)
    qkv_ref[...] = pq[:, :CONV_DIM].reshape(nb, tt, CONV_DIM)
    z_ref[...] = pq[:, CONV_DIM:].reshape(nb, tt, QK_DIM)
    gt_ref[...] = _gdn_front_gates(hb, wba_ref, alog_ref, dtb_ref, chunk).reshape(nb, tt, 3 * LANES)


def _gdn_front(x, shift, scale, nw, wq, wba, alog, dtb, *, nb, tt, chunk):
    b, l, _ = x.shape
    tok = lambda w: pl.BlockSpec((nb, tt, w), lambda i, j: (i, j, 0))
    mod = lambda col: pl.BlockSpec((nb, 1, D_MODEL), lambda i, j: (i, 0, col))
    full = _resident
    return pl.pallas_call(
        functools.partial(_gdn_front_body, chunk=chunk),
        grid=(b // nb, l // tt),
        in_specs=[tok(D_MODEL), mod(0), mod(1), full(nw), full(wq), full(wba), full(alog), full(dtb)],
        out_specs=[tok(CONV_DIM), tok(QK_DIM), tok(3 * LANES)],
        out_shape=[jax.ShapeDtypeStruct((b, l, CONV_DIM), F32),
                   jax.ShapeDtypeStruct((b, l, QK_DIM), F32),
                   jax.ShapeDtypeStruct((b, l, 3 * LANES), F32)],
        compiler_params=_params("parallel", "parallel"),
        name="gdn_front",
    )(x, shift, scale, nw, wq, wba, alog, dtb)


def _inv_unit_lower(neg_a, ri, ci, chunk):
    base = min(16, chunk)
    same = lambda size: (ri >> _log2(size)) == (ci >> _log2(size))
    n = jnp.where(same(base), neg_a, 0.0)
    q, m, k = n, n, 1
    while 2 * k < base:
        m = _bmm(m, m)
        q = q + m + _bmm(q, m)
        k *= 2
    size = base
    while size < chunk:
        e = jnp.where(same(2 * size) & jnp.logical_not(same(size)), neg_a, 0.0)
        x = e + _bmm(q, e)
        q = q + (x + _bmm(x, q))
        size *= 2
    return q


def _gdn_delta_body(*refs, nb, tt, chunk, has_state, fused):
    it = iter(refs)
    take = lambda n: [next(it) for _ in range(n)]
    if fused:
        x_ref, sh_ref, sc_ref, nw_ref, wq_ref, wba_ref, alog_ref, dtb_ref = take(8)
    else:
        qkv_ref, z_ref, gt_ref = take(3)
    cw_ref, on_ref = take(2)
    if has_state:
        s0_ref, c0_ref = take(2)
    o_ref, s_ref, cn_ref, xc, qkv_s, o_s, u_s, wq_s, qk_s, kdt_s = take(10)
    if fused:
        z_ref, gt_ref = take(2)
        hb = _gdn_front_h(x_ref, sh_ref, sc_ref, nw_ref)
        pq = jnp.dot(hb, wq_ref[...], preferred_element_type=F32)
        z_ref[...] = pq[:, CONV_DIM:].reshape(nb, tt, QK_DIM)
        gt_ref[...] = _gdn_front_gates(hb, wba_ref, alog_ref, dtb_ref, chunk).reshape(nb, tt, 3 * LANES)
        qkv_in = pq[:, :CONV_DIM].reshape(nb, tt, CONV_DIM)
    else:
        qkv_in = qkv_ref[...]
    hist = CONV_TAPS - 1
    pad = CONV_PAD

    @pl.when(pl.program_id(1) == 0)
    def _():
        xc[:, 0:pad - hist, :] = jnp.zeros((nb, pad - hist, CONV_DIM), F32)
        if has_state:
            s_ref[...] = s0_ref[...]
            xc[:, pad - hist:pad, :] = c0_ref[...]
        else:
            s_ref[...] = jnp.zeros(s_ref.shape, F32)
            xc[:, pad - hist:pad, :] = jnp.zeros((nb, hist, CONV_DIM), F32)

    xc[:, pad:pad + tt, :] = qkv_in
    cw = cw_ref[...]
    lead = SUBLANES
    x_cur = xc[:, pad - lead:pad + tt, :]
    x_prev = xc[:, pad - lead - 1:pad + tt - 1, :]
    z = x_cur * cw[1:2] + x_prev * cw[0:1]
    y = x_cur[:, lead:] * cw[3:4] + x_prev[:, lead:] * cw[2:3] + z[:, lead - 2:lead - 2 + tt]
    qkv_s[...] = _silu(y)
    last = xc[:, pad + tt - hist:pad + tt, :]
    cn_ref[...] = last
    xc[:, pad - hist:pad, :] = last

    for hd in range(N_HEADS):
        for part, scl in ((0, HEAD_DIM ** -0.5), (1, 1.0)):
            lo = part * QK_DIM + hd * HEAD_DIM
            xh = qkv_s[:, :, lo:lo + HEAD_DIM]
            xh = xh * (lax.rsqrt(jnp.sum(xh * xh, axis=-1, keepdims=True) + EPS) * scl)
            qkv_s[:, :, lo:lo + HEAD_DIM] = xh


    per_stack = MXU_DIM // chunk
    units_all = [(s, hd) for s in range(nb) for hd in range(N_HEADS)]
    assert len(units_all) % per_stack == 0
    stacks = [units_all[i:i + per_stack] for i in range(0, len(units_all), per_stack)]
    r = MXU_DIM
    ri = lax.broadcasted_iota(jnp.int32, (r, r), 0)
    ci = lax.broadcasted_iota(jnp.int32, (r, r), 1)
    same = (ri >> _log2(chunk)) == (ci >> _log2(chunk))
    tril = same & (ri >= ci)
    strict = same & (ri > ci)
    eye = ri == ci

    n_chunks = tt // chunk
    dot = lambda p, q: jnp.dot(p, q, preferred_element_type=F32)

    def gather(f):
        return jnp.stack([jnp.concatenate([f(s, hd, slice(c * chunk, (c + 1) * chunk)) for s, hd in units], axis=0)
                          for c in range(n_chunks) for units in stacks], axis=0)

    qn = gather(lambda s, hd, rows: qkv_s[s, rows, hd * HEAD_DIM:(hd + 1) * HEAD_DIM])
    kn = gather(lambda s, hd, rows: qkv_s[s, rows, QK_DIM + hd * HEAD_DIM:QK_DIM + (hd + 1) * HEAD_DIM])
    v = gather(lambda s, hd, rows: qkv_s[s, rows, 2 * QK_DIM + hd * HEAD_DIM:2 * QK_DIM + (hd + 1) * HEAD_DIM])
    beta = gather(lambda s, hd, rows: _lane_col(gt_ref[s, rows, 0:LANES], hd))
    gc = gather(lambda s, hd, rows: _lane_col(gt_ref[s, rows, LANES:2 * LANES], N_HEADS + hd))
    gl = gather(lambda s, hd, rows: _lane_col(gt_ref[s, rows, 2 * LANES:3 * LANES], N_HEADS + hd))
    ns = qn.shape[0]
    eg = jnp.exp(gc)
    kb = kn * beta
    gc_row = jnp.sum(jnp.where(eye, gc, 0.0), axis=1, keepdims=True)
    diff = gc - gc_row
    decay = jnp.exp(jnp.where(tril, diff, -jnp.inf))
    decay_strict = jnp.exp(jnp.where(strict, diff, -jnp.inf))
    q_inv = _inv_unit_lower(_bmm_nt(-kb, kn) * decay_strict, ri, ci, chunk)
    rhs = jnp.concatenate([v * beta, kb * eg], axis=2)
    uw = rhs + _bmm(q_inv, rhs)
    u_s[...] = uw[:, :, :HEAD_DIM]
    per_unit = lambda t: t.reshape(ns, per_stack, chunk, HEAD_DIM)
    wq_s[...] = jnp.concatenate([per_unit(uw[:, :, HEAD_DIM:]), per_unit(qn * eg)], axis=2).reshape(
        ns, 2 * r, HEAD_DIM).astype(BF16)
    qk_s[...] = (_bmm_nt(qn, kn) * decay).astype(BF16)
    kdt_s[...] = jnp.swapaxes(kn * jnp.exp(gl - gc), 1, 2).astype(BF16)

    unit_of_row = lax.broadcasted_iota(jnp.int32, (r, HEAD_DIM), 0) >> _log2(chunk)
    for c in range(n_chunks):
        rows = slice(c * chunk, (c + 1) * chunk)
        for k, units in enumerate(stacks):
            idx = c * len(stacks) + k
            ws, qs = [], []
            for i, (s, hd) in enumerate(units):
                both = dot(wq_s[idx, 2 * i * chunk:2 * (i + 1) * chunk, :], s_ref[s, hd].astype(BF16))
                ws.append(both[:chunk])
                qs.append(both[chunk:])
            v_new = (u_s[idx] - jnp.concatenate(ws, axis=0)).astype(BF16)
            o = jnp.concatenate(qs, axis=0) + dot(qk_s[idx], v_new)
            kdt = kdt_s[idx]
            for i, (s, hd) in enumerate(units):
                o_s[s, rows, hd * HEAD_DIM:(hd + 1) * HEAD_DIM] = o[i * chunk:(i + 1) * chunk]
                g_last = jnp.exp(_lane_col(gt_ref[s, c * chunk:c * chunk + 1, 2 * LANES:3 * LANES], N_HEADS + hd))
                upd = dot(kdt, jnp.where(unit_of_row == i, v_new, jnp.zeros_like(v_new)))
                s_ref[s, hd] = s_ref[s, hd] * g_last + upd

    on = on_ref[...]
    for hd in range(N_HEADS):
        sl = slice(hd * HEAD_DIM, (hd + 1) * HEAD_DIM)
        o_ref[:, :, sl] = _rms(o_s[:, :, sl], on) * _silu(z_ref[:, :, sl])


def _gdn_delta(front, cw, onorm, s0, c0, *, dims, nb, tt, chunk, fused):
    b, l = dims
    has_state = s0 is not None
    n_stacks = nb * tt * N_HEADS // MXU_DIM
    tok = lambda w: pl.BlockSpec((nb, tt, w), lambda i, j: (i, j, 0))
    mod = lambda col: pl.BlockSpec((nb, 1, D_MODEL), lambda i, j: (i, 0, col))
    full = _resident
    st_spec = pl.BlockSpec((nb, N_HEADS, HEAD_DIM, HEAD_DIM), lambda i, j: (i, 0, 0, 0))
    cv_spec = pl.BlockSpec((nb, CONV_TAPS - 1, CONV_DIM), lambda i, j: (i, 0, 0))
    if fused:
        in_specs = [tok(D_MODEL), mod(0), mod(1)] + [full(a) for a in front[3:]]
    else:
        in_specs = [tok(CONV_DIM), tok(QK_DIM), tok(3 * LANES)]
    in_specs += [full(cw), full(onorm)]
    args = list(front) + [cw, onorm]
    if has_state:
        in_specs += [st_spec, cv_spec]
        args += [s0, c0]
    scratch = [pltpu.VMEM((nb, tt + CONV_PAD, CONV_DIM), F32),
               pltpu.VMEM((nb, tt, CONV_DIM), F32),
               pltpu.VMEM((nb, tt, QK_DIM), F32),
               pltpu.VMEM((n_stacks, MXU_DIM, HEAD_DIM), F32),
               pltpu.VMEM((n_stacks, 2 * MXU_DIM, HEAD_DIM), BF16),
               pltpu.VMEM((n_stacks, MXU_DIM, MXU_DIM), BF16),
               pltpu.VMEM((n_stacks, HEAD_DIM, MXU_DIM), BF16)]
    if fused:
        scratch += [pltpu.VMEM((nb, tt, QK_DIM), F32), pltpu.VMEM((nb, tt, 3 * LANES), F32)]
    return pl.pallas_call(
        functools.partial(_gdn_delta_body, nb=nb, tt=tt, chunk=chunk, has_state=has_state, fused=fused),
        grid=(b // nb, l // tt),
        in_specs=in_specs,
        out_specs=[tok(QK_DIM), st_spec, cv_spec],
        out_shape=[jax.ShapeDtypeStruct((b, l, QK_DIM), F32),
                   jax.ShapeDtypeStruct((b, N_HEADS, HEAD_DIM, HEAD_DIM), F32),
                   jax.ShapeDtypeStruct((b, CONV_TAPS - 1, CONV_DIM), F32)],
        scratch_shapes=scratch,
        compiler_params=_params("parallel", "arbitrary"),
        name="gdn_delta",
    )(*args)


def _route(logits):
    lane = lax.broadcasted_iota(jnp.int32, logits.shape, 1)
    neg, big = -1e30, 1 << 20
    rmax = lambda t: jnp.max(t, axis=1, keepdims=True)
    rsum = lambda t: jnp.sum(t, axis=1, keepdims=True)
    first = lambda cond: jnp.min(jnp.where(cond, lane, big), axis=1, keepdims=True)
    is_g = lane < N_GROUPS
    lg = jnp.where(is_g, logits, neg)
    mg = rmax(lg)
    g_idx = first(is_g & (lg >= mg))
    p_group = 1.0 / rsum(jnp.where(is_g, jnp.exp(lg - mg), 0.0))
    lo = N_GROUPS + EXPERTS_PER_GROUP * g_idx
    in_g = (lane >= lo) & (lane < lo + EXPERTS_PER_GROUP)
    le = jnp.where(in_g, logits, neg)
    m1 = rmax(le)
    se = rsum(jnp.where(in_g, jnp.exp(le - m1), 0.0))
    i1 = first(in_g & (le >= m1))
    le2 = jnp.where(lane == i1, neg, le)
    m2 = rmax(le2)
    i2 = first(in_g & (lane != i1) & (le2 >= m2))
    p1 = 1.0 / se
    p2 = jnp.exp(m2 - m1) / se
    tot = p1 + p2
    local = (jnp.where(lane == i1 - lo, p1 / tot * p_group, 0.0)
             + jnp.where(lane == i2 - lo, p2 / tot * p_group, 0.0))
    return g_idx, local


META_GROUP, META_RANK = 4, 5
PIECE_LANES = 8
SORT_PAD = 16
EXPERT_ROWS = 256
EXPERT_STEP = 32
SORT_ROWS = MOE_ROWS + N_GROUPS * SORT_PAD + EXPERT_ROWS
USED_ROWS = MOE_ROWS + LANES
MOE_HALVES = 2


def _moe_route_body(*refs, nb, tt, has_proj):
    it = iter(refs)
    x_ref = next(it)
    if has_proj:
        y_ref, wo_ref, gm_ref = next(it), next(it), next(it)
    sh_ref, sc_ref, nw_ref, wr_ref, br_ref, x1_ref, h_ref, meta_ref, cnt_ref = (next(it) for _ in range(9))
    r = nb * tt
    dot = lambda p, q: jnp.dot(p, q, preferred_element_type=F32)

    x = x_ref[...].reshape(nb, tt, D_MODEL)
    if has_proj:
        x = x + gm_ref[...] * _mm(y_ref[...].reshape(r, D_MODEL), wo_ref[...]).reshape(nb, tt, D_MODEL)
    h = (_rms(x, nw_ref[...]) * (1.0 + sc_ref[...]) + sh_ref[...]).reshape(r, D_MODEL)
    x1_ref[...] = x.reshape(r, D_MODEL)
    h_hi, h_lo = _split_bf16(h, 2)
    h_ref[...] = h_hi
    w_hi, w_lo = _split_bf16(wr_ref[...], 2)
    logits = dot(h_hi, w_hi) + (dot(h_hi, w_lo) + dot(h_lo, w_hi)) + br_ref[...]
    g_idx, local = _route(logits)

    lane = lax.broadcasted_iota(jnp.int32, (r, LANES), 1)
    onehot = jnp.where(lane == g_idx, 1.0, 0.0)
    ri = lax.broadcasted_iota(jnp.int32, (r, r), 0)
    ci = lax.broadcasted_iota(jnp.int32, (r, r), 1)
    earlier = jnp.where(ri > ci, 1.0, 0.0).astype(BF16)
    rank = jnp.sum(onehot * dot(earlier, onehot.astype(BF16)), axis=1, keepdims=True)
    meta_ref[...] = jnp.where(lane == META_GROUP, g_idx.astype(F32),
                              jnp.where(lane == META_RANK, rank, local))
    counts = jnp.sum(onehot, axis=0, keepdims=True).astype(jnp.int32)
    cnt_ref[...] = jnp.broadcast_to(counts, (1, SUBLANES, LANES))


def _moe_expert_body(cnt_ref, *refs, nb, tt, n_time_tiles, final_norm):
    it = iter(refs)
    x1_ref, h_ref, meta_ref, gf_ref, w1_ref, w3_ref, w2_ref = (next(it) for _ in range(7))
    if final_norm:
        nf_ref = next(it)
    out_ref, hs_s, ms_s, acc_s = (next(it) for _ in range(4))
    r = MOE_ROWS
    dot = lambda p, q: jnp.dot(p, q, preferred_element_type=F32)
    step = pl.program_id(0) * n_time_tiles + pl.program_id(1)
    grp = pl.program_id(2)

    def layout(half):
        counts = [cnt_ref[MOE_HALVES * step + half, g] for g in range(N_GROUPS)]
        starts, nxt = [], 0
        for g in range(N_GROUPS):
            starts.append(nxt)
            nxt = nxt + (((counts[g] + (SORT_PAD - 1)) >> _log2(SORT_PAD)) << _log2(SORT_PAD))
        return counts, starts

    def positions(half, starts):
        meta = meta_ref[half * r:(half + 1) * r, :]
        g_col = _lane_col(meta, META_GROUP)
        pos = _lane_col(meta, META_RANK)
        for g in range(N_GROUPS):
            pos = pos + jnp.where(g_col == g, jnp.asarray(starts[g], jnp.int32).astype(F32), 0.0)
        return meta, pos

    @pl.when(grp == 0)
    def _():
        for half in range(MOE_HALVES):
            meta, pos = positions(half, layout(half)[1])
            pos_row = jnp.broadcast_to(pos, (r, LANES)).T[0:1, :].astype(jnp.int32)
            perm = jnp.where(lax.broadcasted_iota(jnp.int32, (SORT_ROWS, r), 0) == pos_row, 1.0, 0.0).astype(BF16)
            hs_s[half] = dot(perm, h_ref[half * r:(half + 1) * r, :]).astype(BF16)
            lane = lax.broadcasted_iota(jnp.int32, (r, LANES), 1)
            pieces = _split_bf16(jnp.where(lane < EXPERTS_PER_GROUP, meta, 0.0), 3)
            packed = pieces[0].astype(F32)
            for k in (1, 2):
                packed = packed + pltpu.roll(pieces[k].astype(F32), k * PIECE_LANES, 1)
            ms = dot(perm, packed.astype(BF16))
            ms_s[half] = ms + pltpu.roll(ms, LANES - PIECE_LANES, 1) + pltpu.roll(ms, LANES - 2 * PIECE_LANES, 1)
        acc_s[...] = jnp.zeros(acc_s.shape, F32)

    pick = lambda vals: functools.reduce(lambda acc, gv: jnp.where(grp == gv[0], gv[1], acc),
                                         list(enumerate(vals))[1:], jnp.asarray(vals[0], jnp.int32))
    lay = [layout(half) for half in range(MOE_HALVES)]
    cnt_g = [pick(c) for c, _ in lay]
    start_g = [pick(s) for _, s in lay]
    def block(first, size):
        base = [jnp.minimum(s + first, SORT_ROWS - size) for s in start_g]
        rows = [pl.ds(pl.multiple_of(bs, SORT_PAD), size) for bs in base]
        iota = lax.broadcasted_iota(jnp.int32, (size, 1), 0)
        hb = jnp.concatenate([hs_s[half, rows[half], :] for half in range(MOE_HALVES)], axis=0)
        mb = []
        for half in range(MOE_HALVES):
            seg_row = iota + (base[half] - start_g[half])
            mb.append(jnp.where((seg_row >= first) & (seg_row < cnt_g[half]), ms_s[half, rows[half], :], 0.0))
        mb = jnp.concatenate(mb, axis=0)
        hid = _silu(dot(hb, w1_ref[...])) * dot(hb, w3_ref[...])
        parts = [(hid[:, j * D_FF:(j + 1) * D_FF] * _lane_col(mb, j)).astype(BF16)
                 for j in range(EXPERTS_PER_GROUP)]
        res = dot(jnp.concatenate(parts, axis=1), w2_ref[...])
        for half in range(MOE_HALVES):
            acc_s[half, rows[half], :] += res[half * size:(half + 1) * size]

    longest = functools.reduce(jnp.maximum, cnt_g)
    n_full = longest // EXPERT_ROWS

    def full_block(k, carry):
        block(k * EXPERT_ROWS, EXPERT_ROWS)
        return carry

    lax.fori_loop(0, n_full, full_block, 0)
    tail = longest - n_full * EXPERT_ROWS
    for size in range(EXPERT_STEP, EXPERT_ROWS + 1, EXPERT_STEP):
        pl.when((tail > size - EXPERT_STEP) & (tail <= size))(
            functools.partial(block, n_full * EXPERT_ROWS, size))

    @pl.when(grp == N_GROUPS - 1)
    def _():
        moes = []
        for half in range(MOE_HALVES):
            _, pos = positions(half, layout(half)[1])
            unperm = jnp.where(lax.broadcasted_iota(jnp.int32, (r, USED_ROWS), 1) == pos.astype(jnp.int32),
                               1.0, 0.0).astype(BF16)
            a_hi, a_lo = _split_bf16(acc_s[half, 0:USED_ROWS, :], 2)
            moes.append(dot(unperm, a_hi) + dot(unperm, a_lo))
        moe = jnp.concatenate(moes, axis=0)
        out = x1_ref[...].reshape(nb, tt, D_MODEL) + gf_ref[...] * moe.reshape(nb, tt, D_MODEL)
        if final_norm:
            out = _rms(out, nf_ref[...])
        out_ref[...] = out.reshape(out_ref.shape)


def _moe(x, y, wo, mod_mix, mod_ffn, nw, wr, br, w1, w3, w2, nf, *, layer, dims, nb, tt,
         in_time_major, out_time_major):
    b, l = dims
    has_proj, final_norm = y is not None, nf is not None
    gb, gt = b // nb, l // tt
    r, n_tiles = nb * tt, gb * gt
    bm = pl.BlockSpec((nb, tt, D_MODEL), lambda i, j, *_: (i, j, 0))
    tm = pl.BlockSpec((tt, D_MODEL), lambda i, j, *_: (j, i))
    mod = lambda col: pl.BlockSpec((nb, 1, D_MODEL), lambda i, j, *_: (i, 0, col))
    flat = lambda w: pl.BlockSpec((r, w), lambda i, j, *_: (i * gt + j, 0))
    x_spec = tm if in_time_major else bm
    in_specs, args = [x_spec], [x]
    if has_proj:
        in_specs += [x_spec, _resident(wo), mod(2)]
        args += [y, wo, mod_mix]
    in_specs += [mod(0), mod(1)] + [_resident(a) for a in (nw, wr, br)]
    args += [mod_ffn, mod_ffn, nw, wr, br]
    x1, h, meta, cnt = pl.pallas_call(
        functools.partial(_moe_route_body, nb=nb, tt=tt, has_proj=has_proj),
        grid=(gb, gt),
        in_specs=in_specs,
        out_specs=[flat(D_MODEL), flat(D_MODEL), flat(LANES),
                   pl.BlockSpec((1, SUBLANES, LANES), lambda i, j: (i * gt + j, 0, 0))],
        out_shape=[jax.ShapeDtypeStruct((n_tiles * r, D_MODEL), F32),
                   jax.ShapeDtypeStruct((n_tiles * r, D_MODEL), BF16),
                   jax.ShapeDtypeStruct((n_tiles * r, LANES), F32),
                   jax.ShapeDtypeStruct((n_tiles, SUBLANES, LANES), jnp.int32)],
        compiler_params=_params("parallel", "parallel"),
        name="moe_route",
    )(*args)

    nb2, tt2 = (nb, MOE_HALVES * tt) if nb == 1 else (MOE_HALVES * nb, tt)
    gb2, gt2 = b // nb2, l // tt2
    assert r == MOE_ROWS and gb2 * gt2 * MOE_HALVES == n_tiles
    gw = EXPERTS_PER_GROUP * D_FF
    bm2 = pl.BlockSpec((nb2, tt2, D_MODEL), lambda i, j, g, *_: (i, j, 0))
    tm2 = pl.BlockSpec((tt2, D_MODEL), lambda i, j, g, *_: (j, i))
    flat2 = lambda w: pl.BlockSpec((MOE_HALVES * r, w), lambda i, j, g, *_: (i * gt2 + j, 0))
    in_specs = [flat2(D_MODEL), flat2(D_MODEL), flat2(LANES),
                pl.BlockSpec((nb2, 1, D_MODEL), lambda i, j, g, *_: (i, 0, 2)),
                pl.BlockSpec((None, D_MODEL, gw), lambda i, j, g, *_: (layer, 0, g)),
                pl.BlockSpec((None, D_MODEL, gw), lambda i, j, g, *_: (layer, 0, g)),
                pl.BlockSpec((None, gw, D_MODEL), lambda i, j, g, *_: (layer, g, 0))]
    args = [x1, h, meta, mod_ffn, w1, w3, w2]
    if final_norm:
        in_specs.append(_resident(nf))
        args.append(nf)
    if out_time_major:
        assert nb == 1
        out_spec, out_shape = tm2, jax.ShapeDtypeStruct((l, b * D_MODEL), F32)
    else:
        out_spec, out_shape = bm2, jax.ShapeDtypeStruct((b, l, D_MODEL), F32)
    return pl.pallas_call(
        functools.partial(_moe_expert_body, nb=nb2, tt=tt2, n_time_tiles=gt2, final_norm=final_norm),
        grid_spec=pltpu.PrefetchScalarGridSpec(
            num_scalar_prefetch=1,
            grid=(gb2, gt2, N_GROUPS),
            in_specs=in_specs,
            out_specs=out_spec,
            scratch_shapes=[pltpu.VMEM((MOE_HALVES, SORT_ROWS, D_MODEL), BF16),
                            pltpu.VMEM((MOE_HALVES, SORT_ROWS, LANES), F32),
                            pltpu.VMEM((MOE_HALVES, SORT_ROWS, D_MODEL), F32)]),
        out_shape=out_shape,
        compiler_params=_params("parallel", "parallel", "arbitrary"),
        name="moe_experts",
    )(cnt[:, 0, :N_GROUPS], *args)


def _s5_body(*refs, tt, nbs, has_state):
    if has_state:
        (x_ref, sh_ref, sc_ref, gm_ref, nw_ref, win_ref, wbr_ref, wbi_ref, wcr_ref, wci_ref, dsk_ref,
         wglu_ref, abr_ref, abi_ref, s0r_ref, s0i_ref, out_ref, sr_ref, si_ref, xr_s, xi_s) = refs
    else:
        (x_ref, sh_ref, sc_ref, gm_ref, nw_ref, win_ref, wbr_ref, wbi_ref, wcr_ref, wci_ref, dsk_ref,
         wglu_ref, abr_ref, abi_ref, out_ref, sr_ref, si_ref, xr_s, xi_s) = refs
    r = tt * nbs
    blocks = D_MODEL // MXU_DIM
    sw = SSM_STATE // blocks

    @pl.when(pl.program_id(1) == 0)
    def _():
        if has_state:
            sr_ref[...] = s0r_ref[...]
            si_ref[...] = s0i_ref[...]
        else:
            sr_ref[...] = jnp.zeros(sr_ref.shape, F32)
            si_ref[...] = jnp.zeros(si_ref.shape, F32)

    x = x_ref[...]
    h = (_rms(x, nw_ref[...]) * (1.0 + sc_ref[...]) + sh_ref[...]).reshape(r, D_MODEL)
    u = _mm(h, win_ref[...])
    ub = u.astype(BF16)
    for j in range(blocks):
        uj = ub[:, j * MXU_DIM:(j + 1) * MXU_DIM]
        xr_s[:, :, j * sw:(j + 1) * sw] = jnp.dot(
            uj, wbr_ref[j], preferred_element_type=F32).reshape(tt, nbs, sw)
        xi_s[:, :, j * sw:(j + 1) * sw] = jnp.dot(
            uj, wbi_ref[j], preferred_element_type=F32).reshape(tt, nbs, sw)

    for rg in range(nbs // SUBLANES):
        rows = slice(rg * SUBLANES, (rg + 1) * SUBLANES)
        for ch in range(SSM_STATE // SCAN_LANES):
            lanes = slice(ch * SCAN_LANES, (ch + 1) * SCAN_LANES)
            ar = jnp.broadcast_to(abr_ref[:, lanes], (SUBLANES, SCAN_LANES))
            ai = jnp.broadcast_to(abi_ref[:, lanes], (SUBLANES, SCAN_LANES))

            hr, hi = sr_ref[rows, lanes], si_ref[rows, lanes]
            for t in range(tt):
                hr, hi = (ar * hr - ai * hi + xr_s[t, rows, lanes],
                          ar * hi + ai * hr + xi_s[t, rows, lanes])
                xr_s[t, rows, lanes] = hr
                xi_s[t, rows, lanes] = hi
            sr_ref[rows, lanes] = hr
            si_ref[rows, lanes] = hi

    ys = []
    for j in range(blocks):
        st_r = xr_s[:, :, j * sw:(j + 1) * sw].reshape(r, sw)
        st_i = xi_s[:, :, j * sw:(j + 1) * sw].reshape(r, sw)
        ys.append(_mm(st_r, wcr_ref[j]) - _mm(st_i, wci_ref[j]))
    y = jnp.concatenate(ys, axis=1) + dsk_ref[...] * u
    ag = _mm(jax.nn.gelu(y), wglu_ref[...])
    mix = ag[:, :D_MODEL] * _sigmoid(ag[:, D_MODEL:])
    out_ref[...] = x + gm_ref[...] * mix.reshape(tt, nbs, D_MODEL)


def _s5(x, mod, nw, win, wbr, wbi, wcr, wci, dsk, wglu, abr, abi, s0r, s0i, *, tt, nbs):
    l, b, _ = x.shape
    has_state = s0r is not None
    xs = pl.BlockSpec((tt, nbs, D_MODEL), lambda i, j: (j, i, 0))
    mods = lambda col: pl.BlockSpec((nbs, D_MODEL), lambda i, j: (i, col))
    full = _resident
    st = pl.BlockSpec((nbs, SSM_STATE), lambda i, j: (i, 0))
    consts = [nw, win, wbr, wbi, wcr, wci, dsk, wglu, abr, abi]
    in_specs = [xs, mods(0), mods(1), mods(2)] + [full(a) for a in consts]
    args = [x, mod, mod, mod] + consts
    if has_state:
        in_specs += [st, st]
        args += [s0r, s0i]
    return pl.pallas_call(
        functools.partial(_s5_body, tt=tt, nbs=nbs, has_state=has_state),
        grid=(b // nbs, l // tt),
        in_specs=in_specs,
        out_specs=[xs, st, st],
        out_shape=[jax.ShapeDtypeStruct((l, b, D_MODEL), F32),
                   jax.ShapeDtypeStruct((b, SSM_STATE), F32),
                   jax.ShapeDtypeStruct((b, SSM_STATE), F32)],
        scratch_shapes=[pltpu.VMEM((tt, nbs, SSM_STATE), F32), pltpu.VMEM((tt, nbs, SSM_STATE), F32)],
        compiler_params=_params("parallel", "arbitrary"),
        name="s5",
    )(*args)


def _s5_discretize(lam_re, lam_im, log_dt, b_re, b_im, c_re, c_im):
    dt = jnp.exp(log_dt)[:, None]
    mag = jnp.exp(lam_re * dt)
    ang = lam_im * dt
    ab_re, ab_im = mag * jnp.cos(ang), mag * jnp.sin(ang)
    den = lam_re * lam_re + lam_im * lam_im
    f_re = ((ab_re - 1.0) * lam_re + ab_im * lam_im) / den
    f_im = (ab_im * lam_re - (ab_re - 1.0) * lam_im) / den
    bb_re = f_re[..., None] * b_re - f_im[..., None] * b_im
    bb_im = f_re[..., None] * b_im + f_im[..., None] * b_re
    blocks = D_MODEL // MXU_DIM
    gpb = SSM_GROUPS // blocks

    def diag_blocks(t, rows_per_group, cols_per_group):
        tiled = jnp.tile(t.reshape(blocks, gpb * rows_per_group, cols_per_group), (1, 1, gpb))
        rg = lax.broadcasted_iota(jnp.int32, tiled.shape, 1) // rows_per_group
        cg = lax.broadcasted_iota(jnp.int32, tiled.shape, 2) // cols_per_group
        return jnp.where(rg == cg, tiled, 0.0).astype(BF16)

    b_blocks = lambda bb: diag_blocks(jnp.swapaxes(bb, 1, 2), SSM_GROUP, SSM_P)
    c_blocks = lambda cc: diag_blocks(jnp.swapaxes(cc, 1, 2), SSM_P, SSM_GROUP)

    return (ab_re.reshape(1, SSM_STATE), ab_im.reshape(1, SSM_STATE),
            b_blocks(bb_re), b_blocks(bb_im), c_blocks(c_re), c_blocks(c_im))


def _trunk(x, mods_mix, mods_ffn, conv0, delta0, re0, im0, wts, *, nb_tok, tt_tok, nb_moe, tt_moe, nb_delta, tt_delta, chunk,
           tt_s5, nbs_s5):
    b, l, _ = x.shape
    mm3 = [m.reshape(b, 1, 3 * D_MODEL) for m in mods_mix]
    mf3 = [m.reshape(b, 1, 3 * D_MODEL) for m in mods_ffn]

    front = (x, mm3[0], mm3[0], wts['norm_mix'][0], wts['gdn_wq'], wts['gdn_wba'], wts['gdn_alog'], wts['gdn_dtb'])
    fused = nb_delta * tt_delta >= ROWS
    if not fused:
        front = _gdn_front(*front, nb=nb_tok, tt=tt_tok, chunk=chunk)
    o, s_delta, s_conv = _gdn_delta(front, wts['gdn_cw'], wts['gdn_onorm'], delta0, conv0, dims=(b, l),
                                    nb=nb_delta, tt=tt_delta, chunk=chunk, fused=fused)
    prompt_like = nb_tok == 1
    moe_w = lambda i: (wts['norm_ffn'][i], wts['moe_wr'][i], wts['moe_br'][i],
                       wts['moe_w1'], wts['moe_w3'], wts['moe_w2'])
    x1 = _moe(x, o, wts['gdn_wout'], mm3[0], mf3[0], *moe_w(0), None, layer=0, dims=(b, l), nb=nb_moe, tt=tt_moe,
              in_time_major=False, out_time_major=prompt_like)
    xt = x1.reshape(l, b, D_MODEL) if prompt_like else jnp.swapaxes(x1, 0, 1)
    x2, s_re, s_im = _s5(xt, mods_mix[1], wts['norm_mix'][1], wts['s5_win'], wts['s5_wbr'], wts['s5_wbi'],
                         wts['s5_wcr'], wts['s5_wci'], wts['s5_d'], wts['s5_wglu'], wts['s5_abr'], wts['s5_abi'],
                         re0, im0, tt=tt_s5, nbs=nbs_s5)
    x2 = x2.reshape(l, b * D_MODEL) if prompt_like else jnp.swapaxes(x2, 0, 1)
    y = _moe(x2, None, None, None, mf3[1], *moe_w(1), wts['norm_final'], layer=1, dims=(b, l), nb=nb_moe, tt=tt_moe,
             in_time_major=prompt_like, out_time_major=False)
    return (y, s_delta[None], s_conv[None],
            s_re.reshape(1, b, SSM_GROUPS, SSM_P), s_im.reshape(1, b, SSM_GROUPS, SSM_P))


def kernel(x_prompt, x_sample, state_delta, state_conv, state_ssm_re, state_ssm_im, c_prompt, c_sample, norm_mix, w_mod_mix, b_mod_mix, norm_ffn, w_mod_ffn, b_mod_ffn, norm_final, gdn_w_in, gdn_conv_w, gdn_a_log, gdn_dt_bias, gdn_o_norm, gdn_w_out, s5_w_in, s5_lam_re, s5_lam_im, s5_log_dt, s5_b_re, s5_b_im, s5_c_re, s5_c_im, s5_d, s5_w_glu, moe_w_rg, moe_b_rg, moe_w_re, moe_b_re, moe_w1, moe_w3, moe_w2):
    bp = x_prompt.shape[0]
    depth = norm_mix.shape[0]

    lane_pad = lambda a, lo: jnp.pad(a, [(0, 0)] * (a.ndim - 1) + [(lo, LANES - lo - a.shape[-1])])
    abr, abi, wbr, wbi, wcr, wci = _s5_discretize(s5_lam_re[0], s5_lam_im[0], s5_log_dt[0], s5_b_re[0],
                                                  s5_b_im[0], s5_c_re[0], s5_c_im[0])
    wts = dict(
        norm_mix=norm_mix.reshape(depth, 1, D_MODEL), norm_ffn=norm_ffn.reshape(depth, 1, D_MODEL),
        norm_final=norm_final.reshape(1, D_MODEL),
        gdn_wq=gdn_w_in[0, :, :CONV_DIM + QK_DIM].astype(BF16),
        gdn_wba=lane_pad(gdn_w_in[0, :, CONV_DIM + QK_DIM:], 0).astype(BF16),
        gdn_alog=lane_pad(gdn_a_log[0][None], N_HEADS), gdn_dtb=lane_pad(gdn_dt_bias[0][None], N_HEADS),
        gdn_cw=jnp.pad(gdn_conv_w[0], ((0, SUBLANES - CONV_TAPS), (0, 0))),
        gdn_onorm=gdn_o_norm[0][None], gdn_wout=gdn_w_out[0].astype(BF16),
        s5_win=s5_w_in[0].astype(BF16), s5_wbr=wbr, s5_wbi=wbi, s5_wcr=wcr, s5_wci=wci,
        s5_d=s5_d[0].reshape(1, D_MODEL), s5_wglu=s5_w_glu[0].astype(BF16), s5_abr=abr, s5_abi=abi,
        moe_wr=lane_pad(jnp.concatenate([moe_w_rg, moe_w_re], axis=-1), 0),
        moe_br=lane_pad(jnp.concatenate([moe_b_rg, moe_b_re], axis=-1), 0)[:, None, :],
        moe_w1=jnp.swapaxes(moe_w1, 1, 2).reshape(depth, D_MODEL, N_EXPERTS * D_FF).astype(BF16),
        moe_w3=jnp.swapaxes(moe_w3, 1, 2).reshape(depth, D_MODEL, N_EXPERTS * D_FF).astype(BF16),
        moe_w2=moe_w2.reshape(depth, N_EXPERTS * D_FF, D_MODEL).astype(BF16),
    )

    c_all = jnp.concatenate([c_prompt, c_sample], axis=0)
    m_mix = _ada_mod(c_all, w_mod_mix, b_mod_mix)
    m_ffn = _ada_mod(c_all, w_mod_ffn, b_mod_ffn)

    y_p, p_delta, p_conv, p_re, p_im = _trunk(
        x_prompt, [m_mix[i, :bp] for i in range(depth)], [m_ffn[i, :bp] for i in range(depth)],
        None, None, None, None, wts,
        nb_tok=1, tt_tok=ROWS, nb_moe=1, tt_moe=MOE_ROWS, nb_delta=1, tt_delta=ROWS, chunk=CHUNK, tt_s5=S5_ROWS // SUBLANES, nbs_s5=SUBLANES)
    ls = x_sample.shape[1]
    y_s, s_delta, s_conv, s_re, s_im = _trunk(
        x_sample, [m_mix[i, bp:] for i in range(depth)], [m_ffn[i, bp:] for i in range(depth)],
        state_conv[0], state_delta[0], state_ssm_re[0].reshape(-1, SSM_STATE),
        state_ssm_im[0].reshape(-1, SSM_STATE), wts,
        nb_tok=ROWS // ls, tt_tok=ls, nb_moe=MOE_ROWS // ls, tt_moe=ls, nb_delta=2 * MXU_DIM // (ls * N_HEADS), tt_delta=ls, chunk=ls,
        tt_s5=ls, nbs_s5=ROWS // ls)
    return (y_p, y_s, p_delta, p_conv, p_re, p_im, s_delta, s_conv, s_re, s_im)
```

```python
import functools

import jax
import jax.numpy as jnp
from jax import lax
from jax.experimental import pallas as pl
from jax.experimental.pallas import tpu as pltpu

F32 = jnp.float32
BF16 = jnp.bfloat16

D_MODEL = 1024
N_HEADS = 8
HEAD_DIM = 128
QK_DIM = N_HEADS * HEAD_DIM
CONV_DIM = 3 * QK_DIM
CONV_TAPS = 4
CONV_PAD = 16
CHUNK = 64
SSM_GROUPS = 64
SSM_GROUP = 16
SSM_P = 64
SSM_STATE = SSM_GROUPS * SSM_P
N_GROUPS = 4
EXPERTS_PER_GROUP = 4
N_EXPERTS = 16
D_FF = 256
EPS = 1e-6

LANES = 128
SUBLANES = 8
MXU_DIM = 256
ROWS = 256
MOE_ROWS = 512
S5_ROWS = 512
LOCKSTEP_STACKS = 4
SCAN_LANES = 512
VMEM_LIMIT = 56 * 1024 * 1024


def _mm(a, b):
    return jnp.dot(a.astype(BF16), b.astype(BF16), preferred_element_type=F32)


def _bmm(a, b):
    return lax.dot_general(a.astype(BF16), b.astype(BF16), (((2,), (1,)), ((0,), (0,))),
                           preferred_element_type=F32)


def _bmm_nt(a, b):
    return lax.dot_general(a.astype(BF16), b.astype(BF16), (((2,), (2,)), ((0,), (0,))),
                           preferred_element_type=F32)


def _split_bf16(x, terms):
    out = []
    for _ in range(terms - 1):
        p = x.astype(BF16)
        out.append(p)
        x = x - p.astype(F32)
    out.append(x.astype(BF16))
    return out


def _rms(x, w):
    return x * lax.rsqrt(jnp.mean(x * x, axis=-1, keepdims=True) + EPS) * w


def _sigmoid(x):
    return 0.5 * jnp.tanh(0.5 * x) + 0.5


def _silu(x):
    h = 0.5 * x
    return h * jnp.tanh(h) + h


def _softplus(x):
    return jnp.maximum(x, 0.0) + jnp.log1p(jnp.exp(-jnp.abs(x)))


def _lane_col(tile, lane):
    li = lax.broadcasted_iota(jnp.int32, tile.shape, 1)
    return jnp.sum(jnp.where(li == lane, tile, 0.0), axis=1, keepdims=True)


def _log2(n):
    assert n & (n - 1) == 0
    return n.bit_length() - 1


def _params(*sem):
    return pltpu.CompilerParams(dimension_semantics=sem, vmem_limit_bytes=VMEM_LIMIT)


def _resident(a):
    return pl.BlockSpec(a.shape, lambda *_: (0,) * a.ndim, pipeline_mode=pl.Buffered(1))


def _mod_body(c_ref, w_ref, b_ref, o_ref):
    o_ref[0] = _mm(_silu(c_ref[...]), w_ref[0]) + b_ref[0]


def _ada_mod(c_all, w_mod, b_mod):
    nl, nr, tn = w_mod.shape[0], c_all.shape[0], 768
    return pl.pallas_call(
        _mod_body,
        grid=(nl, 3 * D_MODEL // tn),
        in_specs=[pl.BlockSpec((nr, D_MODEL), lambda l, j: (0, 0)),
                  pl.BlockSpec((1, D_MODEL, tn), lambda l, j: (l, 0, j)),
                  pl.BlockSpec((1, 1, tn), lambda l, j: (l, 0, j))],
        out_specs=pl.BlockSpec((1, nr, tn), lambda l, j: (l, 0, j)),
        out_shape=jax.ShapeDtypeStruct((nl, nr, 3 * D_MODEL), F32),
        compiler_params=_params("parallel", "parallel"),
        name="ada_mod",
    )(c_all, w_mod, b_mod.reshape(nl, 1, 3 * D_MODEL))


def _gdn_front_h(x_ref, sh_ref, sc_ref, nw_ref):
    x = x_ref[...]
    h = _rms(x, nw_ref[...]) * (1.0 + sc_ref[...]) + sh_ref[...]
    return h.reshape(x.shape[0] * x.shape[1], D_MODEL).astype(BF16)


def _gdn_front_gates(hb, wba_ref, alog_ref, dtb_ref, chunk):
    r = hb.shape[0]
    ba = jnp.dot(hb, wba_ref[...], preferred_element_type=F32)
    beta = _sigmoid(ba)
    g = -jnp.exp(alog_ref[...]) * _softplus(ba + dtb_ref[...])
    ri = lax.broadcasted_iota(jnp.int32, (r, r), 0)
    ci = lax.broadcasted_iota(jnp.int32, (r, r), 1)
    same = (ri >> _log2(chunk)) == (ci >> _log2(chunk))
    low = jnp.where(same & (ri >= ci), 1.0, 0.0).astype(BF16)
    ones = jnp.where(same, 1.0, 0.0).astype(BF16)
    gc = jnp.zeros((r, LANES), F32)
    gl = jnp.zeros((r, LANES), F32)
    for piece in _split_bf16(g, 3):
        gc = gc + jnp.dot(low, piece, preferred_element_type=F32)
        gl = gl + jnp.dot(ones, piece, preferred_element_type=F32)
    return jnp.concatenate([beta, gc, gl], axis=-1)


def _gdn_front_body(x_ref, sh_ref, sc_ref, nw_ref, wq_ref, wba_ref, alog_ref, dtb_ref,
                    qkv_ref, z_ref, gt_ref, *, chunk):
    nb, tt, _ = x_ref.shape
    hb = _gdn_front_h(x_ref, sh_ref, sc_ref, nw_ref)
    pq = jnp.dot(hb, wq_ref[...], preferred_element_type=F32)
    qkv_ref[...] = pq[:, :CONV_DIM].reshape(nb, tt, CONV_DIM)
    z_ref[...] = pq[:, CONV_DIM:].reshape(nb, tt, QK_DIM)
    gt_ref[...] = _gdn_front_gates(hb, wba_ref, alog_ref, dtb_ref, chunk).reshape(nb, tt, 3 * LANES)


def _gdn_front(x, shift, scale, nw, wq, wba, alog, dtb, *, nb, tt, chunk):
    b, l, _ = x.shape
    tok = lambda w: pl.BlockSpec((nb, tt, w), lambda i, j: (i, j, 0))
    mod = lambda col: pl.BlockSpec((nb, 1, D_MODEL), lambda i, j: (i, 0, col))
    full = _resident
    return pl.pallas_call(
        functools.partial(_gdn_front_body, chunk=chunk),
        grid=(b // nb, l // tt),
        in_specs=[tok(D_MODEL), mod(0), mod(1), full(nw), full(wq), full(wba), full(alog), full(dtb)],
        out_specs=[tok(CONV_DIM), tok(QK_DIM), tok(3 * LANES)],
        out_shape=[jax.ShapeDtypeStruct((b, l, CONV_DIM), F32),
                   jax.ShapeDtypeStruct((b, l, QK_DIM), F32),
                   jax.ShapeDtypeStruct((b, l, 3 * LANES), F32)],
        compiler_params=_params("parallel", "parallel"),
        name="gdn_front",
    )(x, shift, scale, nw, wq, wba, alog, dtb)


def _inv_unit_lower(neg_a, ri, ci, chunk):
    base = min(16, chunk)
    same = lambda size: (ri >> _log2(size)) == (ci >> _log2(size))
    n = jnp.where(same(base), neg_a, 0.0)
    q, m, k = n, n, 1
    while 2 * k < base:
        m = _bmm(m, m)
        q = q + m + _bmm(q, m)
        k *= 2
    size = base
    while size < chunk:
        e = jnp.where(same(2 * size) & jnp.logical_not(same(size)), neg_a, 0.0)
        x = e + _bmm(q, e)
        q = q + (x + _bmm(x, q))
        size *= 2
    return q


def _gdn_delta_body(*refs, nb, tt, chunk, has_state, fused):
    it = iter(refs)
    take = lambda n: [next(it) for _ in range(n)]
    if fused:
        x_ref, sh_ref, sc_ref, nw_ref, wq_ref, wba_ref, alog_ref, dtb_ref = take(8)
    else:
        qkv_ref, z_ref, gt_ref = take(3)
    cw_ref, on_ref = take(2)
    if has_state:
        s0_ref, c0_ref = take(2)
    o_ref, s_ref, cn_ref, xc, qkv_s, o_s, u_s, wq_s, qk_s, kdt_s = take(10)
    if fused:
        z_ref, gt_ref = take(2)
        hb = _gdn_front_h(x_ref, sh_ref, sc_ref, nw_ref)
        pq = jnp.dot(hb, wq_ref[...], preferred_element_type=F32)
        z_ref[...] = pq[:, CONV_DIM:].reshape(nb, tt, QK_DIM)
        gt_ref[...] = _gdn_front_gates(hb, wba_ref, alog_ref, dtb_ref, chunk).reshape(nb, tt, 3 * LANES)
        qkv_in = pq[:, :CONV_DIM].reshape(nb, tt, CONV_DIM)
    else:
        qkv_in = qkv_ref[...]
    hist = CONV_TAPS - 1
    pad = CONV_PAD

    @pl.when(pl.program_id(1) == 0)
    def _():
        xc[:, 0:pad - hist, :] = jnp.zeros((nb, pad - hist, CONV_DIM), F32)
        if has_state:
            s_ref[...] = s0_ref[...]
            xc[:, pad - hist:pad, :] = c0_ref[...]
        else:
            s_ref[...] = jnp.zeros(s_ref.shape, F32)
            xc[:, pad - hist:pad, :] = jnp.zeros((nb, hist, CONV_DIM), F32)

    xc[:, pad:pad + tt, :] = qkv_in
    cw = cw_ref[...]
    lead = SUBLANES

    def conv_rows(r0, r1):
        x_cur = xc[:, pad + r0 - lead:pad + r1, :]
        x_prev = xc[:, pad + r0 - lead - 1:pad + r1 - 1, :]
        z = x_cur * cw[1:2] + x_prev * cw[0:1]
        y = x_cur[:, lead:] * cw[3:4] + x_prev[:, lead:] * cw[2:3] + z[:, lead - 2:lead - 2 + r1 - r0]
        qkv_s[:, r0:r1, :] = _silu(y)
        for hd in range(N_HEADS):
            for part, scl in ((0, HEAD_DIM ** -0.5), (1, 1.0)):
                lanes = slice(part * QK_DIM + hd * HEAD_DIM, part * QK_DIM + (hd + 1) * HEAD_DIM)
                xh = qkv_s[:, r0:r1, lanes]
                xh = xh * (lax.rsqrt(jnp.sum(xh * xh, axis=-1, keepdims=True) + EPS) * scl)
                qkv_s[:, r0:r1, lanes] = xh


    per_stack = MXU_DIM // chunk
    units_all = [(s, hd) for s in range(nb) for hd in range(N_HEADS)]
    assert len(units_all) % per_stack == 0
    stacks = [units_all[i:i + per_stack] for i in range(0, len(units_all), per_stack)]
    r = MXU_DIM
    ri = lax.broadcasted_iota(jnp.int32, (r, r), 0)
    ci = lax.broadcasted_iota(jnp.int32, (r, r), 1)
    same = (ri >> _log2(chunk)) == (ci >> _log2(chunk))
    tril = same & (ri >= ci)
    strict = same & (ri > ci)
    eye = ri == ci

    n_chunks = tt // chunk
    dot = lambda p, q: jnp.dot(p, q, preferred_element_type=F32)

    pairs = [(c, units) for c in range(n_chunks) for units in stacks]
    rows_done = 0
    for first in range(0, len(pairs), LOCKSTEP_STACKS):
        group = pairs[first:first + LOCKSTEP_STACKS]
        ns = len(group)
        out = slice(first, first + ns)
        rows_needed = (group[-1][0] + 1) * chunk
        if rows_needed > rows_done:
            conv_rows(rows_done, rows_needed)
            rows_done = rows_needed

        def gather(f):
            return jnp.stack([jnp.concatenate([f(s, hd, slice(c * chunk, (c + 1) * chunk)) for s, hd in units], axis=0)
                              for c, units in group], axis=0)

        qn = gather(lambda s, hd, rows: qkv_s[s, rows, hd * HEAD_DIM:(hd + 1) * HEAD_DIM])
        kn = gather(lambda s, hd, rows: qkv_s[s, rows, QK_DIM + hd * HEAD_DIM:QK_DIM + (hd + 1) * HEAD_DIM])
        v = gather(lambda s, hd, rows: qkv_s[s, rows, 2 * QK_DIM + hd * HEAD_DIM:2 * QK_DIM + (hd + 1) * HEAD_DIM])
        beta = gather(lambda s, hd, rows: _lane_col(gt_ref[s, rows, 0:LANES], hd))
        gc = gather(lambda s, hd, rows: _lane_col(gt_ref[s, rows, LANES:2 * LANES], N_HEADS + hd))
        gl = gather(lambda s, hd, rows: _lane_col(gt_ref[s, rows, 2 * LANES:3 * LANES], N_HEADS + hd))
        eg = jnp.exp(gc)
        kb = kn * beta
        gc_row = jnp.sum(jnp.where(eye, gc, 0.0), axis=1, keepdims=True)
        diff = gc - gc_row
        decay = jnp.exp(jnp.where(tril, diff, -jnp.inf))
        decay_strict = jnp.exp(jnp.where(strict, diff, -jnp.inf))
        q_inv = _inv_unit_lower(_bmm_nt(-kb, kn) * decay_strict, ri, ci, chunk)
        rhs = jnp.concatenate([v * beta, kb * eg], axis=2)
        uw = rhs + _bmm(q_inv, rhs)
        u_s[out] = uw[:, :, :HEAD_DIM]
        per_unit = lambda t: t.reshape(ns, per_stack, chunk, HEAD_DIM)
        wq_s[out] = jnp.concatenate([per_unit(uw[:, :, HEAD_DIM:]), per_unit(qn * eg)], axis=2).reshape(
            ns, 2 * r, HEAD_DIM).astype(BF16)
        qk_s[out] = (_bmm_nt(qn, kn) * decay).astype(BF16)
        kdt_s[out] = jnp.swapaxes(kn * jnp.exp(gl - gc), 1, 2).astype(BF16)

        _gdn_sequential(first, group, chunk, gt_ref, s_ref, o_s, u_s, wq_s, qk_s, kdt_s)

    last = xc[:, pad + tt - hist:pad + tt, :]
    cn_ref[...] = last
    xc[:, pad - hist:pad, :] = last

    on = on_ref[...]
    for hd in range(N_HEADS):
        sl = slice(hd * HEAD_DIM, (hd + 1) * HEAD_DIM)
        o_ref[:, :, sl] = _rms(o_s[:, :, sl], on) * _silu(z_ref[:, :, sl])


def _gdn_sequential(first, group, chunk, gt_ref, s_ref, o_s, u_s, wq_s, qk_s, kdt_s):
    dot = lambda p, q: jnp.dot(p, q, preferred_element_type=F32)
    unit_of_row = lax.broadcasted_iota(jnp.int32, (MXU_DIM, HEAD_DIM), 0) >> _log2(chunk)
    for idx, (c, units) in enumerate(group, start=first):
        rows = slice(c * chunk, (c + 1) * chunk)
        ws, qs = [], []
        for i, (s, hd) in enumerate(units):
            both = dot(wq_s[idx, 2 * i * chunk:2 * (i + 1) * chunk, :], s_ref[s, hd].astype(BF16))
            ws.append(both[:chunk])
            qs.append(both[chunk:])
        v_new = (u_s[idx] - jnp.concatenate(ws, axis=0)).astype(BF16)
        o = jnp.concatenate(qs, axis=0) + dot(qk_s[idx], v_new)
        kdt = kdt_s[idx]
        for i, (s, hd) in enumerate(units):
            o_s[s, rows, hd * HEAD_DIM:(hd + 1) * HEAD_DIM] = o[i * chunk:(i + 1) * chunk]
            g_last = jnp.exp(_lane_col(gt_ref[s, c * chunk:c * chunk + 1, 2 * LANES:3 * LANES], N_HEADS + hd))
            upd = dot(kdt, jnp.where(unit_of_row == i, v_new, jnp.zeros_like(v_new)))
            s_ref[s, hd] = s_ref[s, hd] * g_last + upd


def _gdn_delta(front, cw, onorm, s0, c0, *, dims, nb, tt, chunk, fused):
    b, l = dims
    has_state = s0 is not None
    n_stacks = nb * tt * N_HEADS // MXU_DIM
    tok = lambda w: pl.BlockSpec((nb, tt, w), lambda i, j: (i, j, 0))
    mod = lambda col: pl.BlockSpec((nb, 1, D_MODEL), lambda i, j: (i, 0, col))
    full = _resident
    st_spec = pl.BlockSpec((nb, N_HEADS, HEAD_DIM, HEAD_DIM), lambda i, j: (i, 0, 0, 0))
    cv_spec = pl.BlockSpec((nb, CONV_TAPS - 1, CONV_DIM), lambda i, j: (i, 0, 0))
    if fused:
        in_specs = [tok(D_MODEL), mod(0), mod(1)] + [full(a) for a in front[3:]]
    else:
        in_specs = [tok(CONV_DIM), tok(QK_DIM), tok(3 * LANES)]
    in_specs += [full(cw), full(onorm)]
    args = list(front) + [cw, onorm]
    if has_state:
        in_specs += [st_spec, cv_spec]
        args += [s0, c0]
    scratch = [pltpu.VMEM((nb, tt + CONV_PAD, CONV_DIM), F32),
               pltpu.VMEM((nb, tt, CONV_DIM), F32),
               pltpu.VMEM((nb, tt, QK_DIM), F32),
               pltpu.VMEM((n_stacks, MXU_DIM, HEAD_DIM), F32),
               pltpu.VMEM((n_stacks, 2 * MXU_DIM, HEAD_DIM), BF16),
               pltpu.VMEM((n_stacks, MXU_DIM, MXU_DIM), BF16),
               pltpu.VMEM((n_stacks, HEAD_DIM, MXU_DIM), BF16)]
    if fused:
        scratch += [pltpu.VMEM((nb, tt, QK_DIM), F32), pltpu.VMEM((nb, tt, 3 * LANES), F32)]
    return pl.pallas_call(
        functools.partial(_gdn_delta_body, nb=nb, tt=tt, chunk=chunk, has_state=has_state, fused=fused),
        grid=(b // nb, l // tt),
        in_specs=in_specs,
        out_specs=[tok(QK_DIM), st_spec, cv_spec],
        out_shape=[jax.ShapeDtypeStruct((b, l, QK_DIM), F32),
                   jax.ShapeDtypeStruct((b, N_HEADS, HEAD_DIM, HEAD_DIM), F32),
                   jax.ShapeDtypeStruct((b, CONV_TAPS - 1, CONV_DIM), F32)],
        scratch_shapes=scratch,
        compiler_params=_params("parallel", "arbitrary"),
        name="gdn_delta",
    )(*args)


def _route(logits):
    lane = lax.broadcasted_iota(jnp.int32, logits.shape, 1)
    neg, big = -1e30, 1 << 20
    rmax = lambda t: jnp.max(t, axis=1, keepdims=True)
    rsum = lambda t: jnp.sum(t, axis=1, keepdims=True)
    first = lambda cond: jnp.min(jnp.where(cond, lane, big), axis=1, keepdims=True)
    is_g = lane < N_GROUPS
    lg = jnp.where(is_g, logits, neg)
    mg = rmax(lg)
    g_idx = first(is_g & (lg >= mg))
    p_group = 1.0 / rsum(jnp.where(is_g, jnp.exp(lg - mg), 0.0))
    lo = N_GROUPS + EXPERTS_PER_GROUP * g_idx
    in_g = (lane >= lo) & (lane < lo + EXPERTS_PER_GROUP)
    le = jnp.where(in_g, logits, neg)
    m1 = rmax(le)
    se = rsum(jnp.where(in_g, jnp.exp(le - m1), 0.0))
    i1 = first(in_g & (le >= m1))
    le2 = jnp.where(lane == i1, neg, le)
    m2 = rmax(le2)
    i2 = first(in_g & (lane != i1) & (le2 >= m2))
    p1 = 1.0 / se
    p2 = jnp.exp(m2 - m1) / se
    tot = p1 + p2
    local = (jnp.where(lane == i1 - lo, p1 / tot * p_group, 0.0)
             + jnp.where(lane == i2 - lo, p2 / tot * p_group, 0.0))
    return g_idx, local


META_GROUP, META_RANK = 4, 5
PIECE_LANES = 8
SORT_PAD = 16
EXPERT_ROWS = 256
EXPERT_STEP = 32
SORT_ROWS = MOE_ROWS + N_GROUPS * SORT_PAD + EXPERT_ROWS
USED_ROWS = MOE_ROWS + LANES
MOE_HALVES = 2


def _moe_route_body(*refs, nb, tt, has_proj):
    it = iter(refs)
    x_ref = next(it)
    if has_proj:
        y_ref, wo_ref, gm_ref = next(it), next(it), next(it)
    sh_ref, sc_ref, nw_ref, wr_ref, br_ref, x1_ref, h_ref, meta_ref, cnt_ref = (next(it) for _ in range(9))
    r = nb * tt
    dot = lambda p, q: jnp.dot(p, q, preferred_element_type=F32)

    x = x_ref[...].reshape(nb, tt, D_MODEL)
    if has_proj:
        x = x + gm_ref[...] * _mm(y_ref[...].reshape(r, D_MODEL), wo_ref[...]).reshape(nb, tt, D_MODEL)
    h = (_rms(x, nw_ref[...]) * (1.0 + sc_ref[...]) + sh_ref[...]).reshape(r, D_MODEL)
    x1_ref[...] = x.reshape(r, D_MODEL)
    h_hi, h_lo = _split_bf16(h, 2)
    h_ref[...] = h_hi
    w_hi, w_lo = _split_bf16(wr_ref[...], 2)
    logits = dot(h_hi, w_hi) + (dot(h_hi, w_lo) + dot(h_lo, w_hi)) + br_ref[...]
    g_idx, local = _route(logits)

    lane = lax.broadcasted_iota(jnp.int32, (r, LANES), 1)
    onehot = jnp.where(lane == g_idx, 1.0, 0.0)
    ri = lax.broadcasted_iota(jnp.int32, (r, r), 0)
    ci = lax.broadcasted_iota(jnp.int32, (r, r), 1)
    earlier = jnp.where(ri > ci, 1.0, 0.0).astype(BF16)
    rank = jnp.sum(onehot * dot(earlier, onehot.astype(BF16)), axis=1, keepdims=True)
    meta_ref[...] = jnp.where(lane == META_GROUP, g_idx.astype(F32),
                              jnp.where(lane == META_RANK, rank, local))
    counts = jnp.sum(onehot, axis=0, keepdims=True).astype(jnp.int32)
    cnt_ref[...] = jnp.broadcast_to(counts, (1, SUBLANES, LANES))


def _moe_expert_body(cnt_ref, *refs, nb, tt, n_time_tiles, final_norm):
    it = iter(refs)
    x1_ref, h_ref, meta_ref, gf_ref, w1_ref, w3_ref, w2_ref = (next(it) for _ in range(7))
    if final_norm:
        nf_ref = next(it)
    out_ref, hs_s, ms_s, acc_s = (next(it) for _ in range(4))
    r = MOE_ROWS
    dot = lambda p, q: jnp.dot(p, q, preferred_element_type=F32)
    step = pl.program_id(0) * n_time_tiles + pl.program_id(1)
    grp = pl.program_id(2)

    def layout(half):
        counts = [cnt_ref[MOE_HALVES * step + half, g] for g in range(N_GROUPS)]
        starts, nxt = [], 0
        for g in range(N_GROUPS):
            starts.append(nxt)
            nxt = nxt + (((counts[g] + (SORT_PAD - 1)) >> _log2(SORT_PAD)) << _log2(SORT_PAD))
        return counts, starts

    def positions(half, starts):
        meta = meta_ref[half * r:(half + 1) * r, :]
        g_col = _lane_col(meta, META_GROUP)
        pos = _lane_col(meta, META_RANK)
        for g in range(N_GROUPS):
            pos = pos + jnp.where(g_col == g, jnp.asarray(starts[g], jnp.int32).astype(F32), 0.0)
        return meta, pos

    @pl.when(grp == 0)
    def _():
        for half in range(MOE_HALVES):
            meta, pos = positions(half, layout(half)[1])
            pos_row = jnp.broadcast_to(pos, (r, LANES)).T[0:1, :].astype(jnp.int32)
            perm = jnp.where(lax.broadcasted_iota(jnp.int32, (SORT_ROWS, r), 0) == pos_row, 1.0, 0.0).astype(BF16)
            hs_s[half] = dot(perm, h_ref[half * r:(half + 1) * r, :]).astype(BF16)
            lane = lax.broadcasted_iota(jnp.int32, (r, LANES), 1)
            pieces = _split_bf16(jnp.where(lane < EXPERTS_PER_GROUP, meta, 0.0), 3)
            packed = pieces[0].astype(F32)
            for k in (1, 2):
                packed = packed + pltpu.roll(pieces[k].astype(F32), k * PIECE_LANES, 1)
            ms = dot(perm, packed.astype(BF16))
            ms_s[half] = ms + pltpu.roll(ms, LANES - PIECE_LANES, 1) + pltpu.roll(ms, LANES - 2 * PIECE_LANES, 1)
        acc_s[...] = jnp.zeros(acc_s.shape, F32)

    pick = lambda vals: functools.reduce(lambda acc, gv: jnp.where(grp == gv[0], gv[1], acc),
                                         list(enumerate(vals))[1:], jnp.asarray(vals[0], jnp.int32))
    lay = [layout(half) for half in range(MOE_HALVES)]
    cnt_g = [pick(c) for c, _ in lay]
    start_g = [pick(s) for _, s in lay]
    def block(first, size):
        base = [jnp.minimum(s + first, SORT_ROWS - size) for s in start_g]
        rows = [pl.ds(pl.multiple_of(bs, SORT_PAD), size) for bs in base]
        iota = lax.broadcasted_iota(jnp.int32, (size, 1), 0)
        hb = jnp.concatenate([hs_s[half, rows[half], :] for half in range(MOE_HALVES)], axis=0)
        mb = []
        for half in range(MOE_HALVES):
            seg_row = iota + (base[half] - start_g[half])
            mb.append(jnp.where((seg_row >= first) & (seg_row < cnt_g[half]), ms_s[half, rows[half], :], 0.0))
        mb = jnp.concatenate(mb, axis=0)
        hid = _silu(dot(hb, w1_ref[...])) * dot(hb, w3_ref[...])
        parts = [(hid[:, j * D_FF:(j + 1) * D_FF] * _lane_col(mb, j)).astype(BF16)
                 for j in range(EXPERTS_PER_GROUP)]
        res = dot(jnp.concatenate(parts, axis=1), w2_ref[...])
        for half in range(MOE_HALVES):
            acc_s[half, rows[half], :] += res[half * size:(half + 1) * size]

    longest = functools.reduce(jnp.maximum, cnt_g)
    n_full = longest // EXPERT_ROWS

    def full_block(k, carry):
        block(k * EXPERT_ROWS, EXPERT_ROWS)
        return carry

    lax.fori_loop(0, n_full, full_block, 0)
    tail = longest - n_full * EXPERT_ROWS
    for size in range(EXPERT_STEP, EXPERT_ROWS + 1, EXPERT_STEP):
        pl.when((tail > size - EXPERT_STEP) & (tail <= size))(
            functools.partial(block, n_full * EXPERT_ROWS, size))

    @pl.when(grp == N_GROUPS - 1)
    def _():
        moes = []
        for half in range(MOE_HALVES):
            _, pos = positions(half, layout(half)[1])
            unperm = jnp.where(lax.broadcasted_iota(jnp.int32, (r, USED_ROWS), 1) == pos.astype(jnp.int32),
                               1.0, 0.0).astype(BF16)
            a_hi, a_lo = _split_bf16(acc_s[half, 0:USED_ROWS, :], 2)
            moes.append(dot(unperm, a_hi) + dot(unperm, a_lo))
        moe = jnp.concatenate(moes, axis=0)
        out = x1_ref[...].reshape(nb, tt, D_MODEL) + gf_ref[...] * moe.reshape(nb, tt, D_MODEL)
        if final_norm:
            out = _rms(out, nf_ref[...])
        out_ref[...] = out.reshape(out_ref.shape)


def _moe(x, y, wo, mod_mix, mod_ffn, nw, wr, br, w1, w3, w2, nf, *, layer, dims, nb, tt,
         in_time_major, out_time_major):
    b, l = dims
    has_proj, final_norm = y is not None, nf is not None
    gb, gt = b // nb, l // tt
    r, n_tiles = nb * tt, gb * gt
    bm = pl.BlockSpec((nb, tt, D_MODEL), lambda i, j, *_: (i, j, 0))
    tm = pl.BlockSpec((tt, D_MODEL), lambda i, j, *_: (j, i))
    mod = lambda col: pl.BlockSpec((nb, 1, D_MODEL), lambda i, j, *_: (i, 0, col))
    flat = lambda w: pl.BlockSpec((r, w), lambda i, j, *_: (i * gt + j, 0))
    x_spec = tm if in_time_major else bm
    in_specs, args = [x_spec], [x]
    if has_proj:
        in_specs += [x_spec, _resident(wo), mod(2)]
        args += [y, wo, mod_mix]
    in_specs += [mod(0), mod(1)] + [_resident(a) for a in (nw, wr, br)]
    args += [mod_ffn, mod_ffn, nw, wr, br]
    x1, h, meta, cnt = pl.pallas_call(
        functools.partial(_moe_route_body, nb=nb, tt=tt, has_proj=has_proj),
        grid=(gb, gt),
        in_specs=in_specs,
        out_specs=[flat(D_MODEL), flat(D_MODEL), flat(LANES),
                   pl.BlockSpec((1, SUBLANES, LANES), lambda i, j: (i * gt + j, 0, 0))],
        out_shape=[jax.ShapeDtypeStruct((n_tiles * r, D_MODEL), F32),
                   jax.ShapeDtypeStruct((n_tiles * r, D_MODEL), BF16),
                   jax.ShapeDtypeStruct((n_tiles * r, LANES), F32),
                   jax.ShapeDtypeStruct((n_tiles, SUBLANES, LANES), jnp.int32)],
        compiler_params=_params("parallel", "parallel"),
        name="moe_route",
    )(*args)

    nb2, tt2 = (nb, MOE_HALVES * tt) if nb == 1 else (MOE_HALVES * nb, tt)
    gb2, gt2 = b // nb2, l // tt2
    assert r == MOE_ROWS and gb2 * gt2 * MOE_HALVES == n_tiles
    gw = EXPERTS_PER_GROUP * D_FF
    bm2 = pl.BlockSpec((nb2, tt2, D_MODEL), lambda i, j, g, *_: (i, j, 0))
    tm2 = pl.BlockSpec((tt2, D_MODEL), lambda i, j, g, *_: (j, i))
    flat2 = lambda w: pl.BlockSpec((MOE_HALVES * r, w), lambda i, j, g, *_: (i * gt2 + j, 0))
    in_specs = [flat2(D_MODEL), flat2(D_MODEL), flat2(LANES),
                pl.BlockSpec((nb2, 1, D_MODEL), lambda i, j, g, *_: (i, 0, 2)),
                pl.BlockSpec((None, D_MODEL, gw), lambda i, j, g, *_: (layer, 0, g)),
                pl.BlockSpec((None, D_MODEL, gw), lambda i, j, g, *_: (layer, 0, g)),
                pl.BlockSpec((None, gw, D_MODEL), lambda i, j, g, *_: (layer, g, 0))]
    args = [x1, h, meta, mod_ffn, w1, w3, w2]
    if final_norm:
        in_specs.append(_resident(nf))
        args.append(nf)
    if out_time_major:
        assert nb == 1
        out_spec, out_shape = tm2, jax.ShapeDtypeStruct((l, b * D_MODEL), F32)
    else:
        out_spec, out_shape = bm2, jax.ShapeDtypeStruct((b, l, D_MODEL), F32)
    return pl.pallas_call(
        functools.partial(_moe_expert_body, nb=nb2, tt=tt2, n_time_tiles=gt2, final_norm=final_norm),
        grid_spec=pltpu.PrefetchScalarGridSpec(
            num_scalar_prefetch=1,
            grid=(gb2, gt2, N_GROUPS),
            in_specs=in_specs,
            out_specs=out_spec,
            scratch_shapes=[pltpu.VMEM((MOE_HALVES, SORT_ROWS, D_MODEL), BF16),
                            pltpu.VMEM((MOE_HALVES, SORT_ROWS, LANES), F32),
                            pltpu.VMEM((MOE_HALVES, SORT_ROWS, D_MODEL), F32)]),
        out_shape=out_shape,
        compiler_params=_params("parallel", "parallel", "arbitrary"),
        name="moe_experts",
    )(cnt[:, 0, :N_GROUPS], *args)


def _s5_body(*refs, tt, nbs, has_state):
    if has_state:
        (x_ref, sh_ref, sc_ref, gm_ref, nw_ref, win_ref, wbr_ref, wbi_ref, wcr_ref, wci_ref, dsk_ref,
         wglu_ref, abr_ref, abi_ref, s0r_ref, s0i_ref, out_ref, sr_ref, si_ref, xr_s, xi_s) = refs
    else:
        (x_ref, sh_ref, sc_ref, gm_ref, nw_ref, win_ref, wbr_ref, wbi_ref, wcr_ref, wci_ref, dsk_ref,
         wglu_ref, abr_ref, abi_ref, out_ref, sr_ref, si_ref, xr_s, xi_s) = refs
    r = tt * nbs
    blocks = D_MODEL // MXU_DIM
    sw = SSM_STATE // blocks

    @pl.when(pl.program_id(1) == 0)
    def _():
        if has_state:
            sr_ref[...] = s0r_ref[...]
            si_ref[...] = s0i_ref[...]
        else:
            sr_ref[...] = jnp.zeros(sr_ref.shape, F32)
            si_ref[...] = jnp.zeros(si_ref.shape, F32)

    x = x_ref[...]
    h = (_rms(x, nw_ref[...]) * (1.0 + sc_ref[...]) + sh_ref[...]).reshape(r, D_MODEL)
    u = _mm(h, win_ref[...])
    ub = u.astype(BF16)
    for j in range(blocks):
        uj = ub[:, j * MXU_DIM:(j + 1) * MXU_DIM]
        xr_s[:, :, j * sw:(j + 1) * sw] = jnp.dot(
            uj, wbr_ref[j], preferred_element_type=F32).reshape(tt, nbs, sw)
        xi_s[:, :, j * sw:(j + 1) * sw] = jnp.dot(
            uj, wbi_ref[j], preferred_element_type=F32).reshape(tt, nbs, sw)

    for rg in range(nbs // SUBLANES):
        rows = slice(rg * SUBLANES, (rg + 1) * SUBLANES)
        for ch in range(SSM_STATE // SCAN_LANES):
            lanes = slice(ch * SCAN_LANES, (ch + 1) * SCAN_LANES)
            ar = jnp.broadcast_to(abr_ref[:, lanes], (SUBLANES, SCAN_LANES))
            ai = jnp.broadcast_to(abi_ref[:, lanes], (SUBLANES, SCAN_LANES))

            hr, hi = sr_ref[rows, lanes], si_ref[rows, lanes]
            for t in range(tt):
                hr, hi = (ar * hr - ai * hi + xr_s[t, rows, lanes],
                          ar * hi + ai * hr + xi_s[t, rows, lanes])
                xr_s[t, rows, lanes] = hr
                xi_s[t, rows, lanes] = hi
            sr_ref[rows, lanes] = hr
            si_ref[rows, lanes] = hi

    ys = []
    for j in range(blocks):
        st_r = xr_s[:, :, j * sw:(j + 1) * sw].reshape(r, sw)
        st_i = xi_s[:, :, j * sw:(j + 1) * sw].reshape(r, sw)
        ys.append(_mm(st_r, wcr_ref[j]) - _mm(st_i, wci_ref[j]))
    y = jnp.concatenate(ys, axis=1) + dsk_ref[...] * u
    ag = _mm(jax.nn.gelu(y), wglu_ref[...])
    mix = ag[:, :D_MODEL] * _sigmoid(ag[:, D_MODEL:])
    out_ref[...] = x + gm_ref[...] * mix.reshape(tt, nbs, D_MODEL)


def _s5(x, mod, nw, win, wbr, wbi, wcr, wci, dsk, wglu, abr, abi, s0r, s0i, *, tt, nbs):
    l, b, _ = x.shape
    has_state = s0r is not None
    xs = pl.BlockSpec((tt, nbs, D_MODEL), lambda i, j: (j, i, 0))
    mods = lambda col: pl.BlockSpec((nbs, D_MODEL), lambda i, j: (i, col))
    full = _resident
    st = pl.BlockSpec((nbs, SSM_STATE), lambda i, j: (i, 0))
    consts = [nw, win, wbr, wbi, wcr, wci, dsk, wglu, abr, abi]
    in_specs = [xs, mods(0), mods(1), mods(2)] + [full(a) for a in consts]
    args = [x, mod, mod, mod] + consts
    if has_state:
        in_specs += [st, st]
        args += [s0r, s0i]
    return pl.pallas_call(
        functools.partial(_s5_body, tt=tt, nbs=nbs, has_state=has_state),
        grid=(b // nbs, l // tt),
        in_specs=in_specs,
        out_specs=[xs, st, st],
        out_shape=[jax.ShapeDtypeStruct((l, b, D_MODEL), F32),
                   jax.ShapeDtypeStruct((b, SSM_STATE), F32),
                   jax.ShapeDtypeStruct((b, SSM_STATE), F32)],
        scratch_shapes=[pltpu.VMEM((tt, nbs, SSM_STATE), F32), pltpu.VMEM((tt, nbs, SSM_STATE), F32)],
        compiler_params=_params("parallel", "arbitrary"),
        name="s5",
    )(*args)


def _s5_discretize(lam_re, lam_im, log_dt, b_re, b_im, c_re, c_im):
    dt = jnp.exp(log_dt)[:, None]
    mag = jnp.exp(lam_re * dt)
    ang = lam_im * dt
    ab_re, ab_im = mag * jnp.cos(ang), mag * jnp.sin(ang)
    den = lam_re * lam_re + lam_im * lam_im
    f_re = ((ab_re - 1.0) * lam_re + ab_im * lam_im) / den
    f_im = (ab_im * lam_re - (ab_re - 1.0) * lam_im) / den
    bb_re = f_re[..., None] * b_re - f_im[..., None] * b_im
    bb_im = f_re[..., None] * b_im + f_im[..., None] * b_re
    blocks = D_MODEL // MXU_DIM
    gpb = SSM_GROUPS // blocks

    def diag_blocks(t, rows_per_group, cols_per_group):
        tiled = jnp.tile(t.reshape(blocks, gpb * rows_per_group, cols_per_group), (1, 1, gpb))
        rg = lax.broadcasted_iota(jnp.int32, tiled.shape, 1) // rows_per_group
        cg = lax.broadcasted_iota(jnp.int32, tiled.shape, 2) // cols_per_group
        return jnp.where(rg == cg, tiled, 0.0).astype(BF16)

    b_blocks = lambda bb: diag_blocks(jnp.swapaxes(bb, 1, 2), SSM_GROUP, SSM_P)
    c_blocks = lambda cc: diag_blocks(jnp.swapaxes(cc, 1, 2), SSM_P, SSM_GROUP)

    return (ab_re.reshape(1, SSM_STATE), ab_im.reshape(1, SSM_STATE),
            b_blocks(bb_re), b_blocks(bb_im), c_blocks(c_re), c_blocks(c_im))


def _trunk(x, mods_mix, mods_ffn, conv0, delta0, re0, im0, wts, *, nb_tok, tt_tok, nb_moe, tt_moe, nb_delta, tt_delta, chunk,
           tt_s5, nbs_s5):
    b, l, _ = x.shape
    mm3 = [m.reshape(b, 1, 3 * D_MODEL) for m in mods_mix]
    mf3 = [m.reshape(b, 1, 3 * D_MODEL) for m in mods_ffn]

    front = (x, mm3[0], mm3[0], wts['norm_mix'][0], wts['gdn_wq'], wts['gdn_wba'], wts['gdn_alog'], wts['gdn_dtb'])
    fused = nb_delta * tt_delta >= ROWS
    if not fused:
        front = _gdn_front(*front, nb=nb_tok, tt=tt_tok, chunk=chunk)
    o, s_delta, s_conv = _gdn_delta(front, wts['gdn_cw'], wts['gdn_onorm'], delta0, conv0, dims=(b, l),
                                    nb=nb_delta, tt=tt_delta, chunk=chunk, fused=fused)
    prompt_like = nb_tok == 1
    moe_w = lambda i: (wts['norm_ffn'][i], wts['moe_wr'][i], wts['moe_br'][i],
                       wts['moe_w1'], wts['moe_w3'], wts['moe_w2'])
    x1 = _moe(x, o, wts['gdn_wout'], mm3[0], mf3[0], *moe_w(0), None, layer=0, dims=(b, l), nb=nb_moe, tt=tt_moe,
              in_time_major=False, out_time_major=prompt_like)
    xt = x1.reshape(l, b, D_MODEL) if prompt_like else jnp.swapaxes(x1, 0, 1)
    x2, s_re, s_im = _s5(xt, mods_mix[1], wts['norm_mix'][1], wts['s5_win'], wts['s5_wbr'], wts['s5_wbi'],
                         wts['s5_wcr'], wts['s5_wci'], wts['s5_d'], wts['s5_wglu'], wts['s5_abr'], wts['s5_abi'],
                         re0, im0, tt=tt_s5, nbs=nbs_s5)
    x2 = x2.reshape(l, b * D_MODEL) if prompt_like else jnp.swapaxes(x2, 0, 1)
    y = _moe(x2, None, None, None, mf3[1], *moe_w(1), wts['norm_final'], layer=1, dims=(b, l), nb=nb_moe, tt=tt_moe,
             in_time_major=prompt_like, out_time_major=False)
    return (y, s_delta[None], s_conv[None],
            s_re.reshape(1, b, SSM_GROUPS, SSM_P), s_im.reshape(1, b, SSM_GROUPS, SSM_P))


def kernel(x_prompt, x_sample, state_delta, state_conv, state_ssm_re, state_ssm_im, c_prompt, c_sample, norm_mix, w_mod_mix, b_mod_mix, norm_ffn, w_mod_ffn, b_mod_ffn, norm_final, gdn_w_in, gdn_conv_w, gdn_a_log, gdn_dt_bias, gdn_o_norm, gdn_w_out, s5_w_in, s5_lam_re, s5_lam_im, s5_log_dt, s5_b_re, s5_b_im, s5_c_re, s5_c_im, s5_d, s5_w_glu, moe_w_rg, moe_b_rg, moe_w_re, moe_b_re, moe_w1, moe_w3, moe_w2):
    bp = x_prompt.shape[0]
    depth = norm_mix.shape[0]

    lane_pad = lambda a, lo: jnp.pad(a, [(0, 0)] * (a.ndim - 1) + [(lo, LANES - lo - a.shape[-1])])
    abr, abi, wbr, wbi, wcr, wci = _s5_discretize(s5_lam_re[0], s5_lam_im[0], s5_log_dt[0], s5_b_re[0],
                                                  s5_b_im[0], s5_c_re[0], s5_c_im[0])
    wts = dict(
        norm_mix=norm_mix.reshape(depth, 1, D_MODEL), norm_ffn=norm_ffn.reshape(depth, 1, D_MODEL),
        norm_final=norm_final.reshape(1, D_MODEL),
        gdn_wq=gdn_w_in[0, :, :CONV_DIM + QK_DIM].astype(BF16),
        gdn_wba=lane_pad(gdn_w_in[0, :, CONV_DIM + QK_DIM:], 0).astype(BF16),
        gdn_alog=lane_pad(gdn_a_log[0][None], N_HEADS), gdn_dtb=lane_pad(gdn_dt_bias[0][None], N_HEADS),
        gdn_cw=jnp.pad(gdn_conv_w[0], ((0, SUBLANES - CONV_TAPS), (0, 0))),
        gdn_onorm=gdn_o_norm[0][None], gdn_wout=gdn_w_out[0].astype(BF16),
        s5_win=s5_w_in[0].astype(BF16), s5_wbr=wbr, s5_wbi=wbi, s5_wcr=wcr, s5_wci=wci,
        s5_d=s5_d[0].reshape(1, D_MODEL), s5_wglu=s5_w_glu[0].astype(BF16), s5_abr=abr, s5_abi=abi,
        moe_wr=lane_pad(jnp.concatenate([moe_w_rg, moe_w_re], axis=-1), 0),
        moe_br=lane_pad(jnp.concatenate([moe_b_rg, moe_b_re], axis=-1), 0)[:, None, :],
        moe_w1=jnp.swapaxes(moe_w1, 1, 2).reshape(depth, D_MODEL, N_EXPERTS * D_FF).astype(BF16),
        moe_w3=jnp.swapaxes(moe_w3, 1, 2).reshape(depth, D_MODEL, N_EXPERTS * D_FF).astype(BF16),
        moe_w2=moe_w2.reshape(depth, N_EXPERTS * D_FF, D_MODEL).astype(BF16),
    )

    c_all = jnp.concatenate([c_prompt, c_sample], axis=0)
    m_mix = _ada_mod(c_all, w_mod_mix, b_mod_mix)
    m_ffn = _ada_mod(c_all, w_mod_ffn, b_mod_ffn)

    y_p, p_delta, p_conv, p_re, p_im = _trunk(
        x_prompt, [m_mix[i, :bp] for i in range(depth)], [m_ffn[i, :bp] for i in range(depth)],
        None, None, None, None, wts,
        nb_tok=1, tt_tok=ROWS, nb_moe=1, tt_moe=MOE_ROWS, nb_delta=1, tt_delta=ROWS, chunk=CHUNK, tt_s5=S5_ROWS // SUBLANES, nbs_s5=SUBLANES)
    ls = x_sample.shape[1]
    y_s, s_delta, s_conv, s_re, s_im = _trunk(
        x_sample, [m_mix[i, bp:] for i in range(depth)], [m_ffn[i, bp:] for i in range(depth)],
        state_conv[0], state_delta[0], state_ssm_re[0].reshape(-1, SSM_STATE),
        state_ssm_im[0].reshape(-1, SSM_STATE), wts,
        nb_tok=ROWS // ls, tt_tok=ls, nb_moe=MOE_ROWS // ls, tt_moe=ls, nb_delta=2 * MXU_DIM // (ls * N_HEADS), tt_delta=ls, chunk=ls,
        tt_s5=ls, nbs_s5=ROWS // ls)
    return (y_p, y_s, p_delta, p_conv, p_re, p_im, s_delta, s_conv, s_re, s_im)
```

```python
import functools

import jax
import jax.numpy as jnp
from jax import lax
from jax.experimental import pallas as pl
from jax.experimental.pallas import tpu as pltpu

F32 = jnp.float32
BF16 = jnp.bfloat16

D_MODEL = 1024
N_HEADS = 8
HEAD_DIM = 128
QK_DIM = N_HEADS * HEAD_DIM
CONV_DIM = 3 * QK_DIM
CONV_TAPS = 4
CONV_PAD = 16
CHUNK = 64
SSM_GROUPS = 64
SSM_GROUP = 16
SSM_P = 64
SSM_STATE = SSM_GROUPS * SSM_P
N_GROUPS = 4
EXPERTS_PER_GROUP = 4
N_EXPERTS = 16
D_FF = 256
EPS = 1e-6

LANES = 128
SUBLANES = 8
MXU_DIM = 256
ROWS = 256
MOE_ROWS = 512
S5_ROWS = 512
LOCKSTEP_STACKS = 4
SCAN_LANES = 512
VMEM_LIMIT = 56 * 1024 * 1024


def _mm(a, b):
    return jnp.dot(a.astype(BF16), b.astype(BF16), preferred_element_type=F32)


def _bmm(a, b):
    return lax.dot_general(a.astype(BF16), b.astype(BF16), (((2,), (1,)), ((0,), (0,))),
                           preferred_element_type=F32)


def _bmm_nt(a, b):
    return lax.dot_general(a.astype(BF16), b.astype(BF16), (((2,), (2,)), ((0,), (0,))),
                           preferred_element_type=F32)


def _split_bf16(x, terms):
    out = []
    for _ in range(terms - 1):
        p = x.astype(BF16)
        out.append(p)
        x = x - p.astype(F32)
    out.append(x.astype(BF16))
    return out


def _rms(x, w):
    return x * lax.rsqrt(jnp.mean(x * x, axis=-1, keepdims=True) + EPS) * w


def _sigmoid(x):
    return 0.5 * jnp.tanh(0.5 * x) + 0.5


def _silu(x):
    h = 0.5 * x
    return h * jnp.tanh(h) + h


def _softplus(x):
    return jnp.maximum(x, 0.0) + jnp.log1p(jnp.exp(-jnp.abs(x)))


def _lane_col(tile, lane):
    li = lax.broadcasted_iota(jnp.int32, tile.shape, 1)
    return jnp.sum(jnp.where(li == lane, tile, 0.0), axis=1, keepdims=True)


def _log2(n):
    assert n & (n - 1) == 0
    return n.bit_length() - 1


def _params(*sem):
    return pltpu.CompilerParams(dimension_semantics=sem, vmem_limit_bytes=VMEM_LIMIT)


def _resident(a):
    return pl.BlockSpec(a.shape, lambda *_: (0,) * a.ndim, pipeline_mode=pl.Buffered(1))


def _mod_body(c_ref, w_ref, b_ref, o_ref):
    o_ref[0] = _mm(_silu(c_ref[...]), w_ref[0]) + b_ref[0]


def _ada_mod(c_all, w_mod, b_mod):
    nl, nr, tn = w_mod.shape[0], c_all.shape[0], 768
    return pl.pallas_call(
        _mod_body,
        grid=(nl, 3 * D_MODEL // tn),
        in_specs=[pl.BlockSpec((nr, D_MODEL), lambda l, j: (0, 0)),
                  pl.BlockSpec((1, D_MODEL, tn), lambda l, j: (l, 0, j)),
                  pl.BlockSpec((1, 1, tn), lambda l, j: (l, 0, j))],
        out_specs=pl.BlockSpec((1, nr, tn), lambda l, j: (l, 0, j)),
        out_shape=jax.ShapeDtypeStruct((nl, nr, 3 * D_MODEL), F32),
        compiler_params=_params("parallel", "parallel"),
        name="ada_mod",
    )(c_all, w_mod, b_mod.reshape(nl, 1, 3 * D_MODEL))


def _gdn_front_h(x_ref, sh_ref, sc_ref, nw_ref):
    x = x_ref[...]
    h = _rms(x, nw_ref[...]) * (1.0 + sc_ref[...]) + sh_ref[...]
    return h.reshape(x.shape[0] * x.shape[1], D_MODEL).astype(BF16)


def _gdn_front_gates(hb, wba_ref, alog_ref, dtb_ref, chunk):
    r = hb.shape[0]
    ba = jnp.dot(hb, wba_ref[...], preferred_element_type=F32)
    beta = _sigmoid(ba)
    g = -jnp.exp(alog_ref[...]) * _softplus(ba + dtb_ref[...])
    ri = lax.broadcasted_iota(jnp.int32, (r, r), 0)
    ci = lax.broadcasted_iota(jnp.int32, (r, r), 1)
    same = (ri >> _log2(chunk)) == (ci >> _log2(chunk))
    low = jnp.where(same & (ri >= ci), 1.0, 0.0).astype(BF16)
    ones = jnp.where(same, 1.0, 0.0).astype(BF16)
    gc = jnp.zeros((r, LANES), F32)
    gl = jnp.zeros((r, LANES), F32)
    for piece in _split_bf16(g, 3):
        gc = gc + jnp.dot(low, piece, preferred_element_type=F32)
        gl = gl + jnp.dot(ones, piece, preferred_element_type=F32)
    return jnp.concatenate([beta, gc, gl], axis=-1)


def _gdn_front_body(x_ref, sh_ref, sc_ref, nw_ref, wq_ref, wba_ref, alog_ref, dtb_ref,
                    qkv_ref, z_ref, gt_ref, *, chunk):
    nb, tt, _ = x_ref.shape
    hb = _gdn_front_h(x_ref, sh_ref, sc_ref, nw_ref)
    pq = jnp.dot(hb, wq_ref[...], preferred_element_type=F32)
    qkv_ref[...] = pq[:, :CONV_DIM].reshape(nb, tt, CONV_DIM)
    z_ref[...] = pq[:, CONV_DIM:].reshape(nb, tt, QK_DIM)
    gt_ref[...] = _gdn_front_gates(hb, wba_ref, alog_ref, dtb_ref, chunk).reshape(nb, tt, 3 * LANES)


def _gdn_front(x, shift, scale, nw, wq, wba, alog, dtb, *, nb, tt, chunk):
    b, l, _ = x.shape
    tok = lambda w: pl.BlockSpec((nb, tt, w), lambda i, j: (i, j, 0))
    mod = lambda col: pl.BlockSpec((nb, 1, D_MODEL), lambda i, j: (i, 0, col))
    full = _resident
    return pl.pallas_call(
        functools.partial(_gdn_front_body, chunk=chunk),
        grid=(b // nb, l // tt),
        in_specs=[tok(D_MODEL), mod(0), mod(1), full(nw), full(wq), full(wba), full(alog), full(dtb)],
        out_specs=[tok(CONV_DIM), tok(QK_DIM), tok(3 * LANES)],
        out_shape=[jax.ShapeDtypeStruct((b, l, CONV_DIM), F32),
                   jax.ShapeDtypeStruct((b, l, QK_DIM), F32),
                   jax.ShapeDtypeStruct((b, l, 3 * LANES), F32)],
        compiler_params=_params("parallel", "parallel"),
        name="gdn_front",
    )(x, shift, scale, nw, wq, wba, alog, dtb)


def _inv_unit_lower(neg_a, ri, ci, chunk):
    base = min(16, chunk)
    same = lambda size: (ri >> _log2(size)) == (ci >> _log2(size))
    n = jnp.where(same(base), neg_a, 0.0)
    q, m, k = n, n, 1
    while 2 * k < base:
        m = _bmm(m, m)
        q = q + m + _bmm(q, m)
        k *= 2
    size = base
    while size < chunk:
        e = jnp.where(same(2 * size) & jnp.logical_not(same(size)), neg_a, 0.0)
        x = e + _bmm(q, e)
        q = q + (x + _bmm(x, q))
        size *= 2
    return q


def _gdn_delta_body(*refs, nb, tt, chunk, has_state, fused):
    it = iter(refs)
    take = lambda n: [next(it) for _ in range(n)]
    if fused:
        x_ref, sh_ref, sc_ref, nw_ref, wq_ref, wba_ref, alog_ref, dtb_ref = take(8)
    else:
        qkv_ref, z_ref, gt_ref = take(3)
    cw_ref, on_ref = take(2)
    if has_state:
        s0_ref, c0_ref = take(2)
    o_ref, s_ref, cn_ref, xc, qkv_s, o_s, u_s, wq_s, qk_s, kdt_s = take(10)
    if fused:
        z_ref, gt_ref = take(2)
        hb = _gdn_front_h(x_ref, sh_ref, sc_ref, nw_ref)
        pq = jnp.dot(hb, wq_ref[...], preferred_element_type=F32)
        z_ref[...] = pq[:, CONV_DIM:].reshape(nb, tt, QK_DIM)
        gt_ref[...] = _gdn_front_gates(hb, wba_ref, alog_ref, dtb_ref, chunk).reshape(nb, tt, 3 * LANES)
        qkv_in = pq[:, :CONV_DIM].reshape(nb, tt, CONV_DIM)
    else:
        qkv_in = qkv_ref[...]
    hist = CONV_TAPS - 1
    pad = CONV_PAD

    @pl.when(pl.program_id(1) == 0)
    def _():
        xc[:, 0:pad - hist, :] = jnp.zeros((nb, pad - hist, CONV_DIM), F32)
        if has_state:
            s_ref[...] = s0_ref[...]
            xc[:, pad - hist:pad, :] = c0_ref[...]
        else:
            s_ref[...] = jnp.zeros(s_ref.shape, F32)
            xc[:, pad - hist:pad, :] = jnp.zeros((nb, hist, CONV_DIM), F32)

    xc[:, pad:pad + tt, :] = qkv_in
    cw = cw_ref[...]
    lead = SUBLANES

    def conv_rows(r0, r1):
        x_cur = xc[:, pad + r0 - lead:pad + r1, :]
        x_prev = xc[:, pad + r0 - lead - 1:pad + r1 - 1, :]
        z = x_cur * cw[1:2] + x_prev * cw[0:1]
        y = x_cur[:, lead:] * cw[3:4] + x_prev[:, lead:] * cw[2:3] + z[:, lead - 2:lead - 2 + r1 - r0]
        qkv_s[:, r0:r1, :] = _silu(y)
        for hd in range(N_HEADS):
            for part, scl in ((0, HEAD_DIM ** -0.5), (1, 1.0)):
                lanes = slice(part * QK_DIM + hd * HEAD_DIM, part * QK_DIM + (hd + 1) * HEAD_DIM)
                xh = qkv_s[:, r0:r1, lanes]
                xh = xh * (lax.rsqrt(jnp.sum(xh * xh, axis=-1, keepdims=True) + EPS) * scl)
                qkv_s[:, r0:r1, lanes] = xh


    per_stack = MXU_DIM // chunk
    units_all = [(s, hd) for s in range(nb) for hd in range(N_HEADS)]
    assert len(units_all) % per_stack == 0
    stacks = [units_all[i:i + per_stack] for i in range(0, len(units_all), per_stack)]
    r = MXU_DIM
    ri = lax.broadcasted_iota(jnp.int32, (r, r), 0)
    ci = lax.broadcasted_iota(jnp.int32, (r, r), 1)
    same = (ri >> _log2(chunk)) == (ci >> _log2(chunk))
    tril = same & (ri >= ci)
    strict = same & (ri > ci)
    eye = ri == ci

    n_chunks = tt // chunk
    dot = lambda p, q: jnp.dot(p, q, preferred_element_type=F32)

    pairs = [(c, units) for c in range(n_chunks) for units in stacks]
    rows_done = 0
    for first in range(0, len(pairs), LOCKSTEP_STACKS):
        group = pairs[first:first + LOCKSTEP_STACKS]
        ns = len(group)
        out = slice(first, first + ns)
        rows_needed = (group[-1][0] + 1) * chunk
        if rows_needed > rows_done:
            conv_rows(rows_done, rows_needed)
            rows_done = rows_needed

        def gather(f):
            return jnp.stack([jnp.concatenate([f(s, hd, slice(c * chunk, (c + 1) * chunk)) for s, hd in units], axis=0)
                              for c, units in group], axis=0)

        qn = gather(lambda s, hd, rows: qkv_s[s, rows, hd * HEAD_DIM:(hd + 1) * HEAD_DIM])
        kn = gather(lambda s, hd, rows: qkv_s[s, rows, QK_DIM + hd * HEAD_DIM:QK_DIM + (hd + 1) * HEAD_DIM])
        v = gather(lambda s, hd, rows: qkv_s[s, rows, 2 * QK_DIM + hd * HEAD_DIM:2 * QK_DIM + (hd + 1) * HEAD_DIM])
        beta = gather(lambda s, hd, rows: _lane_col(gt_ref[s, rows, 0:LANES], hd))
        gc = gather(lambda s, hd, rows: _lane_col(gt_ref[s, rows, LANES:2 * LANES], N_HEADS + hd))
        gl = gather(lambda s, hd, rows: _lane_col(gt_ref[s, rows, 2 * LANES:3 * LANES], N_HEADS + hd))
        eg = jnp.exp(gc)
        kb = kn * beta
        gc_row = jnp.sum(jnp.where(eye, gc, 0.0), axis=1, keepdims=True)
        diff = gc - gc_row
        decay = jnp.exp(jnp.where(tril, diff, -jnp.inf))
        decay_strict = jnp.exp(jnp.where(strict, diff, -jnp.inf))
        q_inv = _inv_unit_lower(_bmm_nt(-kb, kn) * decay_strict, ri, ci, chunk)
        rhs = jnp.concatenate([v * beta, kb * eg], axis=2)
        uw = rhs + _bmm(q_inv, rhs)
        u_s[out] = uw[:, :, :HEAD_DIM]
        per_unit = lambda t: t.reshape(ns, per_stack, chunk, HEAD_DIM)
        wq_s[out] = jnp.concatenate([per_unit(uw[:, :, HEAD_DIM:]), per_unit(qn * eg)], axis=2).reshape(
            ns, 2 * r, HEAD_DIM).astype(BF16)
        qk_s[out] = (_bmm_nt(qn, kn) * decay).astype(BF16)
        kdt_s[out] = jnp.swapaxes(kn * jnp.exp(gl - gc), 1, 2).astype(BF16)

        _gdn_sequential(first, group, chunk, gt_ref, s_ref, o_s, u_s, wq_s, qk_s, kdt_s)

    last = xc[:, pad + tt - hist:pad + tt, :]
    cn_ref[...] = last
    xc[:, pad - hist:pad, :] = last

    on = on_ref[...]
    for hd in range(N_HEADS):
        sl = slice(hd * HEAD_DIM, (hd + 1) * HEAD_DIM)
        o_ref[:, :, sl] = _rms(o_s[:, :, sl], on) * _silu(z_ref[:, :, sl])


def _gdn_sequential(first, group, chunk, gt_ref, s_ref, o_s, u_s, wq_s, qk_s, kdt_s):
    dot = lambda p, q: jnp.dot(p, q, preferred_element_type=F32)
    unit_of_row = lax.broadcasted_iota(jnp.int32, (MXU_DIM, HEAD_DIM), 0) >> _log2(chunk)
    for idx, (c, units) in enumerate(group, start=first):
        rows = slice(c * chunk, (c + 1) * chunk)
        ws, qs = [], []
        for i, (s, hd) in enumerate(units):
            both = dot(wq_s[idx, 2 * i * chunk:2 * (i + 1) * chunk, :], s_ref[s, hd].astype(BF16))
            ws.append(both[:chunk])
            qs.append(both[chunk:])
        v_new = (u_s[idx] - jnp.concatenate(ws, axis=0)).astype(BF16)
        o = jnp.concatenate(qs, axis=0) + dot(qk_s[idx], v_new)
        kdt = kdt_s[idx]
        for i, (s, hd) in enumerate(units):
            o_s[s, rows, hd * HEAD_DIM:(hd + 1) * HEAD_DIM] = o[i * chunk:(i + 1) * chunk]
            g_last = jnp.exp(_lane_col(gt_ref[s, c * chunk:c * chunk + 1, 2 * LANES:3 * LANES], N_HEADS + hd))
            upd = dot(kdt, jnp.where(unit_of_row == i, v_new, jnp.zeros_like(v_new)))
            s_ref[s, hd] = s_ref[s, hd] * g_last + upd


def _gdn_delta(front, cw, onorm, s0, c0, *, dims, nb, tt, chunk, fused):
    b, l = dims
    has_state = s0 is not None
    n_stacks = nb * tt * N_HEADS // MXU_DIM
    tok = lambda w: pl.BlockSpec((nb, tt, w), lambda i, j: (i, j, 0))
    mod = lambda col: pl.BlockSpec((nb, 1, D_MODEL), lambda i, j: (i, 0, col))
    full = _resident
    st_spec = pl.BlockSpec((nb, N_HEADS, HEAD_DIM, HEAD_DIM), lambda i, j: (i, 0, 0, 0))
    cv_spec = pl.BlockSpec((nb, CONV_TAPS - 1, CONV_DIM), lambda i, j: (i, 0, 0))
    if fused:
        in_specs = [tok(D_MODEL), mod(0), mod(1)] + [full(a) for a in front[3:]]
    else:
        in_specs = [tok(CONV_DIM), tok(QK_DIM), tok(3 * LANES)]
    in_specs += [full(cw), full(onorm)]
    args = list(front) + [cw, onorm]
    if has_state:
        in_specs += [st_spec, cv_spec]
        args += [s0, c0]
    scratch = [pltpu.VMEM((nb, tt + CONV_PAD, CONV_DIM), F32),
               pltpu.VMEM((nb, tt, CONV_DIM), F32),
               pltpu.VMEM((nb, tt, QK_DIM), F32),
               pltpu.VMEM((n_stacks, MXU_DIM, HEAD_DIM), F32),
               pltpu.VMEM((n_stacks, 2 * MXU_DIM, HEAD_DIM), BF16),
               pltpu.VMEM((n_stacks, MXU_DIM, MXU_DIM), BF16),
               pltpu.VMEM((n_stacks, HEAD_DIM, MXU_DIM), BF16)]
    if fused:
        scratch += [pltpu.VMEM((nb, tt, QK_DIM), F32), pltpu.VMEM((nb, tt, 3 * LANES), F32)]
    return pl.pallas_call(
        functools.partial(_gdn_delta_body, nb=nb, tt=tt, chunk=chunk, has_state=has_state, fused=fused),
        grid=(b // nb, l // tt),
        in_specs=in_specs,
        out_specs=[tok(QK_DIM), st_spec, cv_spec],
        out_shape=[jax.ShapeDtypeStruct((b, l, QK_DIM), F32),
                   jax.ShapeDtypeStruct((b, N_HEADS, HEAD_DIM, HEAD_DIM), F32),
                   jax.ShapeDtypeStruct((b, CONV_TAPS - 1, CONV_DIM), F32)],
        scratch_shapes=scratch,
        compiler_params=_params("parallel", "arbitrary"),
        name="gdn_delta",
    )(*args)


def _route(logits):
    lane = lax.broadcasted_iota(jnp.int32, logits.shape, 1)
    neg, big = -1e30, 1 << 20
    rmax = lambda t: jnp.max(t, axis=1, keepdims=True)
    rsum = lambda t: jnp.sum(t, axis=1, keepdims=True)
    first = lambda cond: jnp.min(jnp.where(cond, lane, big), axis=1, keepdims=True)
    is_g = lane < N_GROUPS
    lg = jnp.where(is_g, logits, neg)
    mg = rmax(lg)
    g_idx = first(is_g & (lg >= mg))
    p_group = 1.0 / rsum(jnp.where(is_g, jnp.exp(lg - mg), 0.0))
    lo = N_GROUPS + EXPERTS_PER_GROUP * g_idx
    in_g = (lane >= lo) & (lane < lo + EXPERTS_PER_GROUP)
    le = jnp.where(in_g, logits, neg)
    m1 = rmax(le)
    se = rsum(jnp.where(in_g, jnp.exp(le - m1), 0.0))
    i1 = first(in_g & (le >= m1))
    le2 = jnp.where(lane == i1, neg, le)
    m2 = rmax(le2)
    i2 = first(in_g & (lane != i1) & (le2 >= m2))
    p1 = 1.0 / se
    p2 = jnp.exp(m2 - m1) / se
    tot = p1 + p2
    local = (jnp.where(lane == i1 - lo, p1 / tot * p_group, 0.0)
             + jnp.where(lane == i2 - lo, p2 / tot * p_group, 0.0))
    return g_idx, local


META_GROUP, META_RANK = 4, 5
PIECE_LANES = 8
SORT_PAD = 16
EXPERT_ROWS = 256
EXPERT_STEP = 32
SORT_ROWS = MOE_ROWS + N_GROUPS * SORT_PAD + EXPERT_ROWS
USED_ROWS = MOE_ROWS + LANES
MOE_HALVES = 2


def _moe_route_body(*refs, nb, tt, has_proj):
    it = iter(refs)
    x_ref = next(it)
    if has_proj:
        y_ref, wo_ref, gm_ref = next(it), next(it), next(it)
    sh_ref, sc_ref, nw_ref, wr_ref, br_ref, x1_ref, h_ref, meta_ref, cnt_ref = (next(it) for _ in range(9))
    r = nb * tt
    dot = lambda p, q: jnp.dot(p, q, preferred_element_type=F32)

    x = x_ref[...].reshape(nb, tt, D_MODEL)
    if has_proj:
        x = x + gm_ref[...] * _mm(y_ref[...].reshape(r, D_MODEL), wo_ref[...]).reshape(nb, tt, D_MODEL)
    h = (_rms(x, nw_ref[...]) * (1.0 + sc_ref[...]) + sh_ref[...]).reshape(r, D_MODEL)
    x1_ref[...] = x.reshape(r, D_MODEL)
    h_hi, h_lo = _split_bf16(h, 2)
    h_ref[...] = h_hi
    w_hi, w_lo = _split_bf16(wr_ref[...], 2)
    logits = dot(h_hi, w_hi) + (dot(h_hi, w_lo) + dot(h_lo, w_hi)) + br_ref[...]
    g_idx, local = _route(logits)

    lane = lax.broadcasted_iota(jnp.int32, (r, LANES), 1)
    onehot = jnp.where(lane == g_idx, 1.0, 0.0)
    ri = lax.broadcasted_iota(jnp.int32, (r, r), 0)
    ci = lax.broadcasted_iota(jnp.int32, (r, r), 1)
    earlier = jnp.where(ri > ci, 1.0, 0.0).astype(BF16)
    rank = jnp.sum(onehot * dot(earlier, onehot.astype(BF16)), axis=1, keepdims=True)
    meta_ref[...] = jnp.where(lane == META_GROUP, g_idx.astype(F32),
                              jnp.where(lane == META_RANK, rank, local))
    counts = jnp.sum(onehot, axis=0, keepdims=True).astype(jnp.int32)
    cnt_ref[...] = jnp.broadcast_to(counts, (1, SUBLANES, LANES))


def _moe_expert_body(cnt_ref, *refs, nb, tt, n_time_tiles, final_norm):
    it = iter(refs)
    x1_ref, h_ref, meta_ref, gf_ref, w1_ref, w3_ref, w2_ref = (next(it) for _ in range(7))
    if final_norm:
        nf_ref = next(it)
    out_ref, hs_s, ms_s, acc_s = (next(it) for _ in range(4))
    r = MOE_ROWS
    dot = lambda p, q: jnp.dot(p, q, preferred_element_type=F32)
    step = pl.program_id(0) * n_time_tiles + pl.program_id(1)
    grp = pl.program_id(2)

    def layout(half):
        counts = [cnt_ref[MOE_HALVES * step + half, g] for g in range(N_GROUPS)]
        starts, nxt = [], 0
        for g in range(N_GROUPS):
            starts.append(nxt)
            nxt = nxt + (((counts[g] + (SORT_PAD - 1)) >> _log2(SORT_PAD)) << _log2(SORT_PAD))
        return counts, starts

    def positions(half, starts):
        meta = meta_ref[half * r:(half + 1) * r, :]
        g_col = _lane_col(meta, META_GROUP)
        pos = _lane_col(meta, META_RANK)
        for g in range(N_GROUPS):
            pos = pos + jnp.where(g_col == g, jnp.asarray(starts[g], jnp.int32).astype(F32), 0.0)
        return meta, pos

    @pl.when(grp == 0)
    def _():
        for half in range(MOE_HALVES):
            meta, pos = positions(half, layout(half)[1])
            pos_row = jnp.broadcast_to(pos, (r, LANES)).T[0:1, :].astype(jnp.int32)
            perm = jnp.where(lax.broadcasted_iota(jnp.int32, (USED_ROWS, r), 0) == pos_row, 1.0, 0.0).astype(BF16)
            hs_s[half, 0:USED_ROWS] = dot(perm, h_ref[half * r:(half + 1) * r, :]).astype(BF16)
            hs_s[half, USED_ROWS:] = jnp.zeros((SORT_ROWS - USED_ROWS, D_MODEL), BF16)
            lane = lax.broadcasted_iota(jnp.int32, (r, LANES), 1)
            pieces = _split_bf16(jnp.where(lane < EXPERTS_PER_GROUP, meta, 0.0), 3)
            packed = pieces[0].astype(F32)
            for k in (1, 2):
                packed = packed + pltpu.roll(pieces[k].astype(F32), k * PIECE_LANES, 1)
            ms = dot(perm, packed.astype(BF16))
            ms_s[half, 0:USED_ROWS] = (ms + pltpu.roll(ms, LANES - PIECE_LANES, 1)
                                       + pltpu.roll(ms, LANES - 2 * PIECE_LANES, 1))
            ms_s[half, USED_ROWS:] = jnp.zeros((SORT_ROWS - USED_ROWS, LANES), F32)
        acc_s[...] = jnp.zeros(acc_s.shape, F32)

    pick = lambda vals: functools.reduce(lambda acc, gv: jnp.where(grp == gv[0], gv[1], acc),
                                         list(enumerate(vals))[1:], jnp.asarray(vals[0], jnp.int32))
    lay = [layout(half) for half in range(MOE_HALVES)]
    cnt_g = [pick(c) for c, _ in lay]
    start_g = [pick(s) for _, s in lay]
    def block(first, size):
        base = [jnp.minimum(s + first, SORT_ROWS - size) for s in start_g]
        rows = [pl.ds(pl.multiple_of(bs, SORT_PAD), size) for bs in base]
        iota = lax.broadcasted_iota(jnp.int32, (size, 1), 0)
        hb = jnp.concatenate([hs_s[half, rows[half], :] for half in range(MOE_HALVES)], axis=0)
        mb = []
        for half in range(MOE_HALVES):
            seg_row = iota + (base[half] - start_g[half])
            mb.append(jnp.where((seg_row >= first) & (seg_row < cnt_g[half]), ms_s[half, rows[half], :], 0.0))
        mb = jnp.concatenate(mb, axis=0)
        hid = _silu(dot(hb, w1_ref[...])) * dot(hb, w3_ref[...])
        parts = [(hid[:, j * D_FF:(j + 1) * D_FF] * _lane_col(mb, j)).astype(BF16)
                 for j in range(EXPERTS_PER_GROUP)]
        res = dot(jnp.concatenate(parts, axis=1), w2_ref[...])
        for half in range(MOE_HALVES):
            acc_s[half, rows[half], :] += res[half * size:(half + 1) * size]

    longest = functools.reduce(jnp.maximum, cnt_g)
    n_full = longest // EXPERT_ROWS

    def full_block(k, carry):
        block(k * EXPERT_ROWS, EXPERT_ROWS)
        return carry

    lax.fori_loop(0, n_full, full_block, 0)
    tail = longest - n_full * EXPERT_ROWS
    for size in range(EXPERT_STEP, EXPERT_ROWS + 1, EXPERT_STEP):
        pl.when((tail > size - EXPERT_STEP) & (tail <= size))(
            functools.partial(block, n_full * EXPERT_ROWS, size))

    @pl.when(grp == N_GROUPS - 1)
    def _():
        moes = []
        for half in range(MOE_HALVES):
            _, pos = positions(half, layout(half)[1])
            unperm = jnp.where(lax.broadcasted_iota(jnp.int32, (r, USED_ROWS), 1) == pos.astype(jnp.int32),
                               1.0, 0.0).astype(BF16)
            a_hi, a_lo = _split_bf16(acc_s[half, 0:USED_ROWS, :], 2)
            moes.append(dot(unperm, a_hi) + dot(unperm, a_lo))
        moe = jnp.concatenate(moes, axis=0)
        out = x1_ref[...].reshape(nb, tt, D_MODEL) + gf_ref[...] * moe.reshape(nb, tt, D_MODEL)
        if final_norm:
            out = _rms(out, nf_ref[...])
        out_ref[...] = out.reshape(out_ref.shape)


def _moe(x, y, wo, mod_mix, mod_ffn, nw, wr, br, w1, w3, w2, nf, *, layer, dims, nb, tt,
         in_time_major, out_time_major):
    b, l = dims
    has_proj, final_norm = y is not None, nf is not None
    gb, gt = b // nb, l // tt
    r, n_tiles = nb * tt, gb * gt
    bm = pl.BlockSpec((nb, tt, D_MODEL), lambda i, j, *_: (i, j, 0))
    tm = pl.BlockSpec((tt, D_MODEL), lambda i, j, *_: (j, i))
    mod = lambda col: pl.BlockSpec((nb, 1, D_MODEL), lambda i, j, *_: (i, 0, col))
    flat = lambda w: pl.BlockSpec((r, w), lambda i, j, *_: (i * gt + j, 0))
    x_spec = tm if in_time_major else bm
    in_specs, args = [x_spec], [x]
    if has_proj:
        in_specs += [x_spec, _resident(wo), mod(2)]
        args += [y, wo, mod_mix]
    in_specs += [mod(0), mod(1)] + [_resident(a) for a in (nw, wr, br)]
    args += [mod_ffn, mod_ffn, nw, wr, br]
    x1, h, meta, cnt = pl.pallas_call(
        functools.partial(_moe_route_body, nb=nb, tt=tt, has_proj=has_proj),
        grid=(gb, gt),
        in_specs=in_specs,
        out_specs=[flat(D_MODEL), flat(D_MODEL), flat(LANES),
                   pl.BlockSpec((1, SUBLANES, LANES), lambda i, j: (i * gt + j, 0, 0))],
        out_shape=[jax.ShapeDtypeStruct((n_tiles * r, D_MODEL), F32),
                   jax.ShapeDtypeStruct((n_tiles * r, D_MODEL), BF16),
                   jax.ShapeDtypeStruct((n_tiles * r, LANES), F32),
                   jax.ShapeDtypeStruct((n_tiles, SUBLANES, LANES), jnp.int32)],
        compiler_params=_params("parallel", "parallel"),
        name="moe_route",
    )(*args)

    nb2, tt2 = (nb, MOE_HALVES * tt) if nb == 1 else (MOE_HALVES * nb, tt)
    gb2, gt2 = b // nb2, l // tt2
    assert r == MOE_ROWS and gb2 * gt2 * MOE_HALVES == n_tiles
    gw = EXPERTS_PER_GROUP * D_FF
    bm2 = pl.BlockSpec((nb2, tt2, D_MODEL), lambda i, j, g, *_: (i, j, 0))
    tm2 = pl.BlockSpec((tt2, D_MODEL), lambda i, j, g, *_: (j, i))
    flat2 = lambda w: pl.BlockSpec((MOE_HALVES * r, w), lambda i, j, g, *_: (i * gt2 + j, 0))
    in_specs = [flat2(D_MODEL), flat2(D_MODEL), flat2(LANES),
                pl.BlockSpec((nb2, 1, D_MODEL), lambda i, j, g, *_: (i, 0, 2)),
                pl.BlockSpec((None, D_MODEL, gw), lambda i, j, g, *_: (layer, 0, g)),
                pl.BlockSpec((None, D_MODEL, gw), lambda i, j, g, *_: (layer, 0, g)),
                pl.BlockSpec((None, gw, D_MODEL), lambda i, j, g, *_: (layer, g, 0))]
    args = [x1, h, meta, mod_ffn, w1, w3, w2]
    if final_norm:
        in_specs.append(_resident(nf))
        args.append(nf)
    if out_time_major:
        assert nb == 1
        out_spec, out_shape = tm2, jax.ShapeDtypeStruct((l, b * D_MODEL), F32)
    else:
        out_spec, out_shape = bm2, jax.ShapeDtypeStruct((b, l, D_MODEL), F32)
    return pl.pallas_call(
        functools.partial(_moe_expert_body, nb=nb2, tt=tt2, n_time_tiles=gt2, final_norm=final_norm),
        grid_spec=pltpu.PrefetchScalarGridSpec(
            num_scalar_prefetch=1,
            grid=(gb2, gt2, N_GROUPS),
            in_specs=in_specs,
            out_specs=out_spec,
            scratch_shapes=[pltpu.VMEM((MOE_HALVES, SORT_ROWS, D_MODEL), BF16),
                            pltpu.VMEM((MOE_HALVES, SORT_ROWS, LANES), F32),
                            pltpu.VMEM((MOE_HALVES, SORT_ROWS, D_MODEL), F32)]),
        out_shape=out_shape,
        compiler_params=_params("parallel", "parallel", "arbitrary"),
        name="moe_experts",
    )(cnt[:, 0, :N_GROUPS], *args)


def _s5_body(*refs, tt, nbs, has_state):
    if has_state:
        (x_ref, sh_ref, sc_ref, gm_ref, nw_ref, win_ref, wbr_ref, wbi_ref, wcr_ref, wci_ref, dsk_ref,
         wglu_ref, abr_ref, abi_ref, s0r_ref, s0i_ref, out_ref, sr_ref, si_ref, xr_s, xi_s) = refs
    else:
        (x_ref, sh_ref, sc_ref, gm_ref, nw_ref, win_ref, wbr_ref, wbi_ref, wcr_ref, wci_ref, dsk_ref,
         wglu_ref, abr_ref, abi_ref, out_ref, sr_ref, si_ref, xr_s, xi_s) = refs
    r = tt * nbs
    blocks = D_MODEL // MXU_DIM
    sw = SSM_STATE // blocks

    @pl.when(pl.program_id(1) == 0)
    def _():
        if has_state:
            sr_ref[...] = s0r_ref[...]
            si_ref[...] = s0i_ref[...]
        else:
            sr_ref[...] = jnp.zeros(sr_ref.shape, F32)
            si_ref[...] = jnp.zeros(si_ref.shape, F32)

    x = x_ref[...]
    h = (_rms(x, nw_ref[...]) * (1.0 + sc_ref[...]) + sh_ref[...]).reshape(r, D_MODEL)
    u = _mm(h, win_ref[...])
    ub = u.astype(BF16)
    for j in range(blocks):
        uj = ub[:, j * MXU_DIM:(j + 1) * MXU_DIM]
        xr_s[:, :, j * sw:(j + 1) * sw] = jnp.dot(
            uj, wbr_ref[j], preferred_element_type=F32).reshape(tt, nbs, sw)
        xi_s[:, :, j * sw:(j + 1) * sw] = jnp.dot(
            uj, wbi_ref[j], preferred_element_type=F32).reshape(tt, nbs, sw)

    for rg in range(nbs // SUBLANES):
        rows = slice(rg * SUBLANES, (rg + 1) * SUBLANES)
        for ch in range(SSM_STATE // SCAN_LANES):
            lanes = slice(ch * SCAN_LANES, (ch + 1) * SCAN_LANES)
            ar = jnp.broadcast_to(abr_ref[:, lanes], (SUBLANES, SCAN_LANES))
            ai = jnp.broadcast_to(abi_ref[:, lanes], (SUBLANES, SCAN_LANES))

            hr, hi = sr_ref[rows, lanes], si_ref[rows, lanes]
            for t in range(tt):
                hr, hi = (ar * hr - ai * hi + xr_s[t, rows, lanes],
                          ar * hi + ai * hr + xi_s[t, rows, lanes])
                xr_s[t, rows, lanes] = hr
                xi_s[t, rows, lanes] = hi
            sr_ref[rows, lanes] = hr
            si_ref[rows, lanes] = hi

    ys = []
    for j in range(blocks):
        st_r = xr_s[:, :, j * sw:(j + 1) * sw].reshape(r, sw)
        st_i = xi_s[:, :, j * sw:(j + 1) * sw].reshape(r, sw)
        ys.append(_mm(st_r, wcr_ref[j]) - _mm(st_i, wci_ref[j]))
    y = jnp.concatenate(ys, axis=1) + dsk_ref[...] * u
    ag = _mm(jax.nn.gelu(y), wglu_ref[...])
    mix = ag[:, :D_MODEL] * _sigmoid(ag[:, D_MODEL:])
    out_ref[...] = x + gm_ref[...] * mix.reshape(tt, nbs, D_MODEL)


def _s5(x, mod, nw, win, wbr, wbi, wcr, wci, dsk, wglu, abr, abi, s0r, s0i, *, tt, nbs):
    l, b, _ = x.shape
    has_state = s0r is not None
    xs = pl.BlockSpec((tt, nbs, D_MODEL), lambda i, j: (j, i, 0))
    mods = lambda col: pl.BlockSpec((nbs, D_MODEL), lambda i, j: (i, col))
    full = _resident
    st = pl.BlockSpec((nbs, SSM_STATE), lambda i, j: (i, 0))
    consts = [nw, win, wbr, wbi, wcr, wci, dsk, wglu, abr, abi]
    in_specs = [xs, mods(0), mods(1), mods(2)] + [full(a) for a in consts]
    args = [x, mod, mod, mod] + consts
    if has_state:
        in_specs += [st, st]
        args += [s0r, s0i]
    return pl.pallas_call(
        functools.partial(_s5_body, tt=tt, nbs=nbs, has_state=has_state),
        grid=(b // nbs, l // tt),
        in_specs=in_specs,
        out_specs=[xs, st, st],
        out_shape=[jax.ShapeDtypeStruct((l, b, D_MODEL), F32),
                   jax.ShapeDtypeStruct((b, SSM_STATE), F32),
                   jax.ShapeDtypeStruct((b, SSM_STATE), F32)],
        scratch_shapes=[pltpu.VMEM((tt, nbs, SSM_STATE), F32), pltpu.VMEM((tt, nbs, SSM_STATE), F32)],
        compiler_params=_params("parallel", "arbitrary"),
        name="s5",
    )(*args)


def _s5_discretize(lam_re, lam_im, log_dt, b_re, b_im, c_re, c_im):
    dt = jnp.exp(log_dt)[:, None]
    mag = jnp.exp(lam_re * dt)
    ang = lam_im * dt
    ab_re, ab_im = mag * jnp.cos(ang), mag * jnp.sin(ang)
    den = lam_re * lam_re + lam_im * lam_im
    f_re = ((ab_re - 1.0) * lam_re + ab_im * lam_im) / den
    f_im = (ab_im * lam_re - (ab_re - 1.0) * lam_im) / den
    bb_re = f_re[..., None] * b_re - f_im[..., None] * b_im
    bb_im = f_re[..., None] * b_im + f_im[..., None] * b_re
    blocks = D_MODEL // MXU_DIM
    gpb = SSM_GROUPS // blocks

    def diag_blocks(t, rows_per_group, cols_per_group):
        tiled = jnp.tile(t.reshape(blocks, gpb * rows_per_group, cols_per_group), (1, 1, gpb))
        rg = lax.broadcasted_iota(jnp.int32, tiled.shape, 1) // rows_per_group
        cg = lax.broadcasted_iota(jnp.int32, tiled.shape, 2) // cols_per_group
        return jnp.where(rg == cg, tiled, 0.0).astype(BF16)

    b_blocks = lambda bb: diag_blocks(jnp.swapaxes(bb, 1, 2), SSM_GROUP, SSM_P)
    c_blocks = lambda cc: diag_blocks(jnp.swapaxes(cc, 1, 2), SSM_P, SSM_GROUP)

    return (ab_re.reshape(1, SSM_STATE), ab_im.reshape(1, SSM_STATE),
            b_blocks(bb_re), b_blocks(bb_im), c_blocks(c_re), c_blocks(c_im))


def _trunk(x, mods_mix, mods_ffn, conv0, delta0, re0, im0, wts, *, nb_tok, tt_tok, nb_moe, tt_moe, nb_delta, tt_delta, chunk,
           tt_s5, nbs_s5):
    b, l, _ = x.shape
    mm3 = [m.reshape(b, 1, 3 * D_MODEL) for m in mods_mix]
    mf3 = [m.reshape(b, 1, 3 * D_MODEL) for m in mods_ffn]

    front = (x, mm3[0], mm3[0], wts['norm_mix'][0], wts['gdn_wq'], wts['gdn_wba'], wts['gdn_alog'], wts['gdn_dtb'])
    fused = nb_delta * tt_delta >= ROWS
    if not fused:
        front = _gdn_front(*front, nb=nb_tok, tt=tt_tok, chunk=chunk)
    o, s_delta, s_conv = _gdn_delta(front, wts['gdn_cw'], wts['gdn_onorm'], delta0, conv0, dims=(b, l),
                                    nb=nb_delta, tt=tt_delta, chunk=chunk, fused=fused)
    prompt_like = nb_tok == 1
    moe_w = lambda i: (wts['norm_ffn'][i], wts['moe_wr'][i], wts['moe_br'][i],
                       wts['moe_w1'], wts['moe_w3'], wts['moe_w2'])
    x1 = _moe(x, o, wts['gdn_wout'], mm3[0], mf3[0], *moe_w(0), None, layer=0, dims=(b, l), nb=nb_moe, tt=tt_moe,
              in_time_major=False, out_time_major=prompt_like)
    xt = x1.reshape(l, b, D_MODEL) if prompt_like else jnp.swapaxes(x1, 0, 1)
    x2, s_re, s_im = _s5(xt, mods_mix[1], wts['norm_mix'][1], wts['s5_win'], wts['s5_wbr'], wts['s5_wbi'],
                         wts['s5_wcr'], wts['s5_wci'], wts['s5_d'], wts['s5_wglu'], wts['s5_abr'], wts['s5_abi'],
                         re0, im0, tt=tt_s5, nbs=nbs_s5)
    x2 = x2.reshape(l, b * D_MODEL) if prompt_like else jnp.swapaxes(x2, 0, 1)
    y = _moe(x2, None, None, None, mf3[1], *moe_w(1), wts['norm_final'], layer=1, dims=(b, l), nb=nb_moe, tt=tt_moe,
             in_time_major=prompt_like, out_time_major=False)
    return (y, s_delta[None], s_conv[None],
            s_re.reshape(1, b, SSM_GROUPS, SSM_P), s_im.reshape(1, b, SSM_GROUPS, SSM_P))


def kernel(x_prompt, x_sample, state_delta, state_conv, state_ssm_re, state_ssm_im, c_prompt, c_sample, norm_mix, w_mod_mix, b_mod_mix, norm_ffn, w_mod_ffn, b_mod_ffn, norm_final, gdn_w_in, gdn_conv_w, gdn_a_log, gdn_dt_bias, gdn_o_norm, gdn_w_out, s5_w_in, s5_lam_re, s5_lam_im, s5_log_dt, s5_b_re, s5_b_im, s5_c_re, s5_c_im, s5_d, s5_w_glu, moe_w_rg, moe_b_rg, moe_w_re, moe_b_re, moe_w1, moe_w3, moe_w2):
    bp = x_prompt.shape[0]
    depth = norm_mix.shape[0]

    lane_pad = lambda a, lo: jnp.pad(a, [(0, 0)] * (a.ndim - 1) + [(lo, LANES - lo - a.shape[-1])])
    abr, abi, wbr, wbi, wcr, wci = _s5_discretize(s5_lam_re[0], s5_lam_im[0], s5_log_dt[0], s5_b_re[0],
                                                  s5_b_im[0], s5_c_re[0], s5_c_im[0])
    wts = dict(
        norm_mix=norm_mix.reshape(depth, 1, D_MODEL), norm_ffn=norm_ffn.reshape(depth, 1, D_MODEL),
        norm_final=norm_final.reshape(1, D_MODEL),
        gdn_wq=gdn_w_in[0, :, :CONV_DIM + QK_DIM].astype(BF16),
        gdn_wba=lane_pad(gdn_w_in[0, :, CONV_DIM + QK_DIM:], 0).astype(BF16),
        gdn_alog=lane_pad(gdn_a_log[0][None], N_HEADS), gdn_dtb=lane_pad(gdn_dt_bias[0][None], N_HEADS),
        gdn_cw=jnp.pad(gdn_conv_w[0], ((0, SUBLANES - CONV_TAPS), (0, 0))),
        gdn_onorm=gdn_o_norm[0][None], gdn_wout=gdn_w_out[0].astype(BF16),
        s5_win=s5_w_in[0].astype(BF16), s5_wbr=wbr, s5_wbi=wbi, s5_wcr=wcr, s5_wci=wci,
        s5_d=s5_d[0].reshape(1, D_MODEL), s5_wglu=s5_w_glu[0].astype(BF16), s5_abr=abr, s5_abi=abi,
        moe_wr=lane_pad(jnp.concatenate([moe_w_rg, moe_w_re], axis=-1), 0),
        moe_br=lane_pad(jnp.concatenate([moe_b_rg, moe_b_re], axis=-1), 0)[:, None, :],
        moe_w1=jnp.swapaxes(moe_w1, 1, 2).reshape(depth, D_MODEL, N_EXPERTS * D_FF).astype(BF16),
        moe_w3=jnp.swapaxes(moe_w3, 1, 2).reshape(depth, D_MODEL, N_EXPERTS * D_FF).astype(BF16),
        moe_w2=moe_w2.reshape(depth, N_EXPERTS * D_FF, D_MODEL).astype(BF16),
    )

    c_all = jnp.concatenate([c_prompt, c_sample], axis=0)
    m_mix = _ada_mod(c_all, w_mod_mix, b_mod_mix)
    m_ffn = _ada_mod(c_all, w_mod_ffn, b_mod_ffn)

    y_p, p_delta, p_conv, p_re, p_im = _trunk(
        x_prompt, [m_mix[i, :bp] for i in range(depth)], [m_ffn[i, :bp] for i in range(depth)],
        None, None, None, None, wts,
        nb_tok=1, tt_tok=ROWS, nb_moe=1, tt_moe=MOE_ROWS, nb_delta=1, tt_delta=ROWS, chunk=CHUNK, tt_s5=S5_ROWS // SUBLANES, nbs_s5=SUBLANES)
    ls = x_sample.shape[1]
    y_s, s_delta, s_conv, s_re, s_im = _trunk(
        x_sample, [m_mix[i, bp:] for i in range(depth)], [m_ffn[i, bp:] for i in range(depth)],
        state_conv[0], state_delta[0], state_ssm_re[0].reshape(-1, SSM_STATE),
        state_ssm_im[0].reshape(-1, SSM_STATE), wts,
        nb_tok=MOE_ROWS // ls, tt_tok=ls, nb_moe=MOE_ROWS // ls, tt_moe=ls, nb_delta=2 * MXU_DIM // (ls * N_HEADS), tt_delta=ls, chunk=ls,
        tt_s5=ls, nbs_s5=ROWS // ls)
    return (y_p, y_s, p_delta, p_conv, p_re, p_im, s_delta, s_conv, s_re, s_im)
```

```python
import functools

import jax
import jax.numpy as jnp
from jax import lax
from jax.experimental import pallas as pl
from jax.experimental.pallas import tpu as pltpu

F32 = jnp.float32
BF16 = jnp.bfloat16

D_MODEL = 1024
N_HEADS = 8
HEAD_DIM = 128
QK_DIM = N_HEADS * HEAD_DIM
CONV_DIM = 3 * QK_DIM
CONV_TAPS = 4
CONV_PAD = 16
CHUNK = 64
SSM_GROUPS = 64
SSM_GROUP = 16
SSM_P = 64
SSM_STATE = SSM_GROUPS * SSM_P
N_GROUPS = 4
EXPERTS_PER_GROUP = 4
N_EXPERTS = 16
D_FF = 256
EPS = 1e-6

LANES = 128
SUBLANES = 8
MXU_DIM = 256
ROWS = 256
MOE_ROWS = 512
S5_ROWS = 512
LOCKSTEP_STACKS = 4
SCAN_LANES = 512
VMEM_LIMIT = 56 * 1024 * 1024


def _mm(a, b):
    return jnp.dot(a.astype(BF16), b.astype(BF16), preferred_element_type=F32)


def _bmm(a, b):
    return lax.dot_general(a.astype(BF16), b.astype(BF16), (((2,), (1,)), ((0,), (0,))),
                           preferred_element_type=F32)


def _bmm_nt(a, b):
    return lax.dot_general(a.astype(BF16), b.astype(BF16), (((2,), (2,)), ((0,), (0,))),
                           preferred_element_type=F32)


def _split_bf16(x, terms):
    out = []
    for _ in range(terms - 1):
        p = x.astype(BF16)
        out.append(p)
        x = x - p.astype(F32)
    out.append(x.astype(BF16))
    return out


def _rms(x, w):
    return x * lax.rsqrt(jnp.mean(x * x, axis=-1, keepdims=True) + EPS) * w


def _sigmoid(x):
    return 0.5 * jnp.tanh(0.5 * x) + 0.5


def _silu(x):
    h = 0.5 * x
    return h * jnp.tanh(h) + h


def _softplus(x):
    return jnp.maximum(x, 0.0) + jnp.log1p(jnp.exp(-jnp.abs(x)))


def _lane_col(tile, lane):
    li = lax.broadcasted_iota(jnp.int32, tile.shape, 1)
    return jnp.sum(jnp.where(li == lane, tile, 0.0), axis=1, keepdims=True)


def _log2(n):
    assert n & (n - 1) == 0
    return n.bit_length() - 1


def _params(*sem):
    return pltpu.CompilerParams(dimension_semantics=sem, vmem_limit_bytes=VMEM_LIMIT)


def _resident(a):
    return pl.BlockSpec(a.shape, lambda *_: (0,) * a.ndim, pipeline_mode=pl.Buffered(1))


def _mod_body(c_ref, w_ref, b_ref, o_ref):
    o_ref[0] = _mm(_silu(c_ref[...]), w_ref[0]) + b_ref[0]


def _ada_mod(c_all, w_mod, b_mod):
    nl, nr, tn = w_mod.shape[0], c_all.shape[0], 768
    return pl.pallas_call(
        _mod_body,
        grid=(nl, 3 * D_MODEL // tn),
        in_specs=[pl.BlockSpec((nr, D_MODEL), lambda l, j: (0, 0)),
                  pl.BlockSpec((1, D_MODEL, tn), lambda l, j: (l, 0, j)),
                  pl.BlockSpec((1, 1, tn), lambda l, j: (l, 0, j))],
        out_specs=pl.BlockSpec((1, nr, tn), lambda l, j: (l, 0, j)),
        out_shape=jax.ShapeDtypeStruct((nl, nr, 3 * D_MODEL), F32),
        compiler_params=_params("parallel", "parallel"),
        name="ada_mod",
    )(c_all, w_mod, b_mod.reshape(nl, 1, 3 * D_MODEL))


def _gdn_front_h(x_ref, sh_ref, sc_ref, nw_ref):
    x = x_ref[...]
    h = _rms(x, nw_ref[...]) * (1.0 + sc_ref[...]) + sh_ref[...]
    return h.reshape(x.shape[0] * x.shape[1], D_MODEL).astype(BF16)


def _gdn_front_gates(hb, wba_ref, alog_ref, dtb_ref, chunk):
    r = hb.shape[0]
    ba = jnp.dot(hb, wba_ref[...], preferred_element_type=F32)
    beta = _sigmoid(ba)
    g = -jnp.exp(alog_ref[...]) * _softplus(ba + dtb_ref[...])
    ri = lax.broadcasted_iota(jnp.int32, (r, r), 0)
    ci = lax.broadcasted_iota(jnp.int32, (r, r), 1)
    same = (ri >> _log2(chunk)) == (ci >> _log2(chunk))
    low = jnp.where(same & (ri >= ci), 1.0, 0.0).astype(BF16)
    ones = jnp.where(same, 1.0, 0.0).astype(BF16)
    gc = jnp.zeros((r, LANES), F32)
    gl = jnp.zeros((r, LANES), F32)
    for piece in _split_bf16(g, 3):
        gc = gc + jnp.dot(low, piece, preferred_element_type=F32)
        gl = gl + jnp.dot(ones, piece, preferred_element_type=F32)
    return jnp.concatenate([beta, gc, gl], axis=-1)


def _gdn_front_body(x_ref, sh_ref, sc_ref, nw_ref, wq_ref, wba_ref, alog_ref, dtb_ref,
                    qkv_ref, z_ref, gt_ref, *, chunk):
    nb, tt, _ = x_ref.shape
    hb = _gdn_front_h(x_ref, sh_ref, sc_ref, nw_ref)
    pq = jnp.dot(hb, wq_ref[...], preferred_element_type=F32)
    qkv_ref[...] = pq[:, :CONV_DIM].reshape(nb, tt, CONV_DIM)
    z_ref[...] = pq[:, CONV_DIM:].reshape(nb, tt, QK_DIM)
    gt_ref[...] = _gdn_front_gates(hb, wba_ref, alog_ref, dtb_ref, chunk).reshape(nb, tt, 3 * LANES)


def _gdn_front(x, shift, scale, nw, wq, wba, alog, dtb, *, nb, tt, chunk):
    b, l, _ = x.shape
    tok = lambda w: pl.BlockSpec((nb, tt, w), lambda i, j: (i, j, 0))
    mod = lambda col: pl.BlockSpec((nb, 1, D_MODEL), lambda i, j: (i, 0, col))
    full = _resident
    return pl.pallas_call(
        functools.partial(_gdn_front_body, chunk=chunk),
        grid=(b // nb, l // tt),
        in_specs=[tok(D_MODEL), mod(0), mod(1), full(nw), full(wq), full(wba), full(alog), full(dtb)],
        out_specs=[tok(CONV_DIM), tok(QK_DIM), tok(3 * LANES)],
        out_shape=[jax.ShapeDtypeStruct((b, l, CONV_DIM), F32),
                   jax.ShapeDtypeStruct((b, l, QK_DIM), F32),
                   jax.ShapeDtypeStruct((b, l, 3 * LANES), F32)],
        compiler_params=_params("parallel", "parallel"),
        name="gdn_front",
    )(x, shift, scale, nw, wq, wba, alog, dtb)


def _inv_unit_lower(neg_a, ri, ci, chunk):
    base = min(16, chunk)
    same = lambda size: (ri >> _log2(size)) == (ci >> _log2(size))
    n = jnp.where(same(base), neg_a, 0.0)
    q, m, k = n, n, 1
    while 2 * k < base:
        m = _bmm(m, m)
        q = q + m + _bmm(q, m)
        k *= 2
    size = base
    while size < chunk:
        e = jnp.where(same(2 * size) & jnp.logical_not(same(size)), neg_a, 0.0)
        x = e + _bmm(q, e)
        q = q + (x + _bmm(x, q))
        size *= 2
    return q


def _gdn_delta_body(*refs, nb, tt, chunk, has_state, fused):
    it = iter(refs)
    take = lambda n: [next(it) for _ in range(n)]
    if fused:
        x_ref, sh_ref, sc_ref, nw_ref, wq_ref, wba_ref, alog_ref, dtb_ref = take(8)
    else:
        qkv_ref, z_ref, gt_ref = take(3)
    cw_ref, on_ref = take(2)
    if has_state:
        s0_ref, c0_ref = take(2)
    o_ref, s_ref, cn_ref, xc, qkv_s, o_s, u_s, wq_s, qk_s, kdt_s = take(10)
    if fused:
        z_ref, gt_ref = take(2)
        hb = _gdn_front_h(x_ref, sh_ref, sc_ref, nw_ref)
        pq = jnp.dot(hb, wq_ref[...], preferred_element_type=F32)
        z_ref[...] = pq[:, CONV_DIM:].reshape(nb, tt, QK_DIM)
        gt_ref[...] = _gdn_front_gates(hb, wba_ref, alog_ref, dtb_ref, chunk).reshape(nb, tt, 3 * LANES)
        qkv_in = pq[:, :CONV_DIM].reshape(nb, tt, CONV_DIM)
    else:
        qkv_in = qkv_ref[...]
    hist = CONV_TAPS - 1
    pad = CONV_PAD

    @pl.when(pl.program_id(1) == 0)
    def _():
        xc[:, 0:pad - hist, :] = jnp.zeros((nb, pad - hist, CONV_DIM), F32)
        if has_state:
            s_ref[...] = s0_ref[...]
            xc[:, pad - hist:pad, :] = c0_ref[...]
        else:
            s_ref[...] = jnp.zeros(s_ref.shape, F32)
            xc[:, pad - hist:pad, :] = jnp.zeros((nb, hist, CONV_DIM), F32)

    xc[:, pad:pad + tt, :] = qkv_in
    cw = cw_ref[...]
    lead = SUBLANES

    def conv_rows(r0, r1):
        x_cur = xc[:, pad + r0 - lead:pad + r1, :]
        x_prev = xc[:, pad + r0 - lead - 1:pad + r1 - 1, :]
        z = x_cur * cw[1:2] + x_prev * cw[0:1]
        y = x_cur[:, lead:] * cw[3:4] + x_prev[:, lead:] * cw[2:3] + z[:, lead - 2:lead - 2 + r1 - r0]
        qkv_s[:, r0:r1, :] = _silu(y)
        for hd in range(N_HEADS):
            for part, scl in ((0, HEAD_DIM ** -0.5), (1, 1.0)):
                lanes = slice(part * QK_DIM + hd * HEAD_DIM, part * QK_DIM + (hd + 1) * HEAD_DIM)
                xh = qkv_s[:, r0:r1, lanes]
                xh = xh * (lax.rsqrt(jnp.sum(xh * xh, axis=-1, keepdims=True) + EPS) * scl)
                qkv_s[:, r0:r1, lanes] = xh


    per_stack = MXU_DIM // chunk
    units_all = [(s, hd) for s in range(nb) for hd in range(N_HEADS)]
    assert len(units_all) % per_stack == 0
    stacks = [units_all[i:i + per_stack] for i in range(0, len(units_all), per_stack)]
    r = MXU_DIM
    ri = lax.broadcasted_iota(jnp.int32, (r, r), 0)
    ci = lax.broadcasted_iota(jnp.int32, (r, r), 1)
    same = (ri >> _log2(chunk)) == (ci >> _log2(chunk))
    tril = same & (ri >= ci)
    strict = same & (ri > ci)
    eye = ri == ci

    n_chunks = tt // chunk
    dot = lambda p, q: jnp.dot(p, q, preferred_element_type=F32)

    pairs = [(c, units) for c in range(n_chunks) for units in stacks]
    rows_done = 0
    for first in range(0, len(pairs), LOCKSTEP_STACKS):
        group = pairs[first:first + LOCKSTEP_STACKS]
        ns = len(group)
        out = slice(first, first + ns)
        rows_needed = (group[-1][0] + 1) * chunk
        if rows_needed > rows_done:
            conv_rows(rows_done, rows_needed)
            rows_done = rows_needed

        def gather(f):
            return jnp.stack([jnp.concatenate([f(s, hd, slice(c * chunk, (c + 1) * chunk)) for s, hd in units], axis=0)
                              for c, units in group], axis=0)

        qn = gather(lambda s, hd, rows: qkv_s[s, rows, hd * HEAD_DIM:(hd + 1) * HEAD_DIM])
        kn = gather(lambda s, hd, rows: qkv_s[s, rows, QK_DIM + hd * HEAD_DIM:QK_DIM + (hd + 1) * HEAD_DIM])
        v = gather(lambda s, hd, rows: qkv_s[s, rows, 2 * QK_DIM + hd * HEAD_DIM:2 * QK_DIM + (hd + 1) * HEAD_DIM])
        beta = gather(lambda s, hd, rows: _lane_col(gt_ref[s, rows, 0:LANES], hd))
        gc = gather(lambda s, hd, rows: _lane_col(gt_ref[s, rows, LANES:2 * LANES], N_HEADS + hd))
        gl = gather(lambda s, hd, rows: _lane_col(gt_ref[s, rows, 2 * LANES:3 * LANES], N_HEADS + hd))
        eg = jnp.exp(gc)
        kb = kn * beta
        gc_row = jnp.sum(jnp.where(eye, gc, 0.0), axis=1, keepdims=True)
        diff = gc - gc_row
        decay = jnp.exp(jnp.where(tril, diff, -jnp.inf))
        decay_strict = jnp.exp(jnp.where(strict, diff, -jnp.inf))
        q_inv = _inv_unit_lower(_bmm_nt(-kb, kn) * decay_strict, ri, ci, chunk)
        rhs = jnp.concatenate([v * beta, kb * eg], axis=2)
        uw = rhs + _bmm(q_inv, rhs)
        u_s[out] = uw[:, :, :HEAD_DIM]
        per_unit = lambda t: t.reshape(ns, per_stack, chunk, HEAD_DIM)
        wq_s[out] = jnp.concatenate([per_unit(uw[:, :, HEAD_DIM:]), per_unit(qn * eg)], axis=2).reshape(
            ns, 2 * r, HEAD_DIM).astype(BF16)
        qk_s[out] = (_bmm_nt(qn, kn) * decay).astype(BF16)
        kdt_s[out] = jnp.swapaxes(kn * jnp.exp(gl - gc), 1, 2).astype(BF16)

        _gdn_sequential(first, group, chunk, gt_ref, s_ref, o_s, u_s, wq_s, qk_s, kdt_s)

    last = xc[:, pad + tt - hist:pad + tt, :]
    cn_ref[...] = last
    xc[:, pad - hist:pad, :] = last

    on = on_ref[...]
    for hd in range(N_HEADS):
        sl = slice(hd * HEAD_DIM, (hd + 1) * HEAD_DIM)
        o_ref[:, :, sl] = _rms(o_s[:, :, sl], on) * _silu(z_ref[:, :, sl])


def _gdn_sequential(first, group, chunk, gt_ref, s_ref, o_s, u_s, wq_s, qk_s, kdt_s):
    dot = lambda p, q: jnp.dot(p, q, preferred_element_type=F32)
    unit_of_row = lax.broadcasted_iota(jnp.int32, (MXU_DIM, HEAD_DIM), 0) >> _log2(chunk)
    for idx, (c, units) in enumerate(group, start=first):
        rows = slice(c * chunk, (c + 1) * chunk)
        ws, qs = [], []
        for i, (s, hd) in enumerate(units):
            both = dot(wq_s[idx, 2 * i * chunk:2 * (i + 1) * chunk, :], s_ref[s, hd].astype(BF16))
            ws.append(both[:chunk])
            qs.append(both[chunk:])
        v_new = (u_s[idx] - jnp.concatenate(ws, axis=0)).astype(BF16)
        o = jnp.concatenate(qs, axis=0) + dot(qk_s[idx], v_new)
        kdt = kdt_s[idx]
        for i, (s, hd) in enumerate(units):
            o_s[s, rows, hd * HEAD_DIM:(hd + 1) * HEAD_DIM] = o[i * chunk:(i + 1) * chunk]
            g_last = jnp.exp(_lane_col(gt_ref[s, c * chunk:c * chunk + 1, 2 * LANES:3 * LANES], N_HEADS + hd))
            upd = dot(kdt, jnp.where(unit_of_row == i, v_new, jnp.zeros_like(v_new)))
            s_ref[s, hd] = s_ref[s, hd] * g_last + upd


def _gdn_delta(front, cw, onorm, s0, c0, *, dims, nb, tt, chunk, fused):
    b, l = dims
    has_state = s0 is not None
    n_stacks = nb * tt * N_HEADS // MXU_DIM
    tok = lambda w: pl.BlockSpec((nb, tt, w), lambda i, j: (i, j, 0))
    mod = lambda col: pl.BlockSpec((nb, 1, D_MODEL), lambda i, j: (i, 0, col))
    full = _resident
    st_spec = pl.BlockSpec((nb, N_HEADS, HEAD_DIM, HEAD_DIM), lambda i, j: (i, 0, 0, 0))
    cv_spec = pl.BlockSpec((nb, CONV_TAPS - 1, CONV_DIM), lambda i, j: (i, 0, 0))
    if fused:
        in_specs = [tok(D_MODEL), mod(0), mod(1)] + [full(a) for a in front[3:]]
    else:
        in_specs = [tok(CONV_DIM), tok(QK_DIM), tok(3 * LANES)]
    in_specs += [full(cw), full(onorm)]
    args = list(front) + [cw, onorm]
    if has_state:
        in_specs += [st_spec, cv_spec]
        args += [s0, c0]
    scratch = [pltpu.VMEM((nb, tt + CONV_PAD, CONV_DIM), F32),
               pltpu.VMEM((nb, tt, CONV_DIM), F32),
               pltpu.VMEM((nb, tt, QK_DIM), F32),
               pltpu.VMEM((n_stacks, MXU_DIM, HEAD_DIM), F32),
               pltpu.VMEM((n_stacks, 2 * MXU_DIM, HEAD_DIM), BF16),
               pltpu.VMEM((n_stacks, MXU_DIM, MXU_DIM), BF16),
               pltpu.VMEM((n_stacks, HEAD_DIM, MXU_DIM), BF16)]
    if fused:
        scratch += [pltpu.VMEM((nb, tt, QK_DIM), F32), pltpu.VMEM((nb, tt, 3 * LANES), F32)]
    return pl.pallas_call(
        functools.partial(_gdn_delta_body, nb=nb, tt=tt, chunk=chunk, has_state=has_state, fused=fused),
        grid=(b // nb, l // tt),
        in_specs=in_specs,
        out_specs=[tok(QK_DIM), st_spec, cv_spec],
        out_shape=[jax.ShapeDtypeStruct((b, l, QK_DIM), F32),
                   jax.ShapeDtypeStruct((b, N_HEADS, HEAD_DIM, HEAD_DIM), F32),
                   jax.ShapeDtypeStruct((b, CONV_TAPS - 1, CONV_DIM), F32)],
        scratch_shapes=scratch,
        compiler_params=_params("parallel", "arbitrary"),
        name="gdn_delta",
    )(*args)


def _route(logits):
    lane = lax.broadcasted_iota(jnp.int32, logits.shape, 1)
    neg, big = -1e30, 1 << 20
    rmax = lambda t: jnp.max(t, axis=1, keepdims=True)
    rsum = lambda t: jnp.sum(t, axis=1, keepdims=True)
    first = lambda cond: jnp.min(jnp.where(cond, lane, big), axis=1, keepdims=True)
    is_g = lane < N_GROUPS
    lg = jnp.where(is_g, logits, neg)
    mg = rmax(lg)
    g_idx = first(is_g & (lg >= mg))
    p_group = 1.0 / rsum(jnp.where(is_g, jnp.exp(lg - mg), 0.0))
    lo = N_GROUPS + EXPERTS_PER_GROUP * g_idx
    in_g = (lane >= lo) & (lane < lo + EXPERTS_PER_GROUP)
    le = jnp.where(in_g, logits, neg)
    m1 = rmax(le)
    se = rsum(jnp.where(in_g, jnp.exp(le - m1), 0.0))
    i1 = first(in_g & (le >= m1))
    le2 = jnp.where(lane == i1, neg, le)
    m2 = rmax(le2)
    i2 = first(in_g & (lane != i1) & (le2 >= m2))
    p1 = 1.0 / se
    p2 = jnp.exp(m2 - m1) / se
    tot = p1 + p2
    local = (jnp.where(lane == i1 - lo, p1 / tot * p_group, 0.0)
             + jnp.where(lane == i2 - lo, p2 / tot * p_group, 0.0))
    return g_idx, local


META_GROUP, META_RANK = 4, 5
PIECE_LANES = 8
SORT_PAD = 16
EXPERT_ROWS = 256
EXPERT_STEP = 32
SORT_ROWS = MOE_ROWS + N_GROUPS * SORT_PAD + EXPERT_ROWS
USED_ROWS = MOE_ROWS + LANES
MOE_HALVES = 2


def _moe_route_body(*refs, nb, tt, has_proj):
    it = iter(refs)
    x_ref = next(it)
    if has_proj:
        y_ref, wo_ref, gm_ref = next(it), next(it), next(it)
    sh_ref, sc_ref, nw_ref, wr_ref, br_ref, x1_ref, h_ref, meta_ref, cnt_ref = (next(it) for _ in range(9))
    r = nb * tt
    dot = lambda p, q: jnp.dot(p, q, preferred_element_type=F32)

    x = x_ref[...].reshape(nb, tt, D_MODEL)
    if has_proj:
        x = x + gm_ref[...] * _mm(y_ref[...].reshape(r, D_MODEL), wo_ref[...]).reshape(nb, tt, D_MODEL)
    h = (_rms(x, nw_ref[...]) * (1.0 + sc_ref[...]) + sh_ref[...]).reshape(r, D_MODEL)
    x1_ref[...] = x.reshape(r, D_MODEL)
    h_hi, h_lo = _split_bf16(h, 2)
    h_ref[...] = h_hi
    w_hi, w_lo = _split_bf16(wr_ref[...], 2)
    logits = dot(h_hi, w_hi) + (dot(h_hi, w_lo) + dot(h_lo, w_hi)) + br_ref[...]
    g_idx, local = _route(logits)

    lane = lax.broadcasted_iota(jnp.int32, (r, LANES), 1)
    onehot = jnp.where(lane == g_idx, 1.0, 0.0)
    ri = lax.broadcasted_iota(jnp.int32, (r, r), 0)
    ci = lax.broadcasted_iota(jnp.int32, (r, r), 1)
    earlier = jnp.where(ri > ci, 1.0, 0.0).astype(BF16)
    rank = jnp.sum(onehot * dot(earlier, onehot.astype(BF16)), axis=1, keepdims=True)
    meta_ref[...] = jnp.where(lane == META_GROUP, g_idx.astype(F32),
                              jnp.where(lane == META_RANK, rank, local))
    counts = jnp.sum(onehot, axis=0, keepdims=True).astype(jnp.int32)
    cnt_ref[...] = jnp.broadcast_to(counts, (1, SUBLANES, LANES))


def _moe_expert_body(cnt_ref, *refs, nb, tt, n_time_tiles, final_norm):
    it = iter(refs)
    x1_ref, h_ref, meta_ref, gf_ref, w1_ref, w3_ref, w2_ref = (next(it) for _ in range(7))
    if final_norm:
        nf_ref = next(it)
    out_ref, hs_s, ms_s, acc_s = (next(it) for _ in range(4))
    r = MOE_ROWS
    dot = lambda p, q: jnp.dot(p, q, preferred_element_type=F32)
    step = pl.program_id(0) * n_time_tiles + pl.program_id(1)
    grp = pl.program_id(2)

    def layout(half):
        counts = [cnt_ref[MOE_HALVES * step + half, g] for g in range(N_GROUPS)]
        starts, nxt = [], 0
        for g in range(N_GROUPS):
            starts.append(nxt)
            nxt = nxt + (((counts[g] + (SORT_PAD - 1)) >> _log2(SORT_PAD)) << _log2(SORT_PAD))
        return counts, starts

    def positions(half, starts):
        meta = meta_ref[half * r:(half + 1) * r, :]
        g_col = _lane_col(meta, META_GROUP)
        pos = _lane_col(meta, META_RANK)
        for g in range(N_GROUPS):
            pos = pos + jnp.where(g_col == g, jnp.asarray(starts[g], jnp.int32).astype(F32), 0.0)
        return meta, pos

    @pl.when(grp == 0)
    def _():
        for half in range(MOE_HALVES):
            meta, pos = positions(half, layout(half)[1])
            pos_row = jnp.broadcast_to(pos, (r, LANES)).T[0:1, :].astype(jnp.int32)
            perm = jnp.where(lax.broadcasted_iota(jnp.int32, (USED_ROWS, r), 0) == pos_row, 1.0, 0.0).astype(BF16)
            hs_s[half, 0:USED_ROWS] = dot(perm, h_ref[half * r:(half + 1) * r, :]).astype(BF16)
            hs_s[half, USED_ROWS:] = jnp.zeros((SORT_ROWS - USED_ROWS, D_MODEL), BF16)
            lane = lax.broadcasted_iota(jnp.int32, (r, LANES), 1)
            pieces = _split_bf16(jnp.where(lane < EXPERTS_PER_GROUP, meta, 0.0), 3)
            packed = pieces[0].astype(F32)
            for k in (1, 2):
                packed = packed + pltpu.roll(pieces[k].astype(F32), k * PIECE_LANES, 1)
            ms = dot(perm, packed.astype(BF16))
            ms_s[half, 0:USED_ROWS] = (ms + pltpu.roll(ms, LANES - PIECE_LANES, 1)
                                       + pltpu.roll(ms, LANES - 2 * PIECE_LANES, 1))
            ms_s[half, USED_ROWS:] = jnp.zeros((SORT_ROWS - USED_ROWS, LANES), F32)
        acc_s[...] = jnp.zeros(acc_s.shape, F32)

    pick = lambda vals: functools.reduce(lambda acc, gv: jnp.where(grp == gv[0], gv[1], acc),
                                         list(enumerate(vals))[1:], jnp.asarray(vals[0], jnp.int32))
    lay = [layout(half) for half in range(MOE_HALVES)]
    cnt_g = [pick(c) for c, _ in lay]
    start_g = [pick(s) for _, s in lay]
    def block(first, size):
        base = [jnp.minimum(s + first, SORT_ROWS - size) for s in start_g]
        rows = [pl.ds(pl.multiple_of(bs, SORT_PAD), size) for bs in base]
        iota = lax.broadcasted_iota(jnp.int32, (size, 1), 0)
        hb = jnp.concatenate([hs_s[half, rows[half], :] for half in range(MOE_HALVES)], axis=0)
        mb = []
        for half in range(MOE_HALVES):
            seg_row = iota + (base[half] - start_g[half])
            mb.append(jnp.where((seg_row >= first) & (seg_row < cnt_g[half]), ms_s[half, rows[half], :], 0.0))
        mb = jnp.concatenate(mb, axis=0)
        hid = _silu(dot(hb, w1_ref[...])) * dot(hb, w3_ref[...])
        parts = [(hid[:, j * D_FF:(j + 1) * D_FF] * _lane_col(mb, j)).astype(BF16)
                 for j in range(EXPERTS_PER_GROUP)]
        res = dot(jnp.concatenate(parts, axis=1), w2_ref[...])
        for half in range(MOE_HALVES):
            acc_s[half, rows[half], :] += res[half * size:(half + 1) * size]

    longest = functools.reduce(jnp.maximum, cnt_g)
    n_full = longest // EXPERT_ROWS

    def full_block(k, carry):
        block(k * EXPERT_ROWS, EXPERT_ROWS)
        return carry

    lax.fori_loop(0, n_full, full_block, 0)
    tail = longest - n_full * EXPERT_ROWS
    for size in range(EXPERT_STEP, EXPERT_ROWS + 1, EXPERT_STEP):
        pl.when((tail > size - EXPERT_STEP) & (tail <= size))(
            functools.partial(block, n_full * EXPERT_ROWS, size))

    @pl.when(grp == N_GROUPS - 1)
    def _():
        moes = []
        for half in range(MOE_HALVES):
            _, pos = positions(half, layout(half)[1])
            unperm = jnp.where(lax.broadcasted_iota(jnp.int32, (r, USED_ROWS), 1) == pos.astype(jnp.int32),
                               1.0, 0.0).astype(BF16)
            a_hi, a_lo = _split_bf16(acc_s[half, 0:USED_ROWS, :], 2)
            moes.append(dot(unperm, a_hi) + dot(unperm, a_lo))
        moe = jnp.concatenate(moes, axis=0)
        out = x1_ref[...].reshape(nb, tt, D_MODEL) + gf_ref[...] * moe.reshape(nb, tt, D_MODEL)
        if final_norm:
            out = _rms(out, nf_ref[...])
        out_ref[...] = out.reshape(out_ref.shape)


def _moe(x, y, wo, mod_mix, mod_ffn, nw, wr, br, w1, w3, w2, nf, *, layer, dims, nb, tt,
         in_time_major, out_time_major):
    b, l = dims
    has_proj, final_norm = y is not None, nf is not None
    gb, gt = b // nb, l // tt
    r, n_tiles = nb * tt, gb * gt
    bm = pl.BlockSpec((nb, tt, D_MODEL), lambda i, j, *_: (i, j, 0))
    tm = pl.BlockSpec((tt, D_MODEL), lambda i, j, *_: (j, i))
    mod = lambda col: pl.BlockSpec((nb, 1, D_MODEL), lambda i, j, *_: (i, 0, col))
    flat = lambda w: pl.BlockSpec((r, w), lambda i, j, *_: (i * gt + j, 0))
    x_spec = tm if in_time_major else bm
    in_specs, args = [x_spec], [x]
    if has_proj:
        in_specs += [x_spec, _resident(wo), mod(2)]
        args += [y, wo, mod_mix]
    in_specs += [mod(0), mod(1)] + [_resident(a) for a in (nw, wr, br)]
    args += [mod_ffn, mod_ffn, nw, wr, br]
    x1, h, meta, cnt = pl.pallas_call(
        functools.partial(_moe_route_body, nb=nb, tt=tt, has_proj=has_proj),
        grid=(gb, gt),
        in_specs=in_specs,
        out_specs=[flat(D_MODEL), flat(D_MODEL), flat(LANES),
                   pl.BlockSpec((1, SUBLANES, LANES), lambda i, j: (i * gt + j, 0, 0))],
        out_shape=[jax.ShapeDtypeStruct((n_tiles * r, D_MODEL), F32),
                   jax.ShapeDtypeStruct((n_tiles * r, D_MODEL), BF16),
                   jax.ShapeDtypeStruct((n_tiles * r, LANES), F32),
                   jax.ShapeDtypeStruct((n_tiles, SUBLANES, LANES), jnp.int32)],
        compiler_params=_params("parallel", "parallel"),
        name="moe_route",
    )(*args)

    nb2, tt2 = (nb, MOE_HALVES * tt) if nb == 1 else (MOE_HALVES * nb, tt)
    gb2, gt2 = b // nb2, l // tt2
    assert r == MOE_ROWS and gb2 * gt2 * MOE_HALVES == n_tiles
    gw = EXPERTS_PER_GROUP * D_FF
    bm2 = pl.BlockSpec((nb2, tt2, D_MODEL), lambda i, j, g, *_: (i, j, 0))
    tm2 = pl.BlockSpec((tt2, D_MODEL), lambda i, j, g, *_: (j, i))
    flat2 = lambda w: pl.BlockSpec((MOE_HALVES * r, w), lambda i, j, g, *_: (i * gt2 + j, 0))
    in_specs = [flat2(D_MODEL), flat2(D_MODEL), flat2(LANES),
                pl.BlockSpec((nb2, 1, D_MODEL), lambda i, j, g, *_: (i, 0, 2)),
                pl.BlockSpec((None, D_MODEL, gw), lambda i, j, g, *_: (layer, 0, g)),
                pl.BlockSpec((None, D_MODEL, gw), lambda i, j, g, *_: (layer, 0, g)),
                pl.BlockSpec((None, gw, D_MODEL), lambda i, j, g, *_: (layer, g, 0))]
    args = [x1, h, meta, mod_ffn, w1, w3, w2]
    if final_norm:
        in_specs.append(_resident(nf))
        args.append(nf)
    if out_time_major:
        assert nb == 1
        out_spec, out_shape = tm2, jax.ShapeDtypeStruct((l, b * D_MODEL), F32)
    else:
        out_spec, out_shape = bm2, jax.ShapeDtypeStruct((b, l, D_MODEL), F32)
    return pl.pallas_call(
        functools.partial(_moe_expert_body, nb=nb2, tt=tt2, n_time_tiles=gt2, final_norm=final_norm),
        grid_spec=pltpu.PrefetchScalarGridSpec(
            num_scalar_prefetch=1,
            grid=(gb2, gt2, N_GROUPS),
            in_specs=in_specs,
            out_specs=out_spec,
            scratch_shapes=[pltpu.VMEM((MOE_HALVES, SORT_ROWS, D_MODEL), BF16),
                            pltpu.VMEM((MOE_HALVES, SORT_ROWS, LANES), F32),
                            pltpu.VMEM((MOE_HALVES, SORT_ROWS, D_MODEL), F32)]),
        out_shape=out_shape,
        compiler_params=_params("parallel", "parallel", "arbitrary"),
        name="moe_experts",
    )(cnt[:, 0, :N_GROUPS], *args)


def _s5_body(*refs, tt, nbs, has_state):
    if has_state:
        (x_ref, sh_ref, sc_ref, gm_ref, nw_ref, win_ref, wbr_ref, wbi_ref, wcr_ref, wci_ref, dsk_ref,
         wglu_ref, abr_ref, abi_ref, s0r_ref, s0i_ref, out_ref, sr_ref, si_ref, xr_s, xi_s) = refs
    else:
        (x_ref, sh_ref, sc_ref, gm_ref, nw_ref, win_ref, wbr_ref, wbi_ref, wcr_ref, wci_ref, dsk_ref,
         wglu_ref, abr_ref, abi_ref, out_ref, sr_ref, si_ref, xr_s, xi_s) = refs
    r = tt * nbs
    blocks = D_MODEL // MXU_DIM
    sw = SSM_STATE // blocks

    @pl.when(pl.program_id(1) == 0)
    def _():
        if has_state:
            sr_ref[...] = s0r_ref[...]
            si_ref[...] = s0i_ref[...]
        else:
            sr_ref[...] = jnp.zeros(sr_ref.shape, F32)
            si_ref[...] = jnp.zeros(si_ref.shape, F32)

    x = x_ref[...]
    h = (_rms(x, nw_ref[...]) * (1.0 + sc_ref[...]) + sh_ref[...]).reshape(r, D_MODEL)
    u = _mm(h, win_ref[...])
    ub = u.astype(BF16)
    for j in range(blocks):
        uj = ub[:, j * MXU_DIM:(j + 1) * MXU_DIM]
        xr_s[:, :, j * sw:(j + 1) * sw] = jnp.dot(
            uj, wbr_ref[j], preferred_element_type=F32).reshape(tt, nbs, sw)
        xi_s[:, :, j * sw:(j + 1) * sw] = jnp.dot(
            uj, wbi_ref[j], preferred_element_type=F32).reshape(tt, nbs, sw)

    for rg in range(nbs // SUBLANES):
        rows = slice(rg * SUBLANES, (rg + 1) * SUBLANES)
        for ch in range(SSM_STATE // SCAN_LANES):
            lanes = slice(ch * SCAN_LANES, (ch + 1) * SCAN_LANES)
            ar = jnp.broadcast_to(abr_ref[:, lanes], (SUBLANES, SCAN_LANES))
            ai = jnp.broadcast_to(abi_ref[:, lanes], (SUBLANES, SCAN_LANES))

            hr, hi = sr_ref[rows, lanes], si_ref[rows, lanes]
            for t in range(tt):
                hr, hi = (ar * hr - ai * hi + xr_s[t, rows, lanes],
                          ar * hi + ai * hr + xi_s[t, rows, lanes])
                xr_s[t, rows, lanes] = hr
                xi_s[t, rows, lanes] = hi
            sr_ref[rows, lanes] = hr
            si_ref[rows, lanes] = hi

    ys = []
    for j in range(blocks):
        st_r = xr_s[:, :, j * sw:(j + 1) * sw].reshape(r, sw)
        st_i = xi_s[:, :, j * sw:(j + 1) * sw].reshape(r, sw)
        ys.append(_mm(st_r, wcr_ref[j]) - _mm(st_i, wci_ref[j]))
    y = jnp.concatenate(ys, axis=1) + dsk_ref[...] * u
    ag = _mm(jax.nn.gelu(y), wglu_ref[...])
    mix = ag[:, :D_MODEL] * _sigmoid(ag[:, D_MODEL:])
    out_ref[...] = x + gm_ref[...] * mix.reshape(tt, nbs, D_MODEL)


def _s5(x, mod, nw, win, wbr, wbi, wcr, wci, dsk, wglu, abr, abi, s0r, s0i, *, tt, nbs):
    l, b, _ = x.shape
    has_state = s0r is not None
    xs = pl.BlockSpec((tt, nbs, D_MODEL), lambda i, j: (j, i, 0))
    mods = lambda col: pl.BlockSpec((nbs, D_MODEL), lambda i, j: (i, col))
    full = _resident
    st = pl.BlockSpec((nbs, SSM_STATE), lambda i, j: (i, 0))
    consts = [nw, win, wbr, wbi, wcr, wci, dsk, wglu, abr, abi]
    in_specs = [xs, mods(0), mods(1), mods(2)] + [full(a) for a in consts]
    args = [x, mod, mod, mod] + consts
    if has_state:
        in_specs += [st, st]
        args += [s0r, s0i]
    return pl.pallas_call(
        functools.partial(_s5_body, tt=tt, nbs=nbs, has_state=has_state),
        grid=(b // nbs, l // tt),
        in_specs=in_specs,
        out_specs=[xs, st, st],
        out_shape=[jax.ShapeDtypeStruct((l, b, D_MODEL), F32),
                   jax.ShapeDtypeStruct((b, SSM_STATE), F32),
                   jax.ShapeDtypeStruct((b, SSM_STATE), F32)],
        scratch_shapes=[pltpu.VMEM((tt, nbs, SSM_STATE), F32), pltpu.VMEM((tt, nbs, SSM_STATE), F32)],
        compiler_params=_params("parallel", "arbitrary"),
        name="s5",
    )(*args)


def _s5_discretize(lam_re, lam_im, log_dt, b_re, b_im, c_re, c_im):
    dt = jnp.exp(log_dt)[:, None]
    mag = jnp.exp(lam_re * dt)
    ang = lam_im * dt
    ab_re, ab_im = mag * jnp.cos(ang), mag * jnp.sin(ang)
    den = lam_re * lam_re + lam_im * lam_im
    f_re = ((ab_re - 1.0) * lam_re + ab_im * lam_im) / den
    f_im = (ab_im * lam_re - (ab_re - 1.0) * lam_im) / den
    bb_re = f_re[..., None] * b_re - f_im[..., None] * b_im
    bb_im = f_re[..., None] * b_im + f_im[..., None] * b_re
    blocks = D_MODEL // MXU_DIM
    gpb = SSM_GROUPS // blocks

    def diag_blocks(t, rows_per_group, cols_per_group):
        tiled = jnp.tile(t.reshape(blocks, gpb * rows_per_group, cols_per_group), (1, 1, gpb))
        rg = lax.broadcasted_iota(jnp.int32, tiled.shape, 1) // rows_per_group
        cg = lax.broadcasted_iota(jnp.int32, tiled.shape, 2) // cols_per_group
        return jnp.where(rg == cg, tiled, 0.0).astype(BF16)

    b_blocks = lambda bb: diag_blocks(jnp.swapaxes(bb, 1, 2), SSM_GROUP, SSM_P)
    c_blocks = lambda cc: diag_blocks(jnp.swapaxes(cc, 1, 2), SSM_P, SSM_GROUP)

    return (ab_re.reshape(1, SSM_STATE), ab_im.reshape(1, SSM_STATE),
            b_blocks(bb_re), b_blocks(bb_im), c_blocks(c_re), c_blocks(c_im))


def _trunk(x, mods_mix, mods_ffn, conv0, delta0, re0, im0, wts, *, nb_tok, tt_tok, nb_moe, tt_moe, nb_delta, tt_delta, chunk,
           tt_s5, nbs_s5):
    b, l, _ = x.shape
    mm3 = [m.reshape(b, 1, 3 * D_MODEL) for m in mods_mix]
    mf3 = [m.reshape(b, 1, 3 * D_MODEL) for m in mods_ffn]

    front = (x, mm3[0], mm3[0], wts['norm_mix'][0], wts['gdn_wq'], wts['gdn_wba'], wts['gdn_alog'], wts['gdn_dtb'])
    fused = nb_delta * tt_delta >= ROWS
    if not fused:
        front = _gdn_front(*front, nb=nb_tok, tt=tt_tok, chunk=chunk)
    o, s_delta, s_conv = _gdn_delta(front, wts['gdn_cw'], wts['gdn_onorm'], delta0, conv0, dims=(b, l),
                                    nb=nb_delta, tt=tt_delta, chunk=chunk, fused=fused)
    prompt_like = nb_tok == 1
    moe_w = lambda i: (wts['norm_ffn'][i], wts['moe_wr'][i], wts['moe_br'][i],
                       wts['moe_w1'], wts['moe_w3'], wts['moe_w2'])
    x1 = _moe(x, o, wts['gdn_wout'], mm3[0], mf3[0], *moe_w(0), None, layer=0, dims=(b, l), nb=nb_moe, tt=tt_moe,
              in_time_major=False, out_time_major=prompt_like)
    xt = x1.reshape(l, b, D_MODEL) if prompt_like else jnp.swapaxes(x1, 0, 1)
    x2, s_re, s_im = _s5(xt, mods_mix[1], wts['norm_mix'][1], wts['s5_win'], wts['s5_wbr'], wts['s5_wbi'],
                         wts['s5_wcr'], wts['s5_wci'], wts['s5_d'], wts['s5_wglu'], wts['s5_abr'], wts['s5_abi'],
                         re0, im0, tt=tt_s5, nbs=nbs_s5)
    x2 = x2.reshape(l, b * D_MODEL) if prompt_like else jnp.swapaxes(x2, 0, 1)
    y = _moe(x2, None, None, None, mf3[1], *moe_w(1), wts['norm_final'], layer=1, dims=(b, l), nb=nb_moe, tt=tt_moe,
             in_time_major=prompt_like, out_time_major=False)
    return (y, s_delta[None], s_conv[None],
            s_re.reshape(1, b, SSM_GROUPS, SSM_P), s_im.reshape(1, b, SSM_GROUPS, SSM_P))


def kernel(x_prompt, x_sample, state_delta, state_conv, state_ssm_re, state_ssm_im, c_prompt, c_sample, norm_mix, w_mod_mix, b_mod_mix, norm_ffn, w_mod_ffn, b_mod_ffn, norm_final, gdn_w_in, gdn_conv_w, gdn_a_log, gdn_dt_bias, gdn_o_norm, gdn_w_out, s5_w_in, s5_lam_re, s5_lam_im, s5_log_dt, s5_b_re, s5_b_im, s5_c_re, s5_c_im, s5_d, s5_w_glu, moe_w_rg, moe_b_rg, moe_w_re, moe_b_re, moe_w1, moe_w3, moe_w2):
    bp = x_prompt.shape[0]
    depth = norm_mix.shape[0]

    lane_pad = lambda a, lo: jnp.pad(a, [(0, 0)] * (a.ndim - 1) + [(lo, LANES - lo - a.shape[-1])])
    abr, abi, wbr, wbi, wcr, wci = _s5_discretize(s5_lam_re[0], s5_lam_im[0], s5_log_dt[0], s5_b_re[0],
                                                  s5_b_im[0], s5_c_re[0], s5_c_im[0])
    wts = dict(
        norm_mix=norm_mix.reshape(depth, 1, D_MODEL), norm_ffn=norm_ffn.reshape(depth, 1, D_MODEL),
        norm_final=norm_final.reshape(1, D_MODEL),
        gdn_wq=gdn_w_in[0, :, :CONV_DIM + QK_DIM].astype(BF16),
        gdn_wba=lane_pad(gdn_w_in[0, :, CONV_DIM + QK_DIM:], 0).astype(BF16),
        gdn_alog=lane_pad(gdn_a_log[0][None], N_HEADS), gdn_dtb=lane_pad(gdn_dt_bias[0][None], N_HEADS),
        gdn_cw=jnp.pad(gdn_conv_w[0], ((0, SUBLANES - CONV_TAPS), (0, 0))),
        gdn_onorm=gdn_o_norm[0][None], gdn_wout=gdn_w_out[0].astype(BF16),
        s5_win=s5_w_in[0].astype(BF16), s5_wbr=wbr, s5_wbi=wbi, s5_wcr=wcr, s5_wci=wci,
        s5_d=s5_d[0].reshape(1, D_MODEL), s5_wglu=s5_w_glu[0].astype(BF16), s5_abr=abr, s5_abi=abi,
        moe_wr=lane_pad(jnp.concatenate([moe_w_rg, moe_w_re], axis=-1), 0),
        moe_br=lane_pad(jnp.concatenate([moe_b_rg, moe_b_re], axis=-1), 0)[:, None, :],
        moe_w1=jnp.swapaxes(moe_w1, 1, 2).reshape(depth, D_MODEL, N_EXPERTS * D_FF).astype(BF16),
        moe_w3=jnp.swapaxes(moe_w3, 1, 2).reshape(depth, D_MODEL, N_EXPERTS * D_FF).astype(BF16),
        moe_w2=moe_w2.reshape(depth, N_EXPERTS * D_FF, D_MODEL).astype(BF16),
    )

    c_all = jnp.concatenate([c_prompt, c_sample], axis=0)
    m_mix = _ada_mod(c_all, w_mod_mix, b_mod_mix)
    m_ffn = _ada_mod(c_all, w_mod_ffn, b_mod_ffn)

    y_p, p_delta, p_conv, p_re, p_im = _trunk(
        x_prompt, [m_mix[i, :bp] for i in range(depth)], [m_ffn[i, :bp] for i in range(depth)],
        None, None, None, None, wts,
        nb_tok=1, tt_tok=ROWS, nb_moe=1, tt_moe=MOE_ROWS, nb_delta=1, tt_delta=ROWS, chunk=CHUNK, tt_s5=S5_ROWS // SUBLANES, nbs_s5=SUBLANES)
    ls = x_sample.shape[1]
    y_s, s_delta, s_conv, s_re, s_im = _trunk(
        x_sample, [m_mix[i, bp:] for i in range(depth)], [m_ffn[i, bp:] for i in range(depth)],
        state_conv[0], state_delta[0], state_ssm_re[0].reshape(-1, SSM_STATE),
        state_ssm_im[0].reshape(-1, SSM_STATE), wts,
        nb_tok=ROWS // ls, tt_tok=ls, nb_moe=MOE_ROWS // ls, tt_moe=ls, nb_delta=2 * MXU_DIM // (ls * N_HEADS), tt_delta=ls, chunk=ls,
        tt_s5=ls, nbs_s5=ROWS // ls)
    return (y_p, y_s, p_delta, p_conv, p_re, p_im, s_delta, s_conv, s_re, s_im)
```

```python
import functools

import jax
import jax.numpy as jnp
from jax import lax
from jax.experimental import pallas as pl
from jax.experimental.pallas import tpu as pltpu

F32 = jnp.float32
BF16 = jnp.bfloat16

D_MODEL = 1024
N_HEADS = 8
HEAD_DIM = 128
QK_DIM = N_HEADS * HEAD_DIM
CONV_DIM = 3 * QK_DIM
CONV_TAPS = 4
CONV_PAD = 16
CHUNK = 64
SSM_GROUPS = 64
SSM_GROUP = 16
SSM_P = 64
SSM_STATE = SSM_GROUPS * SSM_P
N_GROUPS = 4
EXPERTS_PER_GROUP = 4
N_EXPERTS = 16
D_FF = 256
EPS = 1e-6

LANES = 128
SUBLANES = 8
MXU_DIM = 256
ROWS = 256
MOE_ROWS = 512
S5_ROWS = 512
LOCKSTEP_STACKS = 4
SCAN_LANES = 512
VMEM_LIMIT = 56 * 1024 * 1024


def _mm(a, b):
    return jnp.dot(a.astype(BF16), b.astype(BF16), preferred_element_type=F32)


def _bmm(a, b):
    return lax.dot_general(a.astype(BF16), b.astype(BF16), (((2,), (1,)), ((0,), (0,))),
                           preferred_element_type=F32)


def _bmm_nt(a, b):
    return lax.dot_general(a.astype(BF16), b.astype(BF16), (((2,), (2,)), ((0,), (0,))),
                           preferred_element_type=F32)


def _split_bf16(x, terms):
    out = []
    for _ in range(terms - 1):
        p = x.astype(BF16)
        out.append(p)
        x = x - p.astype(F32)
    out.append(x.astype(BF16))
    return out


def _rms(x, w):
    return x * lax.rsqrt(jnp.mean(x * x, axis=-1, keepdims=True) + EPS) * w


def _sigmoid(x):
    return 0.5 * jnp.tanh(0.5 * x) + 0.5


def _silu(x):
    h = 0.5 * x
    return h * jnp.tanh(h) + h


def _softplus(x):
    return jnp.maximum(x, 0.0) + jnp.log1p(jnp.exp(-jnp.abs(x)))


def _lane_col(tile, lane):
    li = lax.broadcasted_iota(jnp.int32, tile.shape, 1)
    return jnp.sum(jnp.where(li == lane, tile, 0.0), axis=1, keepdims=True)


def _log2(n):
    assert n & (n - 1) == 0
    return n.bit_length() - 1


def _params(*sem, fuse_inputs=None):
    return pltpu.CompilerParams(dimension_semantics=sem, vmem_limit_bytes=VMEM_LIMIT,
                                allow_input_fusion=fuse_inputs)


def _resident(a):
    return pl.BlockSpec(a.shape, lambda *_: (0,) * a.ndim, pipeline_mode=pl.Buffered(1))


def _mod_body(c_ref, w_ref, b_ref, o_ref):
    o_ref[0] = _mm(_silu(c_ref[...]), w_ref[0]) + b_ref[0]


def _ada_mod(c_all, w_mod, b_mod):
    nl, nr, tn = w_mod.shape[0], c_all.shape[0], 768
    return pl.pallas_call(
        _mod_body,
        grid=(nl, 3 * D_MODEL // tn),
        in_specs=[pl.BlockSpec((nr, D_MODEL), lambda l, j: (0, 0)),
                  pl.BlockSpec((1, D_MODEL, tn), lambda l, j: (l, 0, j)),
                  pl.BlockSpec((1, 1, tn), lambda l, j: (l, 0, j))],
        out_specs=pl.BlockSpec((1, nr, tn), lambda l, j: (l, 0, j)),
        out_shape=jax.ShapeDtypeStruct((nl, nr, 3 * D_MODEL), F32),
        compiler_params=_params("parallel", "parallel"),
        name="ada_mod",
    )(c_all, w_mod, b_mod.reshape(nl, 1, 3 * D_MODEL))


def _gdn_front_h(x_ref, sh_ref, sc_ref, nw_ref):
    x = x_ref[...]
    h = _rms(x, nw_ref[...]) * (1.0 + sc_ref[...]) + sh_ref[...]
    return h.reshape(x.shape[0] * x.shape[1], D_MODEL).astype(BF16)


def _gdn_front_gates(hb, wba_ref, alog_ref, dtb_ref, chunk):
    r = hb.shape[0]
    ba = jnp.dot(hb, wba_ref[...], preferred_element_type=F32)
    beta = _sigmoid(ba)
    g = -jnp.exp(alog_ref[...]) * _softplus(ba + dtb_ref[...])
    ri = lax.broadcasted_iota(jnp.int32, (r, r), 0)
    ci = lax.broadcasted_iota(jnp.int32, (r, r), 1)
    same = (ri >> _log2(chunk)) == (ci >> _log2(chunk))
    low = jnp.where(same & (ri >= ci), 1.0, 0.0).astype(BF16)
    ones = jnp.where(same, 1.0, 0.0).astype(BF16)
    gc = jnp.zeros((r, LANES), F32)
    gl = jnp.zeros((r, LANES), F32)
    for piece in _split_bf16(g, 3):
        gc = gc + jnp.dot(low, piece, preferred_element_type=F32)
        gl = gl + jnp.dot(ones, piece, preferred_element_type=F32)
    return jnp.concatenate([beta, gc, gl], axis=-1)


def _gdn_front_body(x_ref, sh_ref, sc_ref, nw_ref, wq_ref, wba_ref, alog_ref, dtb_ref,
                    qkv_ref, z_ref, gt_ref, *, chunk):
    nb, tt, _ = x_ref.shape
    hb = _gdn_front_h(x_ref, sh_ref, sc_ref, nw_ref)
    pq = jnp.dot(hb, wq_ref[...], preferred_element_type=F32)
    qkv_ref[...] = pq[:, :CONV_DIM].reshape(nb, tt, CONV_DIM)
    z_ref[...] = pq[:, CONV_DIM:].reshape(nb, tt, QK_DIM)
    gt_ref[...] = _gdn_front_gates(hb, wba_ref, alog_ref, dtb_ref, chunk).reshape(nb, tt, 3 * LANES)


def _gdn_front(x, shift, scale, nw, wq, wba, alog, dtb, *, nb, tt, chunk):
    b, l, _ = x.shape
    tok = lambda w: pl.BlockSpec((nb, tt, w), lambda i, j: (i, j, 0))
    mod = lambda col: pl.BlockSpec((nb, 1, D_MODEL), lambda i, j: (i, 0, col))
    full = _resident
    return pl.pallas_call(
        functools.partial(_gdn_front_body, chunk=chunk),
        grid=(b // nb, l // tt),
        in_specs=[tok(D_MODEL), mod(0), mod(1), full(nw), full(wq), full(wba), full(alog), full(dtb)],
        out_specs=[tok(CONV_DIM), tok(QK_DIM), tok(3 * LANES)],
        out_shape=[jax.ShapeDtypeStruct((b, l, CONV_DIM), F32),
                   jax.ShapeDtypeStruct((b, l, QK_DIM), F32),
                   jax.ShapeDtypeStruct((b, l, 3 * LANES), F32)],
        compiler_params=_params("parallel", "parallel"),
        name="gdn_front",
    )(x, shift, scale, nw, wq, wba, alog, dtb)


def _inv_unit_lower(neg_a, ri, ci, chunk):
    base = min(16, chunk)
    same = lambda size: (ri >> _log2(size)) == (ci >> _log2(size))
    n = jnp.where(same(base), neg_a, 0.0)
    q, m, k = n, n, 1
    while 2 * k < base:
        m = _bmm(m, m)
        q = q + m + _bmm(q, m)
        k *= 2
    size = base
    while size < chunk:
        e = jnp.where(same(2 * size) & jnp.logical_not(same(size)), neg_a, 0.0)
        x = e + _bmm(q, e)
        q = q + (x + _bmm(x, q))
        size *= 2
    return q


def _gdn_delta_body(*refs, nb, tt, chunk, has_state, fused):
    it = iter(refs)
    take = lambda n: [next(it) for _ in range(n)]
    if fused:
        x_ref, sh_ref, sc_ref, nw_ref, wq_ref, wba_ref, alog_ref, dtb_ref = take(8)
    else:
        qkv_ref, z_ref, gt_ref = take(3)
    cw_ref, on_ref = take(2)
    if has_state:
        s0_ref, c0_ref = take(2)
    o_ref, s_ref, cn_ref, xc, qkv_s, o_s, u_s, wq_s, qk_s, kdt_s = take(10)
    if fused:
        z_ref, gt_ref = take(2)
        hb = _gdn_front_h(x_ref, sh_ref, sc_ref, nw_ref)
        pq = jnp.dot(hb, wq_ref[...], preferred_element_type=F32)
        z_ref[...] = pq[:, CONV_DIM:].reshape(nb, tt, QK_DIM)
        gt_ref[...] = _gdn_front_gates(hb, wba_ref, alog_ref, dtb_ref, chunk).reshape(nb, tt, 3 * LANES)
        qkv_in = pq[:, :CONV_DIM].reshape(nb, tt, CONV_DIM)
    else:
        qkv_in = qkv_ref[...]
    hist = CONV_TAPS - 1
    pad = CONV_PAD

    @pl.when(pl.program_id(1) == 0)
    def _():
        xc[:, 0:pad - hist, :] = jnp.zeros((nb, pad - hist, CONV_DIM), F32)
        if has_state:
            s_ref[...] = s0_ref[...]
            xc[:, pad - hist:pad, :] = c0_ref[...]
        else:
            s_ref[...] = jnp.zeros(s_ref.shape, F32)
            xc[:, pad - hist:pad, :] = jnp.zeros((nb, hist, CONV_DIM), F32)

    xc[:, pad:pad + tt, :] = qkv_in
    cw = cw_ref[...]
    lead = SUBLANES

    def conv_rows(r0, r1):
        x_cur = xc[:, pad + r0 - lead:pad + r1, :]
        x_prev = xc[:, pad + r0 - lead - 1:pad + r1 - 1, :]
        z = x_cur * cw[1:2] + x_prev * cw[0:1]
        y = x_cur[:, lead:] * cw[3:4] + x_prev[:, lead:] * cw[2:3] + z[:, lead - 2:lead - 2 + r1 - r0]
        qkv_s[:, r0:r1, :] = _silu(y)
        for hd in range(N_HEADS):
            for part, scl in ((0, HEAD_DIM ** -0.5), (1, 1.0)):
                lanes = slice(part * QK_DIM + hd * HEAD_DIM, part * QK_DIM + (hd + 1) * HEAD_DIM)
                xh = qkv_s[:, r0:r1, lanes]
                xh = xh * (lax.rsqrt(jnp.sum(xh * xh, axis=-1, keepdims=True) + EPS) * scl)
                qkv_s[:, r0:r1, lanes] = xh


    per_stack = MXU_DIM // chunk
    units_all = [(s, hd) for s in range(nb) for hd in range(N_HEADS)]
    assert len(units_all) % per_stack == 0
    stacks = [units_all[i:i + per_stack] for i in range(0, len(units_all), per_stack)]
    r = MXU_DIM
    ri = lax.broadcasted_iota(jnp.int32, (r, r), 0)
    ci = lax.broadcasted_iota(jnp.int32, (r, r), 1)
    same = (ri >> _log2(chunk)) == (ci >> _log2(chunk))
    tril = same & (ri >= ci)
    strict = same & (ri > ci)
    eye = ri == ci

    n_chunks = tt // chunk
    dot = lambda p, q: jnp.dot(p, q, preferred_element_type=F32)

    pairs = [(c, units) for c in range(n_chunks) for units in stacks]
    rows_done = 0
    for first in range(0, len(pairs), LOCKSTEP_STACKS):
        group = pairs[first:first + LOCKSTEP_STACKS]
        ns = len(group)
        out = slice(first, first + ns)
        rows_needed = (group[-1][0] + 1) * chunk
        if rows_needed > rows_done:
            conv_rows(rows_done, rows_needed)
            rows_done = rows_needed

        def gather(f):
            return jnp.stack([jnp.concatenate([f(s, hd, slice(c * chunk, (c + 1) * chunk)) for s, hd in units], axis=0)
                              for c, units in group], axis=0)

        qn = gather(lambda s, hd, rows: qkv_s[s, rows, hd * HEAD_DIM:(hd + 1) * HEAD_DIM])
        kn = gather(lambda s, hd, rows: qkv_s[s, rows, QK_DIM + hd * HEAD_DIM:QK_DIM + (hd + 1) * HEAD_DIM])
        v = gather(lambda s, hd, rows: qkv_s[s, rows, 2 * QK_DIM + hd * HEAD_DIM:2 * QK_DIM + (hd + 1) * HEAD_DIM])
        beta = gather(lambda s, hd, rows: _lane_col(gt_ref[s, rows, 0:LANES], hd))
        gc = gather(lambda s, hd, rows: _lane_col(gt_ref[s, rows, LANES:2 * LANES], N_HEADS + hd))
        gl = gather(lambda s, hd, rows: _lane_col(gt_ref[s, rows, 2 * LANES:3 * LANES], N_HEADS + hd))
        eg = jnp.exp(gc)
        kb = kn * beta
        gc_row = jnp.sum(jnp.where(eye, gc, 0.0), axis=1, keepdims=True)
        diff = gc - gc_row
        decay = jnp.exp(jnp.where(tril, diff, -jnp.inf))
        decay_strict = jnp.exp(jnp.where(strict, diff, -jnp.inf))
        q_inv = _inv_unit_lower(_bmm_nt(-kb, kn) * decay_strict, ri, ci, chunk)
        rhs = jnp.concatenate([v * beta, kb * eg], axis=2)
        uw = rhs + _bmm(q_inv, rhs)
        u_s[out] = uw[:, :, :HEAD_DIM]
        per_unit = lambda t: t.reshape(ns, per_stack, chunk, HEAD_DIM)
        wq_s[out] = jnp.concatenate([per_unit(uw[:, :, HEAD_DIM:]), per_unit(qn * eg)], axis=2).reshape(
            ns, 2 * r, HEAD_DIM).astype(BF16)
        qk_s[out] = (_bmm_nt(qn, kn) * decay).astype(BF16)
        kdt_s[out] = jnp.swapaxes(kn * jnp.exp(gl - gc), 1, 2).astype(BF16)

        _gdn_sequential(first, group, chunk, gt_ref, s_ref, o_s, u_s, wq_s, qk_s, kdt_s)

    last = xc[:, pad + tt - hist:pad + tt, :]
    cn_ref[...] = last
    xc[:, pad - hist:pad, :] = last

    on = on_ref[...]
    for hd in range(N_HEADS):
        sl = slice(hd * HEAD_DIM, (hd + 1) * HEAD_DIM)
        o_ref[:, :, sl] = _rms(o_s[:, :, sl], on) * _silu(z_ref[:, :, sl])


def _gdn_sequential(first, group, chunk, gt_ref, s_ref, o_s, u_s, wq_s, qk_s, kdt_s):
    dot = lambda p, q: jnp.dot(p, q, preferred_element_type=F32)
    unit_of_row = lax.broadcasted_iota(jnp.int32, (MXU_DIM, HEAD_DIM), 0) >> _log2(chunk)
    for idx, (c, units) in enumerate(group, start=first):
        rows = slice(c * chunk, (c + 1) * chunk)
        ws, qs = [], []
        for i, (s, hd) in enumerate(units):
            both = dot(wq_s[idx, 2 * i * chunk:2 * (i + 1) * chunk, :], s_ref[s, hd].astype(BF16))
            ws.append(both[:chunk])
            qs.append(both[chunk:])
        v_new = (u_s[idx] - jnp.concatenate(ws, axis=0)).astype(BF16)
        o = jnp.concatenate(qs, axis=0) + dot(qk_s[idx], v_new)
        kdt = kdt_s[idx]
        for i, (s, hd) in enumerate(units):
            o_s[s, rows, hd * HEAD_DIM:(hd + 1) * HEAD_DIM] = o[i * chunk:(i + 1) * chunk]
            g_last = jnp.exp(_lane_col(gt_ref[s, c * chunk:c * chunk + 1, 2 * LANES:3 * LANES], N_HEADS + hd))
            upd = dot(kdt, jnp.where(unit_of_row == i, v_new, jnp.zeros_like(v_new)))
            s_ref[s, hd] = s_ref[s, hd] * g_last + upd


def _gdn_delta(front, cw, onorm, s0, c0, *, dims, nb, tt, chunk, fused):
    b, l = dims
    has_state = s0 is not None
    n_stacks = nb * tt * N_HEADS // MXU_DIM
    tok = lambda w: pl.BlockSpec((nb, tt, w), lambda i, j: (i, j, 0))
    mod = lambda col: pl.BlockSpec((nb, 1, D_MODEL), lambda i, j: (i, 0, col))
    full = _resident
    st_spec = pl.BlockSpec((nb, N_HEADS, HEAD_DIM, HEAD_DIM), lambda i, j: (i, 0, 0, 0))
    cv_spec = pl.BlockSpec((nb, CONV_TAPS - 1, CONV_DIM), lambda i, j: (i, 0, 0))
    if fused:
        in_specs = [tok(D_MODEL), mod(0), mod(1)] + [full(a) for a in front[3:]]
    else:
        in_specs = [tok(CONV_DIM), tok(QK_DIM), tok(3 * LANES)]
    in_specs += [full(cw), full(onorm)]
    args = list(front) + [cw, onorm]
    if has_state:
        in_specs += [st_spec, cv_spec]
        args += [s0, c0]
    scratch = [pltpu.VMEM((nb, tt + CONV_PAD, CONV_DIM), F32),
               pltpu.VMEM((nb, tt, CONV_DIM), F32),
               pltpu.VMEM((nb, tt, QK_DIM), F32),
               pltpu.VMEM((n_stacks, MXU_DIM, HEAD_DIM), F32),
               pltpu.VMEM((n_stacks, 2 * MXU_DIM, HEAD_DIM), BF16),
               pltpu.VMEM((n_stacks, MXU_DIM, MXU_DIM), BF16),
               pltpu.VMEM((n_stacks, HEAD_DIM, MXU_DIM), BF16)]
    if fused:
        scratch += [pltpu.VMEM((nb, tt, QK_DIM), F32), pltpu.VMEM((nb, tt, 3 * LANES), F32)]
    return pl.pallas_call(
        functools.partial(_gdn_delta_body, nb=nb, tt=tt, chunk=chunk, has_state=has_state, fused=fused),
        grid=(b // nb, l // tt),
        in_specs=in_specs,
        out_specs=[tok(QK_DIM), st_spec, cv_spec],
        out_shape=[jax.ShapeDtypeStruct((b, l, QK_DIM), F32),
                   jax.ShapeDtypeStruct((b, N_HEADS, HEAD_DIM, HEAD_DIM), F32),
                   jax.ShapeDtypeStruct((b, CONV_TAPS - 1, CONV_DIM), F32)],
        scratch_shapes=scratch,
        compiler_params=_params("parallel", "arbitrary",
                                fuse_inputs=[fused and i == 4 for i in range(len(args))]),
        name="gdn_delta",
    )(*args)


def _route(logits):
    lane = lax.broadcasted_iota(jnp.int32, logits.shape, 1)
    neg, big = -1e30, 1 << 20
    rmax = lambda t: jnp.max(t, axis=1, keepdims=True)
    rsum = lambda t: jnp.sum(t, axis=1, keepdims=True)
    first = lambda cond: jnp.min(jnp.where(cond, lane, big), axis=1, keepdims=True)
    is_g = lane < N_GROUPS
    lg = jnp.where(is_g, logits, neg)
    mg = rmax(lg)
    g_idx = first(is_g & (lg >= mg))
    p_group = 1.0 / rsum(jnp.where(is_g, jnp.exp(lg - mg), 0.0))
    lo = N_GROUPS + EXPERTS_PER_GROUP * g_idx
    in_g = (lane >= lo) & (lane < lo + EXPERTS_PER_GROUP)
    le = jnp.where(in_g, logits, neg)
    m1 = rmax(le)
    se = rsum(jnp.where(in_g, jnp.exp(le - m1), 0.0))
    i1 = first(in_g & (le >= m1))
    le2 = jnp.where(lane == i1, neg, le)
    m2 = rmax(le2)
    i2 = first(in_g & (lane != i1) & (le2 >= m2))
    p1 = 1.0 / se
    p2 = jnp.exp(m2 - m1) / se
    tot = p1 + p2
    local = (jnp.where(lane == i1 - lo, p1 / tot * p_group, 0.0)
             + jnp.where(lane == i2 - lo, p2 / tot * p_group, 0.0))
    return g_idx, local


META_GROUP, META_RANK = 4, 5
PIECE_LANES = 8
SORT_PAD = 16
EXPERT_ROWS = 256
EXPERT_STEP = 32
SORT_ROWS = MOE_ROWS + N_GROUPS * SORT_PAD + EXPERT_ROWS
USED_ROWS = MOE_ROWS + LANES
MOE_HALVES = 2


def _moe_route_body(*refs, nb, tt, has_proj):
    it = iter(refs)
    x_ref = next(it)
    if has_proj:
        y_ref, wo_ref, gm_ref = next(it), next(it), next(it)
    sh_ref, sc_ref, nw_ref, wr_ref, br_ref, x1_ref, h_ref, meta_ref, cnt_ref = (next(it) for _ in range(9))
    r = nb * tt
    dot = lambda p, q: jnp.dot(p, q, preferred_element_type=F32)

    x = x_ref[...].reshape(nb, tt, D_MODEL)
    if has_proj:
        x = x + gm_ref[...] * _mm(y_ref[...].reshape(r, D_MODEL), wo_ref[...]).reshape(nb, tt, D_MODEL)
    h = (_rms(x, nw_ref[...]) * (1.0 + sc_ref[...]) + sh_ref[...]).reshape(r, D_MODEL)
    x1_ref[...] = x.reshape(r, D_MODEL)
    h_hi, h_lo = _split_bf16(h, 2)
    h_ref[...] = h_hi
    w_hi, w_lo = _split_bf16(wr_ref[...], 2)
    logits = dot(h_hi, w_hi) + (dot(h_hi, w_lo) + dot(h_lo, w_hi)) + br_ref[...]
    g_idx, local = _route(logits)

    lane = lax.broadcasted_iota(jnp.int32, (r, LANES), 1)
    onehot = jnp.where(lane == g_idx, 1.0, 0.0)
    ri = lax.broadcasted_iota(jnp.int32, (r, r), 0)
    ci = lax.broadcasted_iota(jnp.int32, (r, r), 1)
    earlier = jnp.where(ri > ci, 1.0, 0.0).astype(BF16)
    rank = jnp.sum(onehot * dot(earlier, onehot.astype(BF16)), axis=1, keepdims=True)
    meta_ref[...] = jnp.where(lane == META_GROUP, g_idx.astype(F32),
                              jnp.where(lane == META_RANK, rank, local))
    counts = jnp.sum(onehot, axis=0, keepdims=True).astype(jnp.int32)
    cnt_ref[...] = jnp.broadcast_to(counts, (1, SUBLANES, LANES))


def _moe_expert_body(cnt_ref, *refs, nb, tt, n_time_tiles, final_norm):
    it = iter(refs)
    x1_ref, h_ref, meta_ref, gf_ref, w1_ref, w3_ref, w2_ref = (next(it) for _ in range(7))
    if final_norm:
        nf_ref = next(it)
    out_ref, hs_s, ms_s, acc_s = (next(it) for _ in range(4))
    r = MOE_ROWS
    dot = lambda p, q: jnp.dot(p, q, preferred_element_type=F32)
    step = pl.program_id(0) * n_time_tiles + pl.program_id(1)
    grp = pl.program_id(2)

    def layout(half):
        counts = [cnt_ref[MOE_HALVES * step + half, g] for g in range(N_GROUPS)]
        starts, nxt = [], 0
        for g in range(N_GROUPS):
            starts.append(nxt)
            nxt = nxt + (((counts[g] + (SORT_PAD - 1)) >> _log2(SORT_PAD)) << _log2(SORT_PAD))
        return counts, starts

    def positions(half, starts):
        meta = meta_ref[half * r:(half + 1) * r, :]
        g_col = _lane_col(meta, META_GROUP)
        pos = _lane_col(meta, META_RANK)
        for g in range(N_GROUPS):
            pos = pos + jnp.where(g_col == g, jnp.asarray(starts[g], jnp.int32).astype(F32), 0.0)
        return meta, pos

    @pl.when(grp == 0)
    def _():
        for half in range(MOE_HALVES):
            meta, pos = positions(half, layout(half)[1])
            pos_row = jnp.broadcast_to(pos, (r, LANES)).T[0:1, :].astype(jnp.int32)
            perm = jnp.where(lax.broadcasted_iota(jnp.int32, (USED_ROWS, r), 0) == pos_row, 1.0, 0.0).astype(BF16)
            hs_s[half, 0:USED_ROWS] = dot(perm, h_ref[half * r:(half + 1) * r, :]).astype(BF16)
            hs_s[half, USED_ROWS:] = jnp.zeros((SORT_ROWS - USED_ROWS, D_MODEL), BF16)
            lane = lax.broadcasted_iota(jnp.int32, (r, LANES), 1)
            pieces = _split_bf16(jnp.where(lane < EXPERTS_PER_GROUP, meta, 0.0), 3)
            packed = pieces[0].astype(F32)
            for k in (1, 2):
                packed = packed + pltpu.roll(pieces[k].astype(F32), k * PIECE_LANES, 1)
            ms = dot(perm, packed.astype(BF16))
            ms_s[half, 0:USED_ROWS] = (ms + pltpu.roll(ms, LANES - PIECE_LANES, 1)
                                       + pltpu.roll(ms, LANES - 2 * PIECE_LANES, 1))
            ms_s[half, USED_ROWS:] = jnp.zeros((SORT_ROWS - USED_ROWS, LANES), F32)
        acc_s[...] = jnp.zeros(acc_s.shape, F32)

    pick = lambda vals: functools.reduce(lambda acc, gv: jnp.where(grp == gv[0], gv[1], acc),
                                         list(enumerate(vals))[1:], jnp.asarray(vals[0], jnp.int32))
    lay = [layout(half) for half in range(MOE_HALVES)]
    cnt_g = [pick(c) for c, _ in lay]
    start_g = [pick(s) for _, s in lay]
    def block(first, size):
        base = [jnp.minimum(s + first, SORT_ROWS - size) for s in start_g]
        rows = [pl.ds(pl.multiple_of(bs, SORT_PAD), size) for bs in base]
        iota = lax.broadcasted_iota(jnp.int32, (size, 1), 0)
        hb = jnp.concatenate([hs_s[half, rows[half], :] for half in range(MOE_HALVES)], axis=0)
        mb = []
        for half in range(MOE_HALVES):
            seg_row = iota + (base[half] - start_g[half])
            mb.append(jnp.where((seg_row >= first) & (seg_row < cnt_g[half]), ms_s[half, rows[half], :], 0.0))
        mb = jnp.concatenate(mb, axis=0)
        hid = _silu(dot(hb, w1_ref[...])) * dot(hb, w3_ref[...])
        parts = [(hid[:, j * D_FF:(j + 1) * D_FF] * _lane_col(mb, j)).astype(BF16)
                 for j in range(EXPERTS_PER_GROUP)]
        res = dot(jnp.concatenate(parts, axis=1), w2_ref[...])
        for half in range(MOE_HALVES):
            acc_s[half, rows[half], :] += res[half * size:(half + 1) * size]

    longest = functools.reduce(jnp.maximum, cnt_g)
    n_full = longest // EXPERT_ROWS

    def full_block(k, carry):
        block(k * EXPERT_ROWS, EXPERT_ROWS)
        return carry

    lax.fori_loop(0, n_full, full_block, 0)
    tail = longest - n_full * EXPERT_ROWS
    for size in range(EXPERT_STEP, EXPERT_ROWS + 1, EXPERT_STEP):
        pl.when((tail > size - EXPERT_STEP) & (tail <= size))(
            functools.partial(block, n_full * EXPERT_ROWS, size))

    @pl.when(grp == N_GROUPS - 1)
    def _():
        moes = []
        for half in range(MOE_HALVES):
            _, pos = positions(half, layout(half)[1])
            unperm = jnp.where(lax.broadcasted_iota(jnp.int32, (r, USED_ROWS), 1) == pos.astype(jnp.int32),
                               1.0, 0.0).astype(BF16)
            a_hi, a_lo = _split_bf16(acc_s[half, 0:USED_ROWS, :], 2)
            moes.append(dot(unperm, a_hi) + dot(unperm, a_lo))
        moe = jnp.concatenate(moes, axis=0)
        out = x1_ref[...].reshape(nb, tt, D_MODEL) + gf_ref[...] * moe.reshape(nb, tt, D_MODEL)
        if final_norm:
            out = _rms(out, nf_ref[...])
        out_ref[...] = out.reshape(out_ref.shape)


def _moe(x, y, wo, mod_mix, mod_ffn, nw, wr, br, w1, w3, w2, nf, *, layer, dims, nb, tt,
         in_time_major, out_time_major):
    b, l = dims
    has_proj, final_norm = y is not None, nf is not None
    gb, gt = b // nb, l // tt
    r, n_tiles = nb * tt, gb * gt
    bm = pl.BlockSpec((nb, tt, D_MODEL), lambda i, j, *_: (i, j, 0))
    tm = pl.BlockSpec((tt, D_MODEL), lambda i, j, *_: (j, i))
    mod = lambda col: pl.BlockSpec((nb, 1, D_MODEL), lambda i, j, *_: (i, 0, col))
    flat = lambda w: pl.BlockSpec((r, w), lambda i, j, *_: (i * gt + j, 0))
    x_spec = tm if in_time_major else bm
    in_specs, args = [x_spec], [x]
    if has_proj:
        in_specs += [x_spec, _resident(wo), mod(2)]
        args += [y, wo, mod_mix]
    in_specs += [mod(0), mod(1)] + [_resident(a) for a in (nw, wr, br)]
    args += [mod_ffn, mod_ffn, nw, wr, br]
    x1, h, meta, cnt = pl.pallas_call(
        functools.partial(_moe_route_body, nb=nb, tt=tt, has_proj=has_proj),
        grid=(gb, gt),
        in_specs=in_specs,
        out_specs=[flat(D_MODEL), flat(D_MODEL), flat(LANES),
                   pl.BlockSpec((1, SUBLANES, LANES), lambda i, j: (i * gt + j, 0, 0))],
        out_shape=[jax.ShapeDtypeStruct((n_tiles * r, D_MODEL), F32),
                   jax.ShapeDtypeStruct((n_tiles * r, D_MODEL), BF16),
                   jax.ShapeDtypeStruct((n_tiles * r, LANES), F32),
                   jax.ShapeDtypeStruct((n_tiles, SUBLANES, LANES), jnp.int32)],
        compiler_params=_params("parallel", "parallel"),
        name="moe_route",
    )(*args)

    nb2, tt2 = (nb, MOE_HALVES * tt) if nb == 1 else (MOE_HALVES * nb, tt)
    gb2, gt2 = b // nb2, l // tt2
    assert r == MOE_ROWS and gb2 * gt2 * MOE_HALVES == n_tiles
    gw = EXPERTS_PER_GROUP * D_FF
    bm2 = pl.BlockSpec((nb2, tt2, D_MODEL), lambda i, j, g, *_: (i, j, 0))
    tm2 = pl.BlockSpec((tt2, D_MODEL), lambda i, j, g, *_: (j, i))
    flat2 = lambda w: pl.BlockSpec((MOE_HALVES * r, w), lambda i, j, g, *_: (i * gt2 + j, 0))
    in_specs = [flat2(D_MODEL), flat2(D_MODEL), flat2(LANES),
                pl.BlockSpec((nb2, 1, D_MODEL), lambda i, j, g, *_: (i, 0, 2)),
                pl.BlockSpec((None, D_MODEL, gw), lambda i, j, g, *_: (layer, 0, g)),
                pl.BlockSpec((None, D_MODEL, gw), lambda i, j, g, *_: (layer, 0, g)),
                pl.BlockSpec((None, gw, D_MODEL), lambda i, j, g, *_: (layer, g, 0))]
    args = [x1, h, meta, mod_ffn, w1, w3, w2]
    if final_norm:
        in_specs.append(_resident(nf))
        args.append(nf)
    if out_time_major:
        assert nb == 1
        out_spec, out_shape = tm2, jax.ShapeDtypeStruct((l, b * D_MODEL), F32)
    else:
        out_spec, out_shape = bm2, jax.ShapeDtypeStruct((b, l, D_MODEL), F32)
    return pl.pallas_call(
        functools.partial(_moe_expert_body, nb=nb2, tt=tt2, n_time_tiles=gt2, final_norm=final_norm),
        grid_spec=pltpu.PrefetchScalarGridSpec(
            num_scalar_prefetch=1,
            grid=(gb2, gt2, N_GROUPS),
            in_specs=in_specs,
            out_specs=out_spec,
            scratch_shapes=[pltpu.VMEM((MOE_HALVES, SORT_ROWS, D_MODEL), BF16),
                            pltpu.VMEM((MOE_HALVES, SORT_ROWS, LANES), F32),
                            pltpu.VMEM((MOE_HALVES, SORT_ROWS, D_MODEL), F32)]),
        out_shape=out_shape,
        compiler_params=_params("parallel", "parallel", "arbitrary"),
        name="moe_experts",
    )(cnt[:, 0, :N_GROUPS], *args)


def _s5_body(*refs, tt, nbs, has_state):
    if has_state:
        (x_ref, sh_ref, sc_ref, gm_ref, nw_ref, win_ref, wbr_ref, wbi_ref, wcr_ref, wci_ref, dsk_ref,
         wglu_ref, abr_ref, abi_ref, s0r_ref, s0i_ref, out_ref, sr_ref, si_ref, xr_s, xi_s) = refs
    else:
        (x_ref, sh_ref, sc_ref, gm_ref, nw_ref, win_ref, wbr_ref, wbi_ref, wcr_ref, wci_ref, dsk_ref,
         wglu_ref, abr_ref, abi_ref, out_ref, sr_ref, si_ref, xr_s, xi_s) = refs
    r = tt * nbs
    blocks = D_MODEL // MXU_DIM
    sw = SSM_STATE // blocks

    @pl.when(pl.program_id(1) == 0)
    def _():
        if has_state:
            sr_ref[...] = s0r_ref[...]
            si_ref[...] = s0i_ref[...]
        else:
            sr_ref[...] = jnp.zeros(sr_ref.shape, F32)
            si_ref[...] = jnp.zeros(si_ref.shape, F32)

    x = x_ref[...]
    h = (_rms(x, nw_ref[...]) * (1.0 + sc_ref[...]) + sh_ref[...]).reshape(r, D_MODEL)
    u = _mm(h, win_ref[...])
    ub = u.astype(BF16)
    for j in range(blocks):
        uj = ub[:, j * MXU_DIM:(j + 1) * MXU_DIM]
        xr_s[:, :, j * sw:(j + 1) * sw] = jnp.dot(
            uj, wbr_ref[j], preferred_element_type=F32).reshape(tt, nbs, sw)
        xi_s[:, :, j * sw:(j + 1) * sw] = jnp.dot(
            uj, wbi_ref[j], preferred_element_type=F32).reshape(tt, nbs, sw)

    for rg in range(nbs // SUBLANES):
        rows = slice(rg * SUBLANES, (rg + 1) * SUBLANES)
        for ch in range(SSM_STATE // SCAN_LANES):
            lanes = slice(ch * SCAN_LANES, (ch + 1) * SCAN_LANES)
            ar = jnp.broadcast_to(abr_ref[:, lanes], (SUBLANES, SCAN_LANES))
            ai = jnp.broadcast_to(abi_ref[:, lanes], (SUBLANES, SCAN_LANES))

            hr, hi = sr_ref[rows, lanes], si_ref[rows, lanes]
            for t in range(tt):
                hr, hi = (ar * hr - ai * hi + xr_s[t, rows, lanes],
                          ar * hi + ai * hr + xi_s[t, rows, lanes])
                xr_s[t, rows, lanes] = hr
                xi_s[t, rows, lanes] = hi
            sr_ref[rows, lanes] = hr
            si_ref[rows, lanes] = hi

    ys = []
    for j in range(blocks):
        st_r = xr_s[:, :, j * sw:(j + 1) * sw].reshape(r, sw)
        st_i = xi_s[:, :, j * sw:(j + 1) * sw].reshape(r, sw)
        ys.append(_mm(st_r, wcr_ref[j]) - _mm(st_i, wci_ref[j]))
    y = jnp.concatenate(ys, axis=1) + dsk_ref[...] * u
    ag = _mm(jax.nn.gelu(y), wglu_ref[...])
    mix = ag[:, :D_MODEL] * _sigmoid(ag[:, D_MODEL:])
    out_ref[...] = x + gm_ref[...] * mix.reshape(tt, nbs, D_MODEL)


def _s5(x, mod, nw, win, wbr, wbi, wcr, wci, dsk, wglu, abr, abi, s0r, s0i, *, tt, nbs):
    l, b, _ = x.shape
    has_state = s0r is not None
    xs = pl.BlockSpec((tt, nbs, D_MODEL), lambda i, j: (j, i, 0))
    mods = lambda col: pl.BlockSpec((nbs, D_MODEL), lambda i, j: (i, col))
    full = _resident
    st = pl.BlockSpec((nbs, SSM_STATE), lambda i, j: (i, 0))
    consts = [nw, win, wbr, wbi, wcr, wci, dsk, wglu, abr, abi]
    in_specs = [xs, mods(0), mods(1), mods(2)] + [full(a) for a in consts]
    args = [x, mod, mod, mod] + consts
    if has_state:
        in_specs += [st, st]
        args += [s0r, s0i]
    return pl.pallas_call(
        functools.partial(_s5_body, tt=tt, nbs=nbs, has_state=has_state),
        grid=(b // nbs, l // tt),
        in_specs=in_specs,
        out_specs=[xs, st, st],
        out_shape=[jax.ShapeDtypeStruct((l, b, D_MODEL), F32),
                   jax.ShapeDtypeStruct((b, SSM_STATE), F32),
                   jax.ShapeDtypeStruct((b, SSM_STATE), F32)],
        scratch_shapes=[pltpu.VMEM((tt, nbs, SSM_STATE), F32), pltpu.VMEM((tt, nbs, SSM_STATE), F32)],
        compiler_params=_params("parallel", "arbitrary",
                                fuse_inputs=[a is win or a is wglu for a in args]),
        name="s5",
    )(*args)


def _s5_discretize(lam_re, lam_im, log_dt, b_re, b_im, c_re, c_im):
    dt = jnp.exp(log_dt)[:, None]
    mag = jnp.exp(lam_re * dt)
    ang = lam_im * dt
    ab_re, ab_im = mag * jnp.cos(ang), mag * jnp.sin(ang)
    den = lam_re * lam_re + lam_im * lam_im
    f_re = ((ab_re - 1.0) * lam_re + ab_im * lam_im) / den
    f_im = (ab_im * lam_re - (ab_re - 1.0) * lam_im) / den
    bb_re = f_re[..., None] * b_re - f_im[..., None] * b_im
    bb_im = f_re[..., None] * b_im + f_im[..., None] * b_re
    blocks = D_MODEL // MXU_DIM
    gpb = SSM_GROUPS // blocks

    def diag_blocks(t, rows_per_group, cols_per_group):
        tiled = jnp.tile(t.reshape(blocks, gpb * rows_per_group, cols_per_group), (1, 1, gpb))
        rg = lax.broadcasted_iota(jnp.int32, tiled.shape, 1) // rows_per_group
        cg = lax.broadcasted_iota(jnp.int32, tiled.shape, 2) // cols_per_group
        return jnp.where(rg == cg, tiled, 0.0).astype(BF16)

    b_blocks = lambda bb: diag_blocks(jnp.swapaxes(bb, 1, 2), SSM_GROUP, SSM_P)
    c_blocks = lambda cc: diag_blocks(jnp.swapaxes(cc, 1, 2), SSM_P, SSM_GROUP)

    return (ab_re.reshape(1, SSM_STATE), ab_im.reshape(1, SSM_STATE),
            b_blocks(bb_re), b_blocks(bb_im), c_blocks(c_re), c_blocks(c_im))


def _trunk(x, mods_mix, mods_ffn, conv0, delta0, re0, im0, wts, *, nb_tok, tt_tok, nb_moe, tt_moe, nb_delta, tt_delta, chunk,
           tt_s5, nbs_s5):
    b, l, _ = x.shape
    mm3 = [m.reshape(b, 1, 3 * D_MODEL) for m in mods_mix]
    mf3 = [m.reshape(b, 1, 3 * D_MODEL) for m in mods_ffn]

    front = (x, mm3[0], mm3[0], wts['norm_mix'][0], wts['gdn_wq'], wts['gdn_wba'], wts['gdn_alog'], wts['gdn_dtb'])
    fused = nb_delta * tt_delta >= ROWS
    if not fused:
        front = _gdn_front(*front, nb=nb_tok, tt=tt_tok, chunk=chunk)
    o, s_delta, s_conv = _gdn_delta(front, wts['gdn_cw'], wts['gdn_onorm'], delta0, conv0, dims=(b, l),
                                    nb=nb_delta, tt=tt_delta, chunk=chunk, fused=fused)
    prompt_like = nb_tok == 1
    moe_w = lambda i: (wts['norm_ffn'][i], wts['moe_wr'][i], wts['moe_br'][i],
                       wts['moe_w1'], wts['moe_w3'], wts['moe_w2'])
    x1 = _moe(x, o, wts['gdn_wout'], mm3[0], mf3[0], *moe_w(0), None, layer=0, dims=(b, l), nb=nb_moe, tt=tt_moe,
              in_time_major=False, out_time_major=prompt_like)
    xt = x1.reshape(l, b, D_MODEL) if prompt_like else jnp.swapaxes(x1, 0, 1)
    x2, s_re, s_im = _s5(xt, mods_mix[1], wts['norm_mix'][1], wts['s5_win'], wts['s5_wbr'], wts['s5_wbi'],
                         wts['s5_wcr'], wts['s5_wci'], wts['s5_d'], wts['s5_wglu'], wts['s5_abr'], wts['s5_abi'],
                         re0, im0, tt=tt_s5, nbs=nbs_s5)
    x2 = x2.reshape(l, b * D_MODEL) if prompt_like else jnp.swapaxes(x2, 0, 1)
    y = _moe(x2, None, None, None, mf3[1], *moe_w(1), wts['norm_final'], layer=1, dims=(b, l), nb=nb_moe, tt=tt_moe,
             in_time_major=prompt_like, out_time_major=False)
    return (y, s_delta[None], s_conv[None],
            s_re.reshape(1, b, SSM_GROUPS, SSM_P), s_im.reshape(1, b, SSM_GROUPS, SSM_P))


def kernel(x_prompt, x_sample, state_delta, state_conv, state_ssm_re, state_ssm_im, c_prompt, c_sample, norm_mix, w_mod_mix, b_mod_mix, norm_ffn, w_mod_ffn, b_mod_ffn, norm_final, gdn_w_in, gdn_conv_w, gdn_a_log, gdn_dt_bias, gdn_o_norm, gdn_w_out, s5_w_in, s5_lam_re, s5_lam_im, s5_log_dt, s5_b_re, s5_b_im, s5_c_re, s5_c_im, s5_d, s5_w_glu, moe_w_rg, moe_b_rg, moe_w_re, moe_b_re, moe_w1, moe_w3, moe_w2):
    bp = x_prompt.shape[0]
    depth = norm_mix.shape[0]

    lane_pad = lambda a, lo: jnp.pad(a, [(0, 0)] * (a.ndim - 1) + [(lo, LANES - lo - a.shape[-1])])
    abr, abi, wbr, wbi, wcr, wci = _s5_discretize(s5_lam_re[0], s5_lam_im[0], s5_log_dt[0], s5_b_re[0],
                                                  s5_b_im[0], s5_c_re[0], s5_c_im[0])
    wts = dict(
        norm_mix=norm_mix.reshape(depth, 1, D_MODEL), norm_ffn=norm_ffn.reshape(depth, 1, D_MODEL),
        norm_final=norm_final.reshape(1, D_MODEL),
        gdn_wq=gdn_w_in[0, :, :CONV_DIM + QK_DIM].astype(BF16),
        gdn_wba=lane_pad(gdn_w_in[0, :, CONV_DIM + QK_DIM:], 0).astype(BF16),
        gdn_alog=lane_pad(gdn_a_log[0][None], N_HEADS), gdn_dtb=lane_pad(gdn_dt_bias[0][None], N_HEADS),
        gdn_cw=jnp.pad(gdn_conv_w[0], ((0, SUBLANES - CONV_TAPS), (0, 0))),
        gdn_onorm=gdn_o_norm[0][None], gdn_wout=gdn_w_out[0].astype(BF16),
        s5_win=s5_w_in[0].astype(BF16), s5_wbr=wbr, s5_wbi=wbi, s5_wcr=wcr, s5_wci=wci,
        s5_d=s5_d[0].reshape(1, D_MODEL), s5_wglu=s5_w_glu[0].astype(BF16), s5_abr=abr, s5_abi=abi,
        moe_wr=lane_pad(jnp.concatenate([moe_w_rg, moe_w_re], axis=-1), 0),
        moe_br=lane_pad(jnp.concatenate([moe_b_rg, moe_b_re], axis=-1), 0)[:, None, :],
        moe_w1=jnp.swapaxes(moe_w1, 1, 2).reshape(depth, D_MODEL, N_EXPERTS * D_FF).astype(BF16),
        moe_w3=jnp.swapaxes(moe_w3, 1, 2).reshape(depth, D_MODEL, N_EXPERTS * D_FF).astype(BF16),
        moe_w2=moe_w2.reshape(depth, N_EXPERTS * D_FF, D_MODEL).astype(BF16),
    )

    c_all = jnp.concatenate([c_prompt, c_sample], axis=0)
    m_mix = _ada_mod(c_all, w_mod_mix, b_mod_mix)
    m_ffn = _ada_mod(c_all, w_mod_ffn, b_mod_ffn)

    y_p, p_delta, p_conv, p_re, p_im = _trunk(
        x_prompt, [m_mix[i, :bp] for i in range(depth)], [m_ffn[i, :bp] for i in range(depth)],
        None, None, None, None, wts,
        nb_tok=1, tt_tok=ROWS, nb_moe=1, tt_moe=MOE_ROWS, nb_delta=1, tt_delta=ROWS, chunk=CHUNK, tt_s5=S5_ROWS // SUBLANES, nbs_s5=SUBLANES)
    ls = x_sample.shape[1]
    y_s, s_delta, s_conv, s_re, s_im = _trunk(
        x_sample, [m_mix[i, bp:] for i in range(depth)], [m_ffn[i, bp:] for i in range(depth)],
        state_conv[0], state_delta[0], state_ssm_re[0].reshape(-1, SSM_STATE),
        state_ssm_im[0].reshape(-1, SSM_STATE), wts,
        nb_tok=ROWS // ls, tt_tok=ls, nb_moe=MOE_ROWS // ls, tt_moe=ls, nb_delta=2 * MXU_DIM // (ls * N_HEADS), tt_delta=ls, chunk=ls,
        tt_s5=ls, nbs_s5=ROWS // ls)
    return (y_p, y_s, p_delta, p_conv, p_re, p_im, s_delta, s_conv, s_re, s_im)
```
